```python
import math
import jax
import jax.numpy as jnp
from jax import lax
import numpy as np

D_MODEL = 1024
BATCH = 8
SEQ = 2048
DEPTH = 1

N_HEADS = 16
N_KV_GROUPS = 4
HEADS_PER_GROUP = N_HEADS // N_KV_GROUPS
D_QK = 96
D_V = 64
CMP_BLOCK = 32
CMP_STRIDE = 16
CMP_HIDDEN = 256
SLC_BLOCK = 64
SLC_TOPN = 16
SLC_QUERY_BLOCK = 64
WINDOW = 512
WIN_QUERY_BLOCK = 128
ATTN_SCALE = D_QK ** -0.5
FORCE_BONUS = 1e6
N_BUCKETS = 32
MAX_DISTANCE = 128
CONV_CHANNELS = D_MODEL
CONV_WIDTH = 31
PEER_HEADS = 8
PEER_NKEYS = 128
PEER_EXPERTS = PEER_NKEYS * PEER_NKEYS
PEER_DQ = 256
PEER_TOPK = 16
PEER_TOKEN_BLOCK = 128
Q_COLS = N_HEADS * D_QK
K_COLS = N_KV_GROUPS * D_QK
V_COLS = N_KV_GROUPS * D_V
NSA_GATE_COLS = 3 * N_HEADS
GLU_COLS = 2 * CONV_CHANNELS
MERGE_COLS = 2 * D_MODEL
IN_SPLITS = (Q_COLS, K_COLS, K_COLS, K_COLS, V_COLS, V_COLS, V_COLS, NSA_GATE_COLS, GLU_COLS, MERGE_COLS)
IN_COLS = sum(IN_SPLITS)
NORM_EPS = 1e-6
NEG_INF = -1e30

kernel_name = 'hybrid_nsa_conformer_peer_block'


def rmsnorm(x, g):
    xf = x.astype(jnp.float32)
    y = xf * lax.rsqrt(jnp.mean(xf * xf, axis=-1, keepdims=True) + NORM_EPS)
    return (y * g).astype(x.dtype)


def layernorm(x, g, b):
    xf = x.astype(jnp.float32)
    xc = xf - jnp.mean(xf, axis=-1, keepdims=True)
    y = xc * lax.rsqrt(jnp.mean(xc * xc, axis=-1, keepdims=True) + NORM_EPS)
    return (y * g + b).astype(x.dtype)


def t5_bucket(dist):
    n = jnp.maximum(dist, 0)
    max_exact = N_BUCKETS // 2
    nf = jnp.maximum(n, 1).astype(jnp.float32)
    large = max_exact + (jnp.log(nf / max_exact) / math.log(MAX_DISTANCE / max_exact) * (N_BUCKETS - max_exact)).astype(jnp.int32)
    return jnp.where(n < max_exact, n, jnp.minimum(large, N_BUCKETS - 1))


def head_bias(rel_table, dist):
    b = jnp.moveaxis(rel_table[t5_bucket(dist)], -1, 0)
    return b.reshape((N_KV_GROUPS, HEADS_PER_GROUP) + dist.shape).astype(jnp.float32)


def masked_softmax(logits, mask):
    p = jax.nn.softmax(jnp.where(mask, logits, NEG_INF), axis=-1)
    return jnp.where(mask, p, 0.0)


def compress_kv(kv, w1, w2, pos_emb):
    B, G, S, d = kv.shape
    n_sub = CMP_BLOCK // CMP_STRIDE
    n_chunk = S // CMP_STRIDE
    n_cmp = n_chunk - n_sub + 1
    chunks = kv.reshape(B, G, n_chunk, CMP_STRIDE, d)
    blocks = jnp.concatenate([chunks[:, :, i:i + n_cmp] for i in range(n_sub)], axis=3) + pos_emb
    return jax.nn.gelu(blocks.reshape(B, G, n_cmp, CMP_BLOCK * d) @ w1) @ w2


def nsa_attention(q, k_cmp, k_slc, k_win, v_cmp, v_slc, v_win, gate_logits, rel_table, w1k, w2k, pos_k, w1v, w2v, pos_v):
    B, G, R, S, _ = q.shape
    t = jnp.arange(S)

    kc = compress_kv(k_cmp, w1k, w2k, pos_k)
    vc = compress_kv(v_cmp, w1v, w2v, pos_v)
    n_cmp = kc.shape[2]
    cmp_start = jnp.arange(n_cmp) * CMP_STRIDE
    cmp_end = cmp_start + CMP_BLOCK - 1
    dist_c = t[:, None] - cmp_end[None, :]
    logits = jnp.einsum('bgrsd,bgnd->bgrsn', q, kc).astype(jnp.float32) * ATTN_SCALE + head_bias(rel_table, dist_c)
    p_cmp = masked_softmax(logits, dist_c >= 0)
    o_cmp = jnp.einsum('bgrsn,bgnd->bgrsd', p_cmp.astype(vc.dtype), vc)

    n_slc = S // SLC_BLOCK
    n_sel = min(SLC_TOPN, n_slc)
    slc_start = jnp.arange(n_slc) * SLC_BLOCK
    overlap = ((cmp_start[:, None] <= slc_start[None, :] + SLC_BLOCK - 1) & (cmp_end[:, None] >= slc_start[None, :])).astype(jnp.float32)
    imp = jnp.einsum('bgsn,nj->bgsj', p_cmp.sum(axis=2), overlap)
    j = jnp.arange(n_slc)[None, :]
    cur = (t // SLC_BLOCK)[:, None]
    forced = (j == 0) | (j == cur) | (j == cur - 1)
    score = jnp.where(j <= cur, imp + jnp.where(forced, FORCE_BONUS, 0.0), NEG_INF)
    _, sel = lax.top_k(score, n_sel)

    ks_blocks = k_slc.reshape(B, G, n_slc, SLC_BLOCK, D_QK)
    vs_blocks = v_slc.reshape(B, G, n_slc, SLC_BLOCK, D_V)
    nqs = S // SLC_QUERY_BLOCK
    q_s = q.reshape(B, G, R, nqs, SLC_QUERY_BLOCK, D_QK).transpose(0, 3, 1, 2, 4, 5).reshape(B * nqs, G, R, SLC_QUERY_BLOCK, D_QK)
    sel_s = sel.reshape(B, G, nqs, SLC_QUERY_BLOCK, n_sel).transpose(0, 2, 1, 3, 4).reshape(B * nqs, G, SLC_QUERY_BLOCK, n_sel)
    b_ids = jnp.repeat(jnp.arange(B), nqs)
    s_ids = jnp.tile(jnp.arange(nqs) * SLC_QUERY_BLOCK, B)
    table_gr = rel_table.reshape(N_BUCKETS, G, R).transpose(1, 0, 2)
    in_blk = jnp.arange(SLC_BLOCK)
    q_off = jnp.arange(SLC_QUERY_BLOCK)
    n_keys = n_sel * SLC_BLOCK

    def selected_block(args):
        b, s0, qb, ib = args
        kg = jax.vmap(lambda kk, ii: kk[ii])(ks_blocks[b], ib).reshape(G, SLC_QUERY_BLOCK, n_keys, D_QK)
        vg = jax.vmap(lambda vv, ii: vv[ii])(vs_blocks[b], ib).reshape(G, SLC_QUERY_BLOCK, n_keys, D_V)
        kpos = (ib[..., None] * SLC_BLOCK + in_blk).reshape(G, SLC_QUERY_BLOCK, n_keys)
        dist = (s0 + q_off)[None, :, None] - kpos
        bias = jax.vmap(lambda tb, bk: jnp.take(tb, bk, axis=0))(table_gr, t5_bucket(dist))
        logits = jnp.einsum('grqd,gqkd->grqk', qb, kg).astype(jnp.float32) * ATTN_SCALE + bias.transpose(0, 3, 1, 2).astype(jnp.float32)
        p = masked_softmax(logits, (dist >= 0)[:, None])
        return jnp.einsum('grqk,gqkd->grqd', p.astype(vg.dtype), vg)

    o_slc = lax.map(selected_block, (b_ids, s_ids, q_s, sel_s))
    o_slc = o_slc.reshape(B, nqs, G, R, SLC_QUERY_BLOCK, D_V).transpose(0, 2, 3, 1, 4, 5).reshape(B, G, R, S, D_V)

    span = WIN_QUERY_BLOCK + WINDOW
    pad = ((0, 0), (0, 0), (WINDOW, 0), (0, 0))
    kp = jnp.pad(k_win, pad)
    vp = jnp.pad(v_win, pad)
    qi = jnp.arange(WIN_QUERY_BLOCK)[:, None]
    kj = jnp.arange(span)[None, :]
    dist_w = WINDOW + qi - kj
    band = (dist_w >= 0) & (dist_w < WINDOW)
    bias_w = head_bias(rel_table, dist_w)
    nqw = S // WIN_QUERY_BLOCK
    q_w = q.reshape(B, G, R, nqw, WIN_QUERY_BLOCK, D_QK).transpose(3, 0, 1, 2, 4, 5)

    def window_block(args):
        qb, s0 = args
        kb = lax.dynamic_slice_in_dim(kp, s0, span, axis=2)
        vb = lax.dynamic_slice_in_dim(vp, s0, span, axis=2)
        logits = jnp.einsum('bgrqd,bgkd->bgrqk', qb, kb).astype(jnp.float32) * ATTN_SCALE + bias_w
        p = masked_softmax(logits, band & (kj >= WINDOW - s0))
        return jnp.einsum('bgrqk,bgkd->bgrqd', p.astype(vb.dtype), vb)

    o_win = lax.map(window_block, (q_w, jnp.arange(nqw) * WIN_QUERY_BLOCK))
    o_win = o_win.transpose(1, 2, 3, 0, 4, 5).reshape(B, G, R, S, D_V)

    g = jax.nn.sigmoid(gate_logits.astype(jnp.float32)).reshape(B, S, 3, G, R).transpose(2, 0, 3, 4, 1)[..., None]
    o = g[0] * o_cmp + g[1] * o_slc + g[2] * o_win
    return o.transpose(0, 3, 1, 2, 4).reshape(B, S, N_HEADS * D_V).astype(q.dtype)


def conformer_conv(glu_in, w_dw, b_dw, ln_g, ln_b):
    a, gate = jnp.split(glu_in, 2, axis=-1)
    u = a * jax.nn.sigmoid(gate)
    y = lax.conv_general_dilated(u, w_dw, window_strides=(1,), padding=[(CONV_WIDTH - 1, 0)],
                                 dimension_numbers=('NWC', 'WIO', 'NWC'), feature_group_count=CONV_CHANNELS) + b_dw
    return jax.nn.silu(layernorm(y, ln_g, ln_b))


def to_groups(z, d):
    B, S, _ = z.shape
    return z.reshape(B, S, N_KV_GROUPS, d).transpose(0, 2, 1, 3)


def token_mixer(h, w_in, rel_table, w1k, w2k, pos_k, w1v, w2v, pos_v, w_attn_out, conv_w, conv_b, conv_ln_g, conv_ln_b, w_conv_out, w_out):
    B, S, _ = h.shape
    offsets = [int(o) for o in np.cumsum(IN_SPLITS)[:-1]]
    q, k_c, k_s, k_w, v_c, v_s, v_w, g_nsa, glu_in, g_merge = jnp.split(h @ w_in, offsets, axis=-1)
    q = q.reshape(B, S, N_KV_GROUPS, HEADS_PER_GROUP, D_QK).transpose(0, 2, 3, 1, 4)
    attn = nsa_attention(q, to_groups(k_c, D_QK), to_groups(k_s, D_QK), to_groups(k_w, D_QK),
                         to_groups(v_c, D_V), to_groups(v_s, D_V), to_groups(v_w, D_V), g_nsa,
                         rel_table, w1k, w2k, pos_k, w1v, w2v, pos_v)
    y_a = attn @ w_attn_out
    y_b = conformer_conv(glu_in, conv_w, conv_b, conv_ln_g, conv_ln_b) @ w_conv_out
    g_a, g_b = jnp.split(g_merge, 2, axis=-1)
    y = jax.nn.sigmoid(g_a) * y_a + jax.nn.sigmoid(g_b) * y_b
    return y @ w_out


def peer_ffn(h, w_q, sub_keys, u, v):
    B, S, D = h.shape
    T = B * S
    hf = h.reshape(T, D)
    q = (hf @ w_q).reshape(T, PEER_HEADS, 2, PEER_DQ // 2)
    s = jnp.einsum('thcd,hckd->thck', q, sub_keys).astype(jnp.float32)
    s1, i1 = lax.top_k(s[:, :, 0], PEER_TOPK)
    s2, i2 = lax.top_k(s[:, :, 1], PEER_TOPK)
    cand = (s1[..., :, None] + s2[..., None, :]).reshape(T, PEER_HEADS, PEER_TOPK * PEER_TOPK)
    cand_idx = (i1[..., :, None] * PEER_NKEYS + i2[..., None, :]).reshape(T, PEER_HEADS, PEER_TOPK * PEER_TOPK)
    top_s, pos = lax.top_k(cand, PEER_TOPK)
    expert_idx = jnp.take_along_axis(cand_idx, pos, axis=-1)
    weights = jax.nn.softmax(top_s, axis=-1)
    n_blk = T // PEER_TOKEN_BLOCK
    n_e = PEER_HEADS * PEER_TOPK
    xs = (hf.reshape(n_blk, PEER_TOKEN_BLOCK, D),
          expert_idx.reshape(n_blk, PEER_TOKEN_BLOCK, n_e),
          weights.reshape(n_blk, PEER_TOKEN_BLOCK, n_e))

    def token_block(args):
        hc, ec, wc = args
        act = jax.nn.gelu(jnp.einsum('cd,ced->ce', hc, u[ec]))
        coef = (act.astype(jnp.float32) * wc).astype(hc.dtype)
        return jnp.einsum('ce,ced->cd', coef, v[ec])

    return lax.map(token_block, xs).reshape(B, S, D)


def setup_inputs(seed: int = 0) -> dict:
    key = jax.random.key(seed)
    ks = jax.random.split(key, 27)
    f32 = jnp.float32
    L = DEPTH
    D = D_MODEL

    def nrm(k, shape, scale):
        return jax.random.normal(k, shape, f32) * scale

    return {
        'x': nrm(ks[0], (BATCH, SEQ, D), 1.0),
        'c': nrm(ks[1], (BATCH, D), 1.0),
        'w_ada': nrm(ks[2], (L, D, 6 * D), 0.5 * D ** -0.5),
        'b_ada': nrm(ks[3], (L, 6 * D), 0.01),
        'g_pre_mix': 1.0 + nrm(ks[4], (L, D), 0.01),
        'g_post_mix': 1.0 + nrm(ks[5], (L, D), 0.01),
        'g_pre_ffn': 1.0 + nrm(ks[6], (L, D), 0.01),
        'g_post_ffn': 1.0 + nrm(ks[7], (L, D), 0.01),
        'rel_table': nrm(ks[8], (N_BUCKETS, N_HEADS), 0.5),
        'w_in': nrm(ks[9], (L, D, IN_COLS), D ** -0.5),
        'cmp_w1k': nrm(ks[10], (L, CMP_BLOCK * D_QK, CMP_HIDDEN), (CMP_BLOCK * D_QK) ** -0.5),
        'cmp_w2k': nrm(ks[11], (L, CMP_HIDDEN, D_QK), CMP_HIDDEN ** -0.5),
        'cmp_pos_k': nrm(ks[12], (L, CMP_BLOCK, D_QK), 0.1),
        'cmp_w1v': nrm(ks[13], (L, CMP_BLOCK * D_V, CMP_HIDDEN), (CMP_BLOCK * D_V) ** -0.5),
        'cmp_w2v': nrm(ks[14], (L, CMP_HIDDEN, D_V), CMP_HIDDEN ** -0.5),
        'cmp_pos_v': nrm(ks[15], (L, CMP_BLOCK, D_V), 0.1),
        'w_attn_out': nrm(ks[16], (L, N_HEADS * D_V, D), (N_HEADS * D_V) ** -0.5),
        'conv_w': nrm(ks[17], (L, CONV_WIDTH, 1, CONV_CHANNELS), CONV_WIDTH ** -0.5),
        'conv_b': nrm(ks[18], (L, CONV_CHANNELS), 0.01),
        'conv_ln_g': 1.0 + nrm(ks[19], (L, CONV_CHANNELS), 0.01),
        'conv_ln_b': nrm(ks[20], (L, CONV_CHANNELS), 0.01),
        'w_conv_out': nrm(ks[21], (L, CONV_CHANNELS, D), CONV_CHANNELS ** -0.5),
        'w_out': nrm(ks[22], (L, D, D), D ** -0.5),
        'peer_wq': nrm(ks[23], (L, D, PEER_HEADS * PEER_DQ), D ** -0.5),
        'peer_subkeys': nrm(ks[24], (L, PEER_HEADS, 2, PEER_NKEYS, PEER_DQ // 2), (PEER_DQ // 2) ** -0.5),
        'peer_u': nrm(ks[25], (L, PEER_EXPERTS, D), D ** -0.5),
        'peer_v': nrm(ks[26], (L, PEER_EXPERTS, D), 1.0),
    }


def reference(x, c, w_ada, b_ada, g_pre_mix, g_post_mix, g_pre_ffn, g_post_ffn, rel_table, w_in,
              cmp_w1k, cmp_w2k, cmp_pos_k, cmp_w1v, cmp_w2v, cmp_pos_v, w_attn_out,
              conv_w, conv_b, conv_ln_g, conv_ln_b, w_conv_out, w_out,
              peer_wq, peer_subkeys, peer_u, peer_v):
    c_act = jax.nn.silu(c)
    for layer in range(DEPTH):
        mod = (c_act @ w_ada[layer] + b_ada[layer])[:, None, :]
        sh1, sc1, gt1, sh2, sc2, gt2 = jnp.split(mod, 6, axis=-1)
        h = rmsnorm(x, g_pre_mix[layer]) * (1.0 + sc1) + sh1
        y = token_mixer(h, w_in[layer], rel_table, cmp_w1k[layer], cmp_w2k[layer], cmp_pos_k[layer],
                        cmp_w1v[layer], cmp_w2v[layer], cmp_pos_v[layer], w_attn_out[layer],
                        conv_w[layer], conv_b[layer], conv_ln_g[layer], conv_ln_b[layer],
                        w_conv_out[layer], w_out[layer])
        x = x + gt1 * rmsnorm(y, g_post_mix[layer])
        h = rmsnorm(x, g_pre_ffn[layer]) * (1.0 + sc2) + sh2
        y = peer_ffn(h, peer_wq[layer], peer_subkeys[layer], peer_u[layer], peer_v[layer])
        x = x + gt2 * rmsnorm(y, g_post_ffn[layer])
    return x
```

```python
import functools
import math

import numpy as np
import jax
import jax.numpy as jnp
from jax import lax
from jax.experimental import pallas as pl
from jax.experimental.pallas import tpu as pltpu

F32 = jnp.float32
BF16 = jnp.bfloat16

D_MODEL = 1024
N_HEADS = 16
N_GROUPS = 4
HEADS_PER_GROUP = N_HEADS // N_GROUPS
D_QK = 96
D_V = 64
CMP_BLOCK = 32
CMP_STRIDE = 16
CMP_HIDDEN = 256
SLC_BLOCK = 64
SLC_TOPN = 16
WINDOW = 512
ATTN_SCALE = D_QK ** -0.5
FORCE_BONUS = 1e6
N_BUCKETS = 32
MAX_DISTANCE = 128
CONV_WIDTH = 31
PEER_HEADS = 8
PEER_NKEYS = 128
PEER_DQ = 256
PEER_TOPK = 16
NORM_EPS = 1e-6
NEG_INF = -1e30

LANE = 128
VMEM_LIMIT = 56 * 1024 * 1024

TQ = 128
SLC_CHUNK = 256
WIN_SPAN = TQ + WINDOW
PROJ_TM = 2048
PROJ_TN = 512
TAIL_TM = 256
HALO = 32
ROUTE_TL = 256
EXP_TL = 512
EXP_TE = 512

QKV_Q0, QKV_KC0, QKV_KS0, QKV_KW0, QKV_VC0, QKV_VS0, QKV_VW0 = 0, 16, 20, 24, 28, 32, 36
QKV_HEADS = 40
REST_GLU_A, REST_GLU_G, REST_MRG_A, REST_MRG_B = 0, 1, 2, 3
REST_GATE0 = 4 * D_MODEL
REST_COLS = 4 * D_MODEL + N_GROUPS * LANE

NT_DIMS = (((1,), (1,)), ((), ()))


def _cparams(*sem):
    return pltpu.CompilerParams(dimension_semantics=sem, vmem_limit_bytes=VMEM_LIMIT)


def _gelu(x):
    return 0.5 * x * (1.0 + jnp.tanh(math.sqrt(2.0 / math.pi) * (x + 0.044715 * (x * x * x))))


def _sigmoid(x):
    return 1.0 / (1.0 + jnp.exp(-x))


def _rms(x, g):
    return x * lax.rsqrt(jnp.mean(x * x, axis=-1, keepdims=True) + NORM_EPS) * g


def _ada_kernel(c_ref, w_ref, b_ref, o_ref):
    c = c_ref[...]
    c_act = c * _sigmoid(c)
    o_ref[...] = jnp.dot(c_act, w_ref[...], preferred_element_type=F32,
                         precision=lax.Precision.HIGHEST) + b_ref[...]


def _ada(c, w, b):
    bsz, d = c.shape
    n = w.shape[1]
    tn = 1024
    return pl.pallas_call(
        _ada_kernel,
        grid=(n // tn,),
        in_specs=[pl.BlockSpec((bsz, d), lambda j: (0, 0)),
                  pl.BlockSpec((d, tn), lambda j: (0, j)),
                  pl.BlockSpec((1, tn), lambda j: (0, j))],
        out_specs=pl.BlockSpec((bsz, tn), lambda j: (0, j)),
        out_shape=jax.ShapeDtypeStruct((bsz, n), F32),
        compiler_params=_cparams("parallel"),
        name="ada",
    )(c, w, b.reshape(1, n))


def _t5_bucket_np(dist):
    n = np.maximum(dist, 0)
    max_exact = N_BUCKETS // 2
    nf = np.maximum(n, 1).astype(np.float64)
    large = max_exact + (np.log(nf / max_exact) / math.log(MAX_DISTANCE / max_exact)
                         * (N_BUCKETS - max_exact)).astype(np.int64)
    return np.where(n < max_exact, n, np.minimum(large, N_BUCKETS - 1)).astype(np.int32)


def _bias_kernel(tab_ref, bkt_ref, o_ref):
    h = pl.program_id(0)
    bkt = bkt_ref[...]
    acc = jnp.where(bkt < 0, NEG_INF, 0.0).astype(F32)
    for i in range(N_BUCKETS):
        acc = jnp.where(bkt == i, tab_ref[i, h], acc)
    o_ref[0] = acc


def _bias_table(rel_table, bucket):
    rows, cols = bucket.shape
    return pl.pallas_call(
        _bias_kernel,
        grid=(N_HEADS,),
        in_specs=[pl.BlockSpec(memory_space=pltpu.SMEM),
                  pl.BlockSpec((rows, cols), lambda h: (0, 0))],
        out_specs=pl.BlockSpec((1, rows, cols), lambda h: (h, 0, 0)),
        out_shape=jax.ShapeDtypeStruct((N_HEADS, rows, cols), F32),
        compiler_params=_cparams("parallel"),
        name="bias_table",
    )(rel_table, jnp.asarray(bucket))


def _static_buckets(seq):
    t = np.arange(seq)[:, None]
    n = np.arange(LANE)[None, :]
    dist_c = t - (n * CMP_STRIDE + CMP_BLOCK - 1)
    bkt_c = np.where(dist_c >= 0, _t5_bucket_np(dist_c), -1).astype(np.int32)
    qi = np.arange(TQ)[:, None]
    kj = np.arange(TQ)[None, :]
    tiles = [_t5_bucket_np(d * TQ + qi - kj) for d in range(3)]
    bkt_t = np.concatenate(tiles, axis=0).astype(np.int32)
    kw = np.arange(WIN_SPAN)[None, :]
    dist_w = WINDOW + qi - kw
    band = (dist_w >= 0) & (dist_w < WINDOW)
    bkt_w = np.where(band, _t5_bucket_np(dist_w), -1).astype(np.int32)
    return bkt_c, bkt_t, bkt_w


def _proj_kernel(x_ref, mod_ref, g_ref, w_ref, o_ref, h_ref, *, heads_out):
    @pl.when(pl.program_id(1) == 0)
    def _():
        x = x_ref[...]
        sh = mod_ref[0, 0:1, :]
        sc = mod_ref[0, 1:2, :]
        h_ref[...] = (_rms(x, g_ref[...]) * (1.0 + sc) + sh).astype(BF16)

    res = jnp.dot(h_ref[...], w_ref[...], preferred_element_type=F32).astype(o_ref.dtype)
    if heads_out:
        for k in range(PROJ_TN // LANE):
            o_ref[0, k] = res[:, k * LANE:(k + 1) * LANE]
    else:
        o_ref[...] = res


def _project(x2d, mod3, g, w, seq, heads_out):
    t, d = x2d.shape
    n = w.shape[1]
    bsz = t // seq
    tiles_per_seq = seq // PROJ_TM
    grid = (t // PROJ_TM, n // PROJ_TN)
    if heads_out:
        hpt = PROJ_TN // LANE
        out_shape = jax.ShapeDtypeStruct((bsz, n // LANE, seq, LANE), BF16)
        out_spec = pl.BlockSpec((1, hpt, PROJ_TM, LANE),
                                lambda i, j: (i // tiles_per_seq, j, i % tiles_per_seq, 0))
    else:
        out_shape = jax.ShapeDtypeStruct((t, n), BF16)
        out_spec = pl.BlockSpec((PROJ_TM, PROJ_TN), lambda i, j: (i, j))
    return pl.pallas_call(
        functools.partial(_proj_kernel, heads_out=heads_out),
        grid=grid,
        in_specs=[pl.BlockSpec((PROJ_TM, d), lambda i, j: (i, 0)),
                  pl.BlockSpec((1, 2, d), lambda i, j: (i // tiles_per_seq, 0, 0)),
                  pl.BlockSpec((1, d), lambda i, j: (0, 0)),
                  pl.BlockSpec((d, PROJ_TN), lambda i, j: (0, j))],
        out_specs=out_spec,
        out_shape=out_shape,
        scratch_shapes=[pltpu.VMEM((PROJ_TM, d), BF16)],
        compiler_params=_cparams("parallel", "arbitrary"),
        name="proj_heads" if heads_out else "proj_rest",
    )(x2d, mod3, g, w)


def _compress_kernel(kch_ref, vch_ref, pk_ref, pv_ref, w1k_ref, w2k_ref, w1v_ref, w2v_ref,
                     kc_ref, vc_ref):
    def one(ch_ref, pos_ref, w1_ref, w2_ref, o_ref):
        a = ch_ref[0, 0].astype(F32)
        a1 = (a + pos_ref[0:1, :]).astype(BF16)
        a2 = (a + pos_ref[1:2, :]).astype(BF16)
        p1 = jnp.dot(a1, w1_ref[0], preferred_element_type=F32)
        p2 = jnp.dot(a2, w1_ref[1], preferred_element_type=F32)
        n_chunk = p2.shape[0]
        hid = _gelu(p1 + pltpu.roll(p2, n_chunk - 1, 0))
        o_ref[0, 0] = jnp.dot(hid.astype(BF16), w2_ref[...],
                              preferred_element_type=F32).astype(BF16)

    one(kch_ref, pk_ref, w1k_ref, w2k_ref, kc_ref)
    one(vch_ref, pv_ref, w1v_ref, w2v_ref, vc_ref)


def _compress(kch, vch, pk, pv, w1k, w2k, w1v, w2v):
    bsz, g, n_chunk, width = kch.shape
    assert n_chunk == LANE, "compressed keys are laid out on one 128-row tile"
    spec_in = pl.BlockSpec((1, 1, n_chunk, width), lambda b, gg: (b, gg, 0, 0))
    spec_out = pl.BlockSpec((1, 1, n_chunk, LANE), lambda b, gg: (b, gg, 0, 0))
    full = lambda a: pl.BlockSpec(a.shape, lambda b, gg: (0,) * a.ndim)
    out = jax.ShapeDtypeStruct((bsz, g, n_chunk, LANE), BF16)
    return pl.pallas_call(
        _compress_kernel,
        grid=(bsz, g),
        in_specs=[spec_in, spec_in, full(pk), full(pv), full(w1k), full(w2k), full(w1v), full(w2v)],
        out_specs=[spec_out, spec_out],
        out_shape=[out, out],
        compiler_params=_cparams("parallel", "parallel"),
        name="compress",
    )(kch, vch, pk, pv, w1k, w2k, w1v, w2v)


def _attn_kernel(q_ref, kc_ref, vc_ref, ks_ref, vs_ref, kw_ref, vw_ref, gate_ref,
                 bias_c_ref, bias_t_ref, bias_w_ref, overlap_ref, expand_ref,
                 o_ref, mask_ref):
    qi = pl.program_id(2)
    rows = HEADS_PER_GROUP * TQ
    seq = mask_ref.shape[1]
    q4 = q_ref[0].reshape(rows, LANE)

    def per_head(x):
        return jnp.broadcast_to(x[None], (HEADS_PER_GROUP,) + x.shape).reshape(rows, x.shape[-1])

    lc = lax.dot_general(q4, kc_ref[0, 0], NT_DIMS, preferred_element_type=F32) * ATTN_SCALE
    lc = lc + bias_c_ref[...].reshape(rows, LANE)
    valid = lc > 0.5 * NEG_INF
    mc = jnp.max(lc, axis=-1, keepdims=True)
    ec = jnp.where(valid, jnp.exp(lc - mc), 0.0)
    sc = jnp.sum(ec, axis=-1, keepdims=True)
    pc = ec / jnp.where(sc > 0.0, sc, 1.0)
    o_cmp = jnp.dot(pc.astype(BF16), vc_ref[0, 0], preferred_element_type=F32)

    psum = jnp.sum(pc.reshape(HEADS_PER_GROUP, TQ, LANE), axis=0)
    imp = jnp.dot(psum, overlap_ref[...], preferred_element_type=F32,
                  precision=lax.Precision.HIGHEST)
    t_pos = qi * TQ + lax.broadcasted_iota(jnp.int32, (TQ, LANE), 0)
    j_blk = lax.broadcasted_iota(jnp.int32, (TQ, LANE), 1)
    cur = t_pos // SLC_BLOCK
    forced = (j_blk == 0) | (j_blk == cur) | (j_blk == cur - 1)
    score = jnp.where(j_blk <= cur, imp + jnp.where(forced, FORCE_BONUS, 0.0), NEG_INF)
    n_slc = seq // SLC_BLOCK
    sc_t = score.T[:n_slc]
    j_row = lax.broadcasted_iota(jnp.int32, (n_slc, TQ), 0)
    rank = jnp.zeros((n_slc, TQ), F32)
    for jp in range(n_slc):
        row = sc_t[jp:jp + 1, :]
        ge = jnp.where(row >= sc_t, 1.0, 0.0)
        gt = jnp.where(row > sc_t, 1.0, 0.0)
        rank = rank + jnp.where(j_row > jp, ge, gt)
    sel_t = jnp.where(rank < float(min(SLC_TOPN, n_slc)), 1.0, 0.0)
    sel_t = jnp.concatenate([sel_t, jnp.zeros((LANE - n_slc, TQ), F32)], axis=0)
    sel = sel_t.T.astype(BF16)
    sel_keys = jnp.dot(sel, expand_ref[...], preferred_element_type=F32)
    k_pos = lax.broadcasted_iota(jnp.int32, (TQ, seq), 1)
    q_pos = qi * TQ + lax.broadcasted_iota(jnp.int32, (TQ, seq), 0)
    mask_ref[...] = jnp.where((sel_keys > 0.5) & (k_pos <= q_pos), 0.0, NEG_INF)

    tiles_per_chunk = SLC_CHUNK // TQ

    def slc_step(c, carry):
        m, l, acc = carry
        k0 = pl.multiple_of(c * SLC_CHUNK, SLC_CHUNK)
        kt = ks_ref[0, 0, pl.ds(k0, SLC_CHUNK), :]
        vt = vs_ref[0, 0, pl.ds(k0, SLC_CHUNK), :]
        s = lax.dot_general(q4, kt, NT_DIMS, preferred_element_type=F32) * ATTN_SCALE
        bias = jnp.concatenate(
            [bias_t_ref[jnp.clip(qi - (c * tiles_per_chunk + k), 0, 2)]
             for k in range(tiles_per_chunk)], axis=-1)
        s = s + bias.reshape(rows, SLC_CHUNK) + per_head(mask_ref[:, pl.ds(k0, SLC_CHUNK)])
        m_new = jnp.maximum(m, jnp.max(s, axis=-1, keepdims=True))
        alpha = jnp.exp(m - m_new)
        p = jnp.exp(s - m_new)
        l = alpha * l + jnp.sum(p, axis=-1, keepdims=True)
        acc = alpha * acc + jnp.dot(p.astype(BF16), vt, preferred_element_type=F32)
        return m_new, l, acc

    n_chunks = (qi + tiles_per_chunk) // tiles_per_chunk
    init = (jnp.full((rows, 1), -jnp.inf, F32), jnp.zeros((rows, 1), F32),
            jnp.zeros((rows, LANE), F32))
    _, l_s, acc_s = lax.fori_loop(0, n_chunks, slc_step, init)
    o_slc = acc_s / l_s

    w0 = pl.multiple_of(qi * TQ, TQ)
    kwin = kw_ref[0, 0, pl.ds(w0, WIN_SPAN), :]
    vwin = vw_ref[0, 0, pl.ds(w0, WIN_SPAN), :]
    sw = lax.dot_general(q4, kwin, NT_DIMS, preferred_element_type=F32) * ATTN_SCALE
    sw = sw + bias_w_ref[...].reshape(rows, WIN_SPAN)
    kj = lax.broadcasted_iota(jnp.int32, (rows, WIN_SPAN), 1)
    sw = jnp.where(kj >= WINDOW - qi * TQ, sw, NEG_INF)
    mw = jnp.max(sw, axis=-1, keepdims=True)
    pw = jnp.exp(sw - mw)
    o_win = jnp.dot(pw.astype(BF16), vwin, preferred_element_type=F32)
    o_win = o_win / jnp.sum(pw, axis=-1, keepdims=True)

    gates = _sigmoid(gate_ref[...].astype(F32))
    outs = []
    for r in range(HEADS_PER_GROUP):
        sl = slice(r * TQ, (r + 1) * TQ)
        g_c = gates[:, r:r + 1]
        g_s = gates[:, HEADS_PER_GROUP + r:HEADS_PER_GROUP + r + 1]
        g_w = gates[:, 2 * HEADS_PER_GROUP + r:2 * HEADS_PER_GROUP + r + 1]
        o = g_c * o_cmp[sl] + g_s * o_slc[sl] + g_w * o_win[sl]
        outs.append(o[:, :D_V])
    o_ref[0] = jnp.concatenate(outs, axis=-1).astype(o_ref.dtype)


def _attention(qkv, kwp, vwp, kc, vc, rest, bias_c, bias_t, bias_w, overlap, expand):
    bsz, _, seq, _ = qkv.shape
    nq = seq // TQ
    g_cols0 = REST_GATE0 // LANE
    head = lambda base: pl.BlockSpec((1, 1, seq, LANE), lambda b, g, i: (b, base + g, 0, 0))
    in_specs = [
        pl.BlockSpec((1, HEADS_PER_GROUP, TQ, LANE), lambda b, g, i: (b, g, i, 0)),
        pl.BlockSpec((1, 1, LANE, LANE), lambda b, g, i: (b, g, 0, 0)),
        pl.BlockSpec((1, 1, LANE, LANE), lambda b, g, i: (b, g, 0, 0)),
        head(QKV_KS0), head(QKV_VS0),
        pl.BlockSpec((1, 1, seq + WINDOW, LANE), lambda b, g, i: (b, g, 0, 0)),
        pl.BlockSpec((1, 1, seq + WINDOW, LANE), lambda b, g, i: (b, g, 0, 0)),
        pl.BlockSpec((TQ, LANE), lambda b, g, i: (b * nq + i, g_cols0 + g)),
        pl.BlockSpec((HEADS_PER_GROUP, TQ, LANE), lambda b, g, i: (g, i, 0)),
        pl.BlockSpec((3, HEADS_PER_GROUP, TQ, TQ), lambda b, g, i: (0, g, 0, 0)),
        pl.BlockSpec((HEADS_PER_GROUP, TQ, WIN_SPAN), lambda b, g, i: (g, 0, 0)),
        pl.BlockSpec((LANE, LANE), lambda b, g, i: (0, 0)),
        pl.BlockSpec((LANE, seq), lambda b, g, i: (0, 0)),
    ]
    return pl.pallas_call(
        _attn_kernel,
        grid=(bsz, N_GROUPS, nq),
        in_specs=in_specs,
        out_specs=pl.BlockSpec((1, TQ, HEADS_PER_GROUP * D_V), lambda b, g, i: (b, i, g)),
        out_shape=jax.ShapeDtypeStruct((bsz, seq, N_HEADS * D_V), BF16),
        scratch_shapes=[pltpu.VMEM((TQ, seq), F32)],
        compiler_params=_cparams("parallel", "parallel", "arbitrary"),
        name="attention",
    )(qkv, kc, vc, qkv, qkv, kwp, vwp, rest, bias_c, bias_t, bias_w, overlap, expand)


def _tail_kernel(attn_ref, ga_ref, gg_ref, ha_ref, hg_ref, ma_ref, mb_ref, x_ref, mod_ref,
                 wa_ref, wc_ref, wo_ref, cw_ref, cvec_ref, gvec_ref,
                 x1_ref, h2_ref, u_ref, *, tiles_per_seq):
    first = (pl.program_id(0) % tiles_per_seq) == 0
    y_a = jnp.dot(attn_ref[...], wa_ref[...], preferred_element_type=F32)

    halo = ha_ref[...].astype(F32) * _sigmoid(hg_ref[...].astype(F32))
    u_ref[0:HALO, :] = jnp.where(first, 0.0, halo)
    u_ref[HALO:, :] = ga_ref[...].astype(F32) * _sigmoid(gg_ref[...].astype(F32))

    conv_b, ln_g, ln_b = cvec_ref[0:1, :], cvec_ref[1:2, :], cvec_ref[2:3, :]
    blk = 32
    pieces = []
    for rb in range(TAIL_TM // blk):
        acc = jnp.zeros((blk, D_MODEL), F32)
        for j in range(CONV_WIDTH):
            start = HALO - (CONV_WIDTH - 1) + rb * blk + j
            acc = acc + cw_ref[j:j + 1, :] * u_ref[start:start + blk, :]
        pieces.append(acc)
    y = jnp.concatenate(pieces, axis=0) + conv_b
    yc = y - jnp.mean(y, axis=-1, keepdims=True)
    yn = yc * lax.rsqrt(jnp.mean(yc * yc, axis=-1, keepdims=True) + NORM_EPS) * ln_g + ln_b
    act = yn * _sigmoid(yn)
    y_b = jnp.dot(act.astype(BF16), wc_ref[...], preferred_element_type=F32)

    merged = (_sigmoid(ma_ref[...].astype(F32)) * y_a + _sigmoid(mb_ref[...].astype(F32)) * y_b)
    out = jnp.dot(merged.astype(BF16), wo_ref[...], preferred_element_type=F32)

    gt1, sh2, sc2 = mod_ref[0, 0:1, :], mod_ref[0, 1:2, :], mod_ref[0, 2:3, :]
    x1 = x_ref[...] + gt1 * _rms(out, gvec_ref[0:1, :])
    x1_ref[...] = x1
    h2_ref[...] = (_rms(x1, gvec_ref[1:2, :]) * (1.0 + sc2) + sh2).astype(BF16)


def _mixer_tail(attn2d, rest, x2d, mod3, wa, wc, wo, cw, cvec, gvec, seq):
    t, d = x2d.shape
    tiles_per_seq = seq // TAIL_TM
    hpt = TAIL_TM // HALO
    row = lambda cb: pl.BlockSpec((TAIL_TM, d), lambda i: (i, cb))
    halo = lambda cb: pl.BlockSpec((HALO, d), lambda i: (jnp.maximum(i * hpt - 1, 0), cb))
    full = lambda a: pl.BlockSpec(a.shape, lambda i: (0,) * a.ndim)
    return pl.pallas_call(
        functools.partial(_tail_kernel, tiles_per_seq=tiles_per_seq),
        grid=(t // TAIL_TM,),
        in_specs=[row(0), row(REST_GLU_A), row(REST_GLU_G), halo(REST_GLU_A), halo(REST_GLU_G),
                  row(REST_MRG_A), row(REST_MRG_B), row(0),
                  pl.BlockSpec((1, 3, d), lambda i: (i // tiles_per_seq, 0, 0)),
                  full(wa), full(wc), full(wo), full(cw), full(cvec), full(gvec)],
        out_specs=[row(0), row(0)],
        out_shape=[jax.ShapeDtypeStruct((t, d), F32), jax.ShapeDtypeStruct((t, d), BF16)],
        scratch_shapes=[pltpu.VMEM((HALO + TAIL_TM, d), F32)],
        compiler_params=_cparams("parallel"),
        name="mixer_tail",
    )(attn2d, rest, rest, rest, rest, rest, rest, x2d, mod3, wa, wc, wo, cw, cvec, gvec)


def _cand_pairs():
    return [(i, j) for i in range(PEER_TOPK) for j in range(PEER_TOPK)
            if (i + 1) * (j + 1) <= PEER_TOPK]


def _top16_rows(s):
    n = s.shape[0]
    kio = lax.broadcasted_iota(jnp.int32, s.shape, 0).astype(F32)
    vals, idxs = [], []
    for _ in range(PEER_TOPK):
        m = jnp.max(s, axis=0, keepdims=True)
        idx = jnp.min(jnp.where(s == m, kio, float(n)), axis=0, keepdims=True)
        vals.append(m)
        idxs.append(idx)
        s = jnp.where(kio == idx, -jnp.inf, s)
    return vals, idxs


def _route_kernel(h2_ref, wqt_ref, sk_ref, f_ref, cnt_ref, g_ref, rb_ref):
    q2t = lax.dot_general(wqt_ref[...], h2_ref[...], NT_DIMS, preferred_element_type=F32)
    tl = q2t.shape[1]
    kio = lax.broadcasted_iota(jnp.int32, (PEER_NKEYS, tl), 0).astype(F32)
    pairs = _cand_pairs()
    n_pad = -len(pairs) % 8
    flat_ids = np.array([i * PEER_TOPK + j for i, j in pairs] + [PEER_TOPK ** 2] * n_pad, np.float32)
    group_start = [min(k for k, (i, _) in enumerate(pairs) if i == ii) for ii in range(PEER_TOPK)]
    group_len = [sum(1 for (i, _) in pairs if i == ii) for ii in range(PEER_TOPK)]
    n_rows = len(pairs) + n_pad
    flat_col = lax.broadcasted_iota(jnp.int32, (n_rows, tl), 0)
    flat = jnp.zeros((n_rows, tl), F32)
    for k in range(n_rows):
        flat = jnp.where(flat_col == k, float(flat_ids[k]), flat)

    for h in range(PEER_HEADS):
        scores, tops = [], []
        for c in range(2):
            hc = 2 * h + c
            qt = q2t[hc * LANE:(hc + 1) * LANE, :].astype(BF16)
            s = jnp.dot(sk_ref[hc], qt, preferred_element_type=F32)
            scores.append(s)
            tops.append(_top16_rows(s))
        (v1, i1), (v2, i2) = tops
        cand0 = jnp.concatenate([v1[i] + v2[j] for i, j in pairs]
                                + [jnp.full((n_pad, tl), -jnp.inf, F32)], axis=0)
        cand = cand0
        for _ in range(PEER_TOPK):
            m = jnp.max(cand, axis=0, keepdims=True)
            fid = jnp.min(jnp.where(cand == m, flat, float(PEER_TOPK ** 2 + 1)), axis=0, keepdims=True)
            cand = jnp.where(flat == fid, -jnp.inf, cand)
        picked = jnp.where((cand == -jnp.inf) & (flat < float(PEER_TOPK ** 2)), 1.0, 0.0)
        top = v1[0] + v2[0]
        z = jnp.sum(picked * jnp.exp(cand0 - top), axis=0, keepdims=True)
        counts = [jnp.sum(picked[group_start[i]:group_start[i] + group_len[i]], axis=0, keepdims=True)
                  for i in range(PEER_TOPK)]

        cnt = jnp.zeros((PEER_NKEYS, tl), F32)
        rank_b = jnp.full((PEER_NKEYS, tl), float(PEER_NKEYS), F32)
        for i in range(PEER_TOPK):
            cnt = jnp.where(kio == i1[i], counts[i], cnt)
            rank_b = jnp.where(kio == i2[i], float(i), rank_b)
        f_ref[h] = jnp.exp(scores[0] - v1[0]) / z
        g_ref[h] = jnp.exp(scores[1] - v2[0])
        cnt_ref[h] = cnt
        rb_ref[h] = rank_b


def _route(h2, wqt, sk):
    t, d = h2.shape
    spec = pl.BlockSpec((PEER_HEADS, PEER_NKEYS, ROUTE_TL), lambda i: (0, 0, i))
    out = jax.ShapeDtypeStruct((PEER_HEADS, PEER_NKEYS, t), F32)
    return pl.pallas_call(
        _route_kernel,
        grid=(t // ROUTE_TL,),
        in_specs=[pl.BlockSpec((ROUTE_TL, d), lambda i: (i, 0)),
                  pl.BlockSpec(wqt.shape, lambda i: (0, 0)),
                  pl.BlockSpec(sk.shape, lambda i: (0, 0, 0))],
        out_specs=[spec, spec, spec, spec],
        out_shape=[out, out, out, out],
        compiler_params=_cparams("parallel"),
        name="peer_route",
    )(h2, wqt, sk)


def _expert_kernel(h2_ref, u_ref, vt_ref, f_ref, cnt_ref, g_ref, rb_ref, x1_ref, mod_ref, gp_ref,
                   o_ref, acc_ref):
    e = pl.program_id(1)

    @pl.when(e == 0)
    def _():
        acc_ref[...] = jnp.zeros_like(acc_ref)

    act = lax.dot_general(u_ref[...], h2_ref[...], NT_DIMS, preferred_element_type=F32)
    act = _gelu(act)
    a_per_step = EXP_TE // PEER_NKEYS
    blocks = []
    for al in range(a_per_step):
        a = e * a_per_step + al
        coef = jnp.zeros((PEER_NKEYS, act.shape[1]), F32)
        for h in range(PEER_HEADS):
            f_row = f_ref[h, pl.ds(a, 1), :]
            c_row = cnt_ref[h, pl.ds(a, 1), :]
            coef = coef + f_row * jnp.where(rb_ref[h] < c_row, g_ref[h], 0.0)
        blocks.append(coef)
    coef = jnp.concatenate(blocks, axis=0)
    acc_ref[...] += jnp.dot(vt_ref[...], (act * coef).astype(BF16), preferred_element_type=F32)

    @pl.when(e == pl.num_programs(1) - 1)
    def _():
        y = acc_ref[...].T
        o_ref[...] = x1_ref[...] + mod_ref[0] * _rms(y, gp_ref[...])


def _experts(h2, u, vt, f, cnt, g, rb, x1, gt2, gpost, seq):
    t, d = h2.shape
    n_exp = u.shape[0]
    tiles_per_seq = seq // EXP_TL
    route = pl.BlockSpec((PEER_HEADS, PEER_NKEYS, EXP_TL), lambda i, e: (0, 0, i))
    return pl.pallas_call(
        _expert_kernel,
        grid=(t // EXP_TL, n_exp // EXP_TE),
        in_specs=[pl.BlockSpec((EXP_TL, d), lambda i, e: (i, 0)),
                  pl.BlockSpec((EXP_TE, d), lambda i, e: (e, 0)),
                  pl.BlockSpec((d, EXP_TE), lambda i, e: (0, e)),
                  route, route, route, route,
                  pl.BlockSpec((EXP_TL, d), lambda i, e: (i, 0)),
                  pl.BlockSpec((1, 1, d), lambda i, e: (i // tiles_per_seq, 0, 0)),
                  pl.BlockSpec((1, d), lambda i, e: (0, 0))],
        out_specs=pl.BlockSpec((EXP_TL, d), lambda i, e: (i, 0)),
        out_shape=jax.ShapeDtypeStruct((t, d), F32),
        scratch_shapes=[pltpu.VMEM((d, EXP_TL), F32)],
        compiler_params=_cparams("parallel", "arbitrary"),
        name="peer_experts",
    )(h2, u, vt, f, cnt, g, rb, x1, gt2, gpost)


def _pad_heads(w, n, width):
    d = w.shape[0]
    return jnp.pad(w.reshape(d, n, width), ((0, 0), (0, 0), (0, LANE - width))).reshape(d, n * LANE)


def _split_w_in(w_in):
    q_cols = N_HEADS * D_QK
    k_cols = N_GROUPS * D_QK
    v_cols = N_GROUPS * D_V
    sizes = (q_cols, k_cols, k_cols, k_cols, v_cols, v_cols, v_cols, 3 * N_HEADS, 2 * D_MODEL, 2 * D_MODEL)
    offs = np.cumsum((0,) + sizes)
    parts = [w_in[:, offs[k]:offs[k + 1]] for k in range(len(sizes))]
    wq, wkc, wks, wkw, wvc, wvs, wvw, wgate, wglu, wmerge = parts
    w_qkv = jnp.concatenate(
        [_pad_heads(wq, N_HEADS, D_QK)]
        + [_pad_heads(w, N_GROUPS, D_QK) for w in (wkc, wks, wkw)]
        + [_pad_heads(w, N_GROUPS, D_V) for w in (wvc, wvs, wvw)], axis=1)
    d = w_in.shape[0]
    wg = wgate.reshape(d, 3, N_GROUPS, HEADS_PER_GROUP).transpose(0, 2, 1, 3)
    wg = wg.reshape(d, N_GROUPS, 3 * HEADS_PER_GROUP)
    wg = jnp.pad(wg, ((0, 0), (0, 0), (0, LANE - 3 * HEADS_PER_GROUP))).reshape(d, N_GROUPS * LANE)
    w_rest = jnp.concatenate([wglu, wmerge, wg], axis=1)
    return w_qkv.astype(BF16), w_rest.astype(BF16)


def _cmp_weights(w1, w2, pos, dh):
    hidden = w1.shape[1]
    w1p = jnp.pad(w1.reshape(CMP_BLOCK, dh, hidden), ((0, 0), (0, LANE - dh), (0, 0)))
    w1p = w1p.reshape(2, CMP_STRIDE * LANE, hidden).astype(BF16)
    w2p = jnp.pad(w2, ((0, 0), (0, LANE - dh))).astype(BF16)
    posp = jnp.pad(pos, ((0, 0), (0, LANE - dh))).reshape(2, CMP_STRIDE * LANE)
    return w1p, w2p, posp


def kernel(x, c, w_ada, b_ada, g_pre_mix, g_post_mix, g_pre_ffn, g_post_ffn, rel_table, w_in,
           cmp_w1k, cmp_w2k, cmp_pos_k, cmp_w1v, cmp_w2v, cmp_pos_v, w_attn_out,
           conv_w, conv_b, conv_ln_g, conv_ln_b, w_conv_out, w_out,
           peer_wq, peer_subkeys, peer_u, peer_v):
    bsz, seq, d = x.shape
    t = bsz * seq
    assert d == D_MODEL and w_ada.shape[0] == 1, "single-layer block with D_MODEL channels"
    assert seq % PROJ_TM == 0 and seq // CMP_STRIDE == LANE and seq % EXP_TL == 0
    x2d = x.reshape(t, d)

    mod = _ada(c, w_ada[0], b_ada[0]).reshape(bsz, 6, d)
    mod_in = mod[:, 0:2]
    mod_tail = mod[:, 2:5]
    mod_out = mod[:, 5:6]

    bkt_c, bkt_t, bkt_w = _static_buckets(seq)
    bias_c = _bias_table(rel_table, bkt_c)
    bias_t = _bias_table(rel_table, bkt_t).reshape(N_HEADS, 3, TQ, TQ).transpose(1, 0, 2, 3)
    bias_w = _bias_table(rel_table, bkt_w)

    w_qkv, w_rest = _split_w_in(w_in[0])
    qkv = _project(x2d, mod_in, g_pre_mix, w_qkv, seq, heads_out=True)
    rest = _project(x2d, mod_in, g_pre_mix, w_rest, seq, heads_out=False)

    n_chunk = seq // CMP_STRIDE
    kch = qkv[:, QKV_KC0:QKV_KC0 + N_GROUPS].reshape(bsz, N_GROUPS, n_chunk, CMP_STRIDE * LANE)
    vch = qkv[:, QKV_VC0:QKV_VC0 + N_GROUPS].reshape(bsz, N_GROUPS, n_chunk, CMP_STRIDE * LANE)
    w1k, w2k, pk = _cmp_weights(cmp_w1k[0], cmp_w2k[0], cmp_pos_k[0], D_QK)
    w1v, w2v, pv = _cmp_weights(cmp_w1v[0], cmp_w2v[0], cmp_pos_v[0], D_V)
    kc, vc = _compress(kch, vch, pk, pv, w1k, w2k, w1v, w2v)

    pad = ((0, 0), (0, 0), (WINDOW, 0), (0, 0))
    kwp = jnp.pad(qkv[:, QKV_KW0:QKV_KW0 + N_GROUPS], pad)
    vwp = jnp.pad(qkv[:, QKV_VW0:QKV_VW0 + N_GROUPS], pad)

    n_idx = np.arange(LANE)[:, None] * CMP_STRIDE
    j_idx = np.arange(LANE)[None, :] * SLC_BLOCK
    overlap = ((n_idx <= j_idx + SLC_BLOCK - 1) & (n_idx + CMP_BLOCK - 1 >= j_idx)
               & (np.arange(LANE)[None, :] < seq // SLC_BLOCK)
               & (np.arange(LANE)[:, None] < n_chunk - 1)).astype(np.float32)
    expand = (np.arange(LANE)[:, None] == np.arange(seq)[None, :] // SLC_BLOCK)
    attn = _attention(qkv, kwp, vwp, kc, vc, rest, bias_c, bias_t, bias_w,
                      jnp.asarray(overlap), jnp.asarray(expand, dtype=BF16))

    cvec = jnp.stack([conv_b[0], conv_ln_g[0], conv_ln_b[0]])
    gvec = jnp.stack([g_post_mix[0], g_pre_ffn[0]])
    x1, h2 = _mixer_tail(attn.reshape(t, N_HEADS * D_V), rest, x2d, mod_tail,
                         w_attn_out[0].astype(BF16), w_conv_out[0].astype(BF16),
                         w_out[0].astype(BF16), conv_w[0, :, 0, :], cvec, gvec, seq)

    wqt = peer_wq[0].T.astype(BF16)
    sk = peer_subkeys[0].reshape(2 * PEER_HEADS, PEER_NKEYS, PEER_DQ // 2).astype(BF16)
    f, cnt, g, rb = _route(h2, wqt, sk)
    out = _experts(h2, peer_u[0].astype(BF16), peer_v[0].T.astype(BF16), f, cnt, g, rb,
                   x1, mod_out, g_post_ffn, seq)
    return out.reshape(bsz, seq, d)
```

```python
import functools
import math

import numpy as np
import jax
import jax.numpy as jnp
from jax import lax
from jax.experimental import pallas as pl
from jax.experimental.pallas import tpu as pltpu

F32 = jnp.float32
BF16 = jnp.bfloat16

D_MODEL = 1024
N_HEADS = 16
N_GROUPS = 4
HEADS_PER_GROUP = N_HEADS // N_GROUPS
D_QK = 96
D_V = 64
CMP_BLOCK = 32
CMP_STRIDE = 16
CMP_HIDDEN = 256
SLC_BLOCK = 64
SLC_TOPN = 16
WINDOW = 512
ATTN_SCALE = D_QK ** -0.5
FORCE_BONUS = 1e6
N_BUCKETS = 32
MAX_DISTANCE = 128
CONV_WIDTH = 31
PEER_HEADS = 8
PEER_NKEYS = 128
PEER_DQ = 256
PEER_TOPK = 16
NORM_EPS = 1e-6
NEG_INF = -1e30

LANE = 128
BF16_ROWS = 16
VMEM_LIMIT = 56 * 1024 * 1024

TQ = 128
SLC_CHUNK = 256
WIN_SPAN = TQ + WINDOW
PROJ_TM = 2048
PROJ_TN = 512
TAIL_TM = 256
HALO = 32
ROUTE_TL = 256
EXP_TL = 512
EXP_TE = 512

QKV_Q0, QKV_KC0, QKV_KS0, QKV_KW0, QKV_VC0, QKV_VS0, QKV_VW0 = 0, 16, 20, 24, 28, 32, 36
QKV_HEADS = 40
REST_GLU_A, REST_GLU_G, REST_MRG_A, REST_MRG_B = 0, 1, 2, 3
REST_GATE0 = 4 * D_MODEL
REST_COLS = 4 * D_MODEL + N_GROUPS * LANE

NT_DIMS = (((1,), (1,)), ((), ()))


def _cparams(*sem):
    return pltpu.CompilerParams(dimension_semantics=sem, vmem_limit_bytes=VMEM_LIMIT)


def _gelu(x):
    return 0.5 * x * (1.0 + jnp.tanh(math.sqrt(2.0 / math.pi) * (x + 0.044715 * (x * x * x))))


def _gelu_x2(x):
    k1 = math.sqrt(2.0 / math.pi)
    return x * (1.0 + jnp.tanh(x * (k1 + (k1 * 0.044715) * (x * x))))


def _sigmoid(x):
    return 1.0 / (1.0 + jnp.exp(-x))


def _rms(x, g):
    return x * lax.rsqrt(jnp.mean(x * x, axis=-1, keepdims=True) + NORM_EPS) * g


def _ada_kernel(c_ref, w_ref, b_ref, o_ref):
    c = c_ref[...]
    c_act = c * _sigmoid(c)
    o_ref[...] = jnp.dot(c_act, w_ref[...], preferred_element_type=F32,
                         precision=lax.Precision.HIGHEST) + b_ref[...]


def _ada(c, w, b):
    bsz, d = c.shape
    n = w.shape[1]
    tn = 1024
    return pl.pallas_call(
        _ada_kernel,
        grid=(n // tn,),
        in_specs=[pl.BlockSpec((bsz, d), lambda j: (0, 0)),
                  pl.BlockSpec((d, tn), lambda j: (0, j)),
                  pl.BlockSpec((1, tn), lambda j: (0, j))],
        out_specs=pl.BlockSpec((bsz, tn), lambda j: (0, j)),
        out_shape=jax.ShapeDtypeStruct((bsz, n), F32),
        compiler_params=_cparams("parallel"),
        name="ada",
    )(c, w, b.reshape(1, n))


def _t5_bucket_np(dist):
    n = np.maximum(dist, 0)
    max_exact = N_BUCKETS // 2
    nf = np.maximum(n, 1).astype(np.float64)
    large = max_exact + (np.log(nf / max_exact) / math.log(MAX_DISTANCE / max_exact)
                         * (N_BUCKETS - max_exact)).astype(np.int64)
    return np.where(n < max_exact, n, np.minimum(large, N_BUCKETS - 1)).astype(np.int32)


def _bias_kernel(tab_ref, bkt_ref, o_ref):
    h = pl.program_id(0)
    bkt = bkt_ref[...]
    acc = jnp.where(bkt < 0, NEG_INF, 0.0).astype(F32)
    for i in range(N_BUCKETS):
        acc = jnp.where(bkt == i, tab_ref[i, h], acc)
    o_ref[0] = acc


def _bias_table(rel_table, bucket):
    rows, cols = bucket.shape
    return pl.pallas_call(
        _bias_kernel,
        grid=(N_HEADS,),
        in_specs=[pl.BlockSpec(memory_space=pltpu.SMEM),
                  pl.BlockSpec((rows, cols), lambda h: (0, 0))],
        out_specs=pl.BlockSpec((1, rows, cols), lambda h: (h, 0, 0)),
        out_shape=jax.ShapeDtypeStruct((N_HEADS, rows, cols), F32),
        compiler_params=_cparams("parallel"),
        name="bias_table",
    )(rel_table, jnp.asarray(bucket))


def _static_buckets(seq):
    t = np.arange(seq)[:, None]
    n = np.arange(LANE)[None, :]
    dist_c = t - (n * CMP_STRIDE + CMP_BLOCK - 1)
    bkt_c = np.where(dist_c >= 0, _t5_bucket_np(dist_c), -1).astype(np.int32)
    qi = np.arange(TQ)[:, None]
    kj = np.arange(TQ)[None, :]
    tiles = [_t5_bucket_np(d * TQ + qi - kj) for d in range(3)]
    bkt_t = np.concatenate(tiles, axis=0).astype(np.int32)
    kw = np.arange(WIN_SPAN)[None, :]
    dist_w = WINDOW + qi - kw
    band = (dist_w >= 0) & (dist_w < WINDOW)
    bkt_w = np.where(band, _t5_bucket_np(dist_w), -1).astype(np.int32)
    return bkt_c, bkt_t, bkt_w


def _proj_kernel(x_ref, mod_ref, g_ref, w_ref, o_ref, h_ref, *, heads_out):
    @pl.when(pl.program_id(1) == 0)
    def _():
        x = x_ref[...]
        sh = mod_ref[0, 0:1, :]
        sc = mod_ref[0, 1:2, :]
        h_ref[...] = (_rms(x, g_ref[...]) * (1.0 + sc) + sh).astype(BF16)

    res = jnp.dot(h_ref[...], w_ref[...], preferred_element_type=F32).astype(o_ref.dtype)
    if heads_out:
        for k in range(PROJ_TN // LANE):
            o_ref[0, k] = res[:, k * LANE:(k + 1) * LANE]
    else:
        o_ref[...] = res


def _project(x2d, mod3, g, w, seq, heads_out):
    t, d = x2d.shape
    n = w.shape[1]
    bsz = t // seq
    tiles_per_seq = seq // PROJ_TM
    grid = (t // PROJ_TM, n // PROJ_TN)
    if heads_out:
        hpt = PROJ_TN // LANE
        out_shape = jax.ShapeDtypeStruct((bsz, n // LANE, seq, LANE), BF16)
        out_spec = pl.BlockSpec((1, hpt, PROJ_TM, LANE),
                                lambda i, j: (i // tiles_per_seq, j, i % tiles_per_seq, 0))
    else:
        out_shape = jax.ShapeDtypeStruct((t, n), BF16)
        out_spec = pl.BlockSpec((PROJ_TM, PROJ_TN), lambda i, j: (i, j))
    return pl.pallas_call(
        functools.partial(_proj_kernel, heads_out=heads_out),
        grid=grid,
        in_specs=[pl.BlockSpec((PROJ_TM, d), lambda i, j: (i, 0)),
                  pl.BlockSpec((1, 2, d), lambda i, j: (i // tiles_per_seq, 0, 0)),
                  pl.BlockSpec((1, d), lambda i, j: (0, 0)),
                  pl.BlockSpec((d, PROJ_TN), lambda i, j: (0, j))],
        out_specs=out_spec,
        out_shape=out_shape,
        scratch_shapes=[pltpu.VMEM((PROJ_TM, d), BF16)],
        compiler_params=_cparams("parallel", "arbitrary"),
        name="proj_heads" if heads_out else "proj_rest",
    )(x2d, mod3, g, w)


def _compress_kernel(kch_ref, vch_ref, pk_ref, pv_ref, w1k_ref, w2k_ref, w1v_ref, w2v_ref,
                     kc_ref, vc_ref):
    def one(ch_ref, pos_ref, w1_ref, w2_ref, o_ref):
        a = ch_ref[0, 0].astype(F32)
        a1 = (a + pos_ref[0:1, :]).astype(BF16)
        a2 = (a + pos_ref[1:2, :]).astype(BF16)
        p1 = jnp.dot(a1, w1_ref[0], preferred_element_type=F32)
        p2 = jnp.dot(a2, w1_ref[1], preferred_element_type=F32)
        n_chunk = p2.shape[0]
        hid = _gelu(p1 + pltpu.roll(p2, n_chunk - 1, 0))
        o_ref[0, 0] = jnp.dot(hid.astype(BF16), w2_ref[...],
                              preferred_element_type=F32).astype(BF16)

    one(kch_ref, pk_ref, w1k_ref, w2k_ref, kc_ref)
    one(vch_ref, pv_ref, w1v_ref, w2v_ref, vc_ref)


def _compress(kch, vch, pk, pv, w1k, w2k, w1v, w2v):
    bsz, g, n_chunk, width = kch.shape
    assert n_chunk == LANE, "compressed keys are laid out on one 128-row tile"
    spec_in = pl.BlockSpec((1, 1, n_chunk, width), lambda b, gg: (b, gg, 0, 0))
    spec_out = pl.BlockSpec((1, 1, n_chunk, LANE), lambda b, gg: (b, gg, 0, 0))
    full = lambda a: pl.BlockSpec(a.shape, lambda b, gg: (0,) * a.ndim)
    out = jax.ShapeDtypeStruct((bsz, g, n_chunk, LANE), BF16)
    return pl.pallas_call(
        _compress_kernel,
        grid=(bsz, g),
        in_specs=[spec_in, spec_in, full(pk), full(pv), full(w1k), full(w2k), full(w1v), full(w2v)],
        out_specs=[spec_out, spec_out],
        out_shape=[out, out],
        compiler_params=_cparams("parallel", "parallel"),
        name="compress",
    )(kch, vch, pk, pv, w1k, w2k, w1v, w2v)


def _attn_kernel(q_ref, kc_ref, vc_ref, ks_ref, vs_ref, kw_ref, vw_ref, gate_ref,
                 bias_c_ref, bias_t_ref, bias_w_ref, overlap_ref, expand_ref,
                 o_ref, mask_ref):
    qi = pl.program_id(2)
    rows = HEADS_PER_GROUP * TQ
    seq = mask_ref.shape[1]
    q4 = q_ref[0].reshape(rows, LANE)

    def per_head(x):
        return jnp.broadcast_to(x[None], (HEADS_PER_GROUP,) + x.shape).reshape(rows, x.shape[-1])

    lc = lax.dot_general(q4, kc_ref[0, 0], NT_DIMS, preferred_element_type=F32) * ATTN_SCALE
    lc = lc + bias_c_ref[...].reshape(rows, LANE)
    valid = lc > 0.5 * NEG_INF
    mc = jnp.max(lc, axis=-1, keepdims=True)
    ec = jnp.where(valid, jnp.exp(lc - mc), 0.0)
    sc = jnp.sum(ec, axis=-1, keepdims=True)
    pc = ec / jnp.where(sc > 0.0, sc, 1.0)
    o_cmp = jnp.dot(pc.astype(BF16), vc_ref[0, 0], preferred_element_type=F32)

    psum = jnp.sum(pc.reshape(HEADS_PER_GROUP, TQ, LANE), axis=0)
    imp = jnp.dot(psum, overlap_ref[...], preferred_element_type=F32,
                  precision=lax.Precision.HIGHEST)
    t_pos = qi * TQ + lax.broadcasted_iota(jnp.int32, (TQ, LANE), 0)
    j_blk = lax.broadcasted_iota(jnp.int32, (TQ, LANE), 1)
    cur = t_pos // SLC_BLOCK
    forced = (j_blk == 0) | (j_blk == cur) | (j_blk == cur - 1)
    score = jnp.where(j_blk <= cur, imp + jnp.where(forced, FORCE_BONUS, 0.0), NEG_INF)
    n_slc = seq // SLC_BLOCK
    sc_t = score.T[:n_slc]
    j_row = lax.broadcasted_iota(jnp.int32, (n_slc, TQ), 0)
    rank = jnp.zeros((n_slc, TQ), F32)
    for jp in range(n_slc):
        row = sc_t[jp:jp + 1, :]
        ge = jnp.where(row >= sc_t, 1.0, 0.0)
        gt = jnp.where(row > sc_t, 1.0, 0.0)
        rank = rank + jnp.where(j_row > jp, ge, gt)
    sel_t = jnp.where(rank < float(min(SLC_TOPN, n_slc)), 1.0, 0.0)
    sel_t = jnp.concatenate([sel_t, jnp.zeros((LANE - n_slc, TQ), F32)], axis=0)
    sel = sel_t.T.astype(BF16)
    sel_keys = jnp.dot(sel, expand_ref[...], preferred_element_type=F32)
    k_pos = lax.broadcasted_iota(jnp.int32, (TQ, seq), 1)
    q_pos = qi * TQ + lax.broadcasted_iota(jnp.int32, (TQ, seq), 0)
    mask_ref[...] = jnp.where((sel_keys > 0.5) & (k_pos <= q_pos), 0.0, NEG_INF)

    tiles_per_chunk = SLC_CHUNK // TQ

    def slc_step(c, carry):
        m, l, acc = carry
        k0 = pl.multiple_of(c * SLC_CHUNK, SLC_CHUNK)
        kt = ks_ref[0, 0, pl.ds(k0, SLC_CHUNK), :]
        vt = vs_ref[0, 0, pl.ds(k0, SLC_CHUNK), :]
        s = lax.dot_general(q4, kt, NT_DIMS, preferred_element_type=F32) * ATTN_SCALE
        bias = jnp.concatenate(
            [bias_t_ref[jnp.clip(qi - (c * tiles_per_chunk + k), 0, 2)]
             for k in range(tiles_per_chunk)], axis=-1)
        s = s + bias.reshape(rows, SLC_CHUNK) + per_head(mask_ref[:, pl.ds(k0, SLC_CHUNK)])
        m_new = jnp.maximum(m, jnp.max(s, axis=-1, keepdims=True))
        alpha = jnp.exp(m - m_new)
        p = jnp.exp(s - m_new)
        l = alpha * l + jnp.sum(p, axis=-1, keepdims=True)
        acc = alpha * acc + jnp.dot(p.astype(BF16), vt, preferred_element_type=F32)
        return m_new, l, acc

    n_chunks = (qi + tiles_per_chunk) // tiles_per_chunk
    init = (jnp.full((rows, 1), -jnp.inf, F32), jnp.zeros((rows, 1), F32),
            jnp.zeros((rows, LANE), F32))
    _, l_s, acc_s = lax.fori_loop(0, n_chunks, slc_step, init)
    o_slc = acc_s / l_s

    w0 = pl.multiple_of(qi * TQ, TQ)
    kwin = kw_ref[0, 0, pl.ds(w0, WIN_SPAN), :]
    vwin = vw_ref[0, 0, pl.ds(w0, WIN_SPAN), :]
    sw = lax.dot_general(q4, kwin, NT_DIMS, preferred_element_type=F32) * ATTN_SCALE
    sw = sw + bias_w_ref[...].reshape(rows, WIN_SPAN)
    kj = lax.broadcasted_iota(jnp.int32, (rows, WIN_SPAN), 1)
    sw = jnp.where(kj >= WINDOW - qi * TQ, sw, NEG_INF)
    mw = jnp.max(sw, axis=-1, keepdims=True)
    pw = jnp.exp(sw - mw)
    o_win = jnp.dot(pw.astype(BF16), vwin, preferred_element_type=F32)
    o_win = o_win / jnp.sum(pw, axis=-1, keepdims=True)

    gates = _sigmoid(gate_ref[...].astype(F32))
    outs = []
    for r in range(HEADS_PER_GROUP):
        sl = slice(r * TQ, (r + 1) * TQ)
        g_c = gates[:, r:r + 1]
        g_s = gates[:, HEADS_PER_GROUP + r:HEADS_PER_GROUP + r + 1]
        g_w = gates[:, 2 * HEADS_PER_GROUP + r:2 * HEADS_PER_GROUP + r + 1]
        o = g_c * o_cmp[sl] + g_s * o_slc[sl] + g_w * o_win[sl]
        outs.append(o[:, :D_V])
    o_ref[0] = jnp.concatenate(outs, axis=-1).astype(o_ref.dtype)


def _attention(qkv, kwp, vwp, kc, vc, rest, bias_c, bias_t, bias_w, overlap, expand):
    bsz, _, seq, _ = qkv.shape
    nq = seq // TQ
    g_cols0 = REST_GATE0 // LANE
    head = lambda base: pl.BlockSpec((1, 1, seq, LANE), lambda b, g, i: (b, base + g, 0, 0))
    in_specs = [
        pl.BlockSpec((1, HEADS_PER_GROUP, TQ, LANE), lambda b, g, i: (b, g, i, 0)),
        pl.BlockSpec((1, 1, LANE, LANE), lambda b, g, i: (b, g, 0, 0)),
        pl.BlockSpec((1, 1, LANE, LANE), lambda b, g, i: (b, g, 0, 0)),
        head(QKV_KS0), head(QKV_VS0),
        pl.BlockSpec((1, 1, seq + WINDOW, LANE), lambda b, g, i: (b, g, 0, 0)),
        pl.BlockSpec((1, 1, seq + WINDOW, LANE), lambda b, g, i: (b, g, 0, 0)),
        pl.BlockSpec((TQ, LANE), lambda b, g, i: (b * nq + i, g_cols0 + g)),
        pl.BlockSpec((HEADS_PER_GROUP, TQ, LANE), lambda b, g, i: (g, i, 0)),
        pl.BlockSpec((3, HEADS_PER_GROUP, TQ, TQ), lambda b, g, i: (0, g, 0, 0)),
        pl.BlockSpec((HEADS_PER_GROUP, TQ, WIN_SPAN), lambda b, g, i: (g, 0, 0)),
        pl.BlockSpec((LANE, LANE), lambda b, g, i: (0, 0)),
        pl.BlockSpec((LANE, seq), lambda b, g, i: (0, 0)),
    ]
    return pl.pallas_call(
        _attn_kernel,
        grid=(bsz, N_GROUPS, nq),
        in_specs=in_specs,
        out_specs=pl.BlockSpec((1, TQ, HEADS_PER_GROUP * D_V), lambda b, g, i: (b, i, g)),
        out_shape=jax.ShapeDtypeStruct((bsz, seq, N_HEADS * D_V), BF16),
        scratch_shapes=[pltpu.VMEM((TQ, seq), F32)],
        compiler_params=_cparams("parallel", "parallel", "arbitrary"),
        name="attention",
    )(qkv, kc, vc, qkv, qkv, kwp, vwp, rest, bias_c, bias_t, bias_w, overlap, expand)


def _tail_kernel(attn_ref, ga_ref, gg_ref, ha_ref, hg_ref, ma_ref, mb_ref, x_ref, mod_ref,
                 wa_ref, wc_ref, wo_ref, cw_ref, cvec_ref, gvec_ref,
                 x1_ref, h2_ref, u_ref, *, tiles_per_seq):
    first = (pl.program_id(0) % tiles_per_seq) == 0
    y_a = jnp.dot(attn_ref[...], wa_ref[...], preferred_element_type=F32)

    halo = ha_ref[...].astype(F32) * _sigmoid(hg_ref[...].astype(F32))
    u_ref[0:HALO, :] = jnp.where(first, 0.0, halo)
    u_ref[HALO:, :] = ga_ref[...].astype(F32) * _sigmoid(gg_ref[...].astype(F32))

    conv_b, ln_g, ln_b = cvec_ref[0:1, :], cvec_ref[1:2, :], cvec_ref[2:3, :]
    blk = 32
    pieces = []
    for rb in range(TAIL_TM // blk):
        acc = jnp.zeros((blk, D_MODEL), F32)
        for j in range(CONV_WIDTH):
            start = HALO - (CONV_WIDTH - 1) + rb * blk + j
            acc = acc + cw_ref[j:j + 1, :] * u_ref[start:start + blk, :]
        pieces.append(acc)
    y = jnp.concatenate(pieces, axis=0) + conv_b
    yc = y - jnp.mean(y, axis=-1, keepdims=True)
    yn = yc * lax.rsqrt(jnp.mean(yc * yc, axis=-1, keepdims=True) + NORM_EPS) * ln_g + ln_b
    act = yn * _sigmoid(yn)
    y_b = jnp.dot(act.astype(BF16), wc_ref[...], preferred_element_type=F32)

    merged = (_sigmoid(ma_ref[...].astype(F32)) * y_a + _sigmoid(mb_ref[...].astype(F32)) * y_b)
    out = jnp.dot(merged.astype(BF16), wo_ref[...], preferred_element_type=F32)

    gt1, sh2, sc2 = mod_ref[0, 0:1, :], mod_ref[0, 1:2, :], mod_ref[0, 2:3, :]
    x1 = x_ref[...] + gt1 * _rms(out, gvec_ref[0:1, :])
    x1_ref[...] = x1
    h2_ref[...] = (_rms(x1, gvec_ref[1:2, :]) * (1.0 + sc2) + sh2).astype(BF16)


def _mixer_tail(attn2d, rest, x2d, mod3, wa, wc, wo, cw, cvec, gvec, seq):
    t, d = x2d.shape
    tiles_per_seq = seq // TAIL_TM
    hpt = TAIL_TM // HALO
    row = lambda cb: pl.BlockSpec((TAIL_TM, d), lambda i: (i, cb))
    halo = lambda cb: pl.BlockSpec((HALO, d), lambda i: (jnp.maximum(i * hpt - 1, 0), cb))
    full = lambda a: pl.BlockSpec(a.shape, lambda i: (0,) * a.ndim)
    return pl.pallas_call(
        functools.partial(_tail_kernel, tiles_per_seq=tiles_per_seq),
        grid=(t // TAIL_TM,),
        in_specs=[row(0), row(REST_GLU_A), row(REST_GLU_G), halo(REST_GLU_A), halo(REST_GLU_G),
                  row(REST_MRG_A), row(REST_MRG_B), row(0),
                  pl.BlockSpec((1, 3, d), lambda i: (i // tiles_per_seq, 0, 0)),
                  full(wa), full(wc), full(wo), full(cw), full(cvec), full(gvec)],
        out_specs=[row(0), row(0)],
        out_shape=[jax.ShapeDtypeStruct((t, d), F32), jax.ShapeDtypeStruct((t, d), BF16)],
        scratch_shapes=[pltpu.VMEM((HALO + TAIL_TM, d), F32)],
        compiler_params=_cparams("parallel"),
        name="mixer_tail",
    )(attn2d, rest, rest, rest, rest, rest, rest, x2d, mod3, wa, wc, wo, cw, cvec, gvec)


def _cand_pairs():
    return [(i, j) for i in range(PEER_TOPK) for j in range(PEER_TOPK)
            if (i + 1) * (j + 1) <= PEER_TOPK]


def _top16_rows(s):
    n = s.shape[0]
    kio = lax.broadcasted_iota(jnp.int32, s.shape, 0).astype(F32)
    vals, idxs = [], []
    for _ in range(PEER_TOPK):
        m = jnp.max(s, axis=0, keepdims=True)
        idx = jnp.min(jnp.where(s == m, kio, float(n)), axis=0, keepdims=True)
        vals.append(m)
        idxs.append(idx)
        s = jnp.where(kio == idx, -jnp.inf, s)
    return vals, idxs


def _route_kernel(h2_ref, wqt_ref, sk_ref, f_ref, cnt_ref, g_ref, rb_ref):
    q2t = lax.dot_general(wqt_ref[...], h2_ref[...], NT_DIMS, preferred_element_type=F32)
    tl = q2t.shape[1]
    kio = lax.broadcasted_iota(jnp.int32, (PEER_NKEYS, tl), 0).astype(F32)
    pairs = _cand_pairs()
    n_pad = -len(pairs) % 8
    flat_ids = np.array([i * PEER_TOPK + j for i, j in pairs] + [PEER_TOPK ** 2] * n_pad, np.float32)
    group_start = [min(k for k, (i, _) in enumerate(pairs) if i == ii) for ii in range(PEER_TOPK)]
    group_len = [sum(1 for (i, _) in pairs if i == ii) for ii in range(PEER_TOPK)]
    n_rows = len(pairs) + n_pad
    flat_col = lax.broadcasted_iota(jnp.int32, (n_rows, tl), 0)
    flat = jnp.zeros((n_rows, tl), F32)
    for k in range(n_rows):
        flat = jnp.where(flat_col == k, float(flat_ids[k]), flat)

    for h in range(PEER_HEADS):
        scores, tops = [], []
        for c in range(2):
            hc = 2 * h + c
            qt = q2t[hc * LANE:(hc + 1) * LANE, :].astype(BF16)
            s = jnp.dot(sk_ref[hc], qt, preferred_element_type=F32)
            scores.append(s)
            tops.append(_top16_rows(s))
        (v1, i1), (v2, i2) = tops
        cand0 = jnp.concatenate([v1[i] + v2[j] for i, j in pairs]
                                + [jnp.full((n_pad, tl), -jnp.inf, F32)], axis=0)
        cand = cand0
        for _ in range(PEER_TOPK):
            m = jnp.max(cand, axis=0, keepdims=True)
            fid = jnp.min(jnp.where(cand == m, flat, float(PEER_TOPK ** 2 + 1)), axis=0, keepdims=True)
            cand = jnp.where(flat == fid, -jnp.inf, cand)
        picked = jnp.where((cand == -jnp.inf) & (flat < float(PEER_TOPK ** 2)), 1.0, 0.0)
        top = v1[0] + v2[0]
        z = jnp.sum(picked * jnp.exp(cand0 - top), axis=0, keepdims=True)
        counts = [jnp.sum(picked[group_start[i]:group_start[i] + group_len[i]], axis=0, keepdims=True)
                  for i in range(PEER_TOPK)]

        cnt = jnp.zeros((PEER_NKEYS, tl), F32)
        rank_b = jnp.full((PEER_NKEYS, tl), float(PEER_NKEYS), F32)
        for i in range(PEER_TOPK):
            cnt = jnp.where(kio == i1[i], counts[i], cnt)
            rank_b = jnp.where(kio == i2[i], float(i), rank_b)
        outs = ((f_ref, jnp.exp(scores[0] - v1[0]) * (0.5 / z)),
                (g_ref, jnp.exp(scores[1] - v2[0])), (cnt_ref, cnt), (rb_ref, rank_b))
        for ref, val in outs:
            for ch in range(tl // LANE):
                ref[h, ch] = val[:, ch * LANE:(ch + 1) * LANE].astype(ref.dtype)


def _route(h2, wqt, sk):
    t, d = h2.shape
    spec = pl.BlockSpec((PEER_HEADS, ROUTE_TL // LANE, PEER_NKEYS, LANE), lambda i: (0, i, 0, 0))
    out = jax.ShapeDtypeStruct((PEER_HEADS, t // LANE, PEER_NKEYS, LANE), F32)
    out16 = jax.ShapeDtypeStruct((PEER_HEADS, t // LANE, PEER_NKEYS, LANE), BF16)
    return pl.pallas_call(
        _route_kernel,
        grid=(t // ROUTE_TL,),
        in_specs=[pl.BlockSpec((ROUTE_TL, d), lambda i: (i, 0)),
                  pl.BlockSpec(wqt.shape, lambda i: (0, 0)),
                  pl.BlockSpec(sk.shape, lambda i: (0, 0, 0))],
        out_specs=[spec, spec, spec, spec],
        out_shape=[out, out, out16, out16],
        compiler_params=_cparams("parallel"),
        name="peer_route",
    )(h2, wqt, sk)


def _expert_kernel(h2_ref, u_ref, vt_ref, f_ref, cnt_ref, g_ref, rb_ref, x1_ref, mod_ref, gp_ref,
                   o_ref, acc_ref, act_ref, coef_ref):
    e = pl.program_id(1)
    n_chunks = EXP_TL // LANE
    a_per_step = EXP_TE // PEER_NKEYS
    a_group = 2

    @pl.when(e == 0)
    def _():
        acc_ref[...] = jnp.zeros_like(acc_ref)

    act = lax.dot_general(u_ref[...], h2_ref[...], NT_DIMS, preferred_element_type=F32)
    for c in range(n_chunks):
        act_ref[c] = act[:, c * LANE:(c + 1) * LANE]

    def lane_chunk(c, carry):
        for ag in range(a_per_step // a_group):
            coefs = [jnp.zeros((PEER_NKEYS, LANE), BF16) for _ in range(a_group)]

            def row_tile(ref, h, a):
                row = jnp.broadcast_to(ref[h, c, pl.ds(a, 1), :], (BF16_ROWS, LANE)).astype(BF16)
                return jnp.broadcast_to(row[None], (PEER_NKEYS // BF16_ROWS, BF16_ROWS, LANE)
                                        ).reshape(PEER_NKEYS, LANE)

            for h in range(PEER_HEADS):
                rank_b = rb_ref[h, c]
                g_b = g_ref[h, c]
                for k in range(a_group):
                    a = e * a_per_step + ag * a_group + k
                    picked = jnp.maximum(jnp.minimum(row_tile(cnt_ref, h, a) - rank_b, g_b), 0.0)
                    coefs[k] = coefs[k] + row_tile(f_ref, h, a) * picked
            for k in range(a_group):
                rows = pl.ds((ag * a_group + k) * PEER_NKEYS, PEER_NKEYS)
                coef_ref[c, rows, :] = _gelu_x2(act_ref[c, rows, :]).astype(BF16) * coefs[k]
        return carry

    lax.fori_loop(0, n_chunks, lane_chunk, 0)
    coef = jnp.concatenate([coef_ref[c] for c in range(n_chunks)], axis=1)
    acc_ref[...] += jnp.dot(vt_ref[...], coef, preferred_element_type=F32)

    @pl.when(e == pl.num_programs(1) - 1)
    def _():
        y = acc_ref[...].T
        o_ref[...] = x1_ref[...] + mod_ref[0] * _rms(y, gp_ref[...])


def _experts(h2, u, vt, f, cnt, g, rb, x1, gt2, gpost, seq):
    t, d = h2.shape
    n_exp = u.shape[0]
    tiles_per_seq = seq // EXP_TL
    n_chunks = EXP_TL // LANE
    route = pl.BlockSpec((PEER_HEADS, n_chunks, PEER_NKEYS, LANE), lambda i, e: (0, i, 0, 0))
    return pl.pallas_call(
        _expert_kernel,
        grid=(t // EXP_TL, n_exp // EXP_TE),
        in_specs=[pl.BlockSpec((EXP_TL, d), lambda i, e: (i, 0)),
                  pl.BlockSpec((EXP_TE, d), lambda i, e: (e, 0)),
                  pl.BlockSpec((d, EXP_TE), lambda i, e: (0, e)),
                  route, route, route, route,
                  pl.BlockSpec((EXP_TL, d), lambda i, e: (i, 0)),
                  pl.BlockSpec((1, 1, d), lambda i, e: (i // tiles_per_seq, 0, 0)),
                  pl.BlockSpec((1, d), lambda i, e: (0, 0))],
        out_specs=pl.BlockSpec((EXP_TL, d), lambda i, e: (i, 0)),
        out_shape=jax.ShapeDtypeStruct((t, d), F32),
        scratch_shapes=[pltpu.VMEM((d, EXP_TL), F32), pltpu.VMEM((n_chunks, EXP_TE, LANE), F32),
                        pltpu.VMEM((n_chunks, EXP_TE, LANE), BF16)],
        compiler_params=_cparams("parallel", "arbitrary"),
        name="peer_experts",
    )(h2, u, vt, f, cnt, g, rb, x1, gt2, gpost)


def _pad_heads(w, n, width):
    d = w.shape[0]
    return jnp.pad(w.reshape(d, n, width), ((0, 0), (0, 0), (0, LANE - width))).reshape(d, n * LANE)


def _split_w_in(w_in):
    q_cols = N_HEADS * D_QK
    k_cols = N_GROUPS * D_QK
    v_cols = N_GROUPS * D_V
    sizes = (q_cols, k_cols, k_cols, k_cols, v_cols, v_cols, v_cols, 3 * N_HEADS, 2 * D_MODEL, 2 * D_MODEL)
    offs = np.cumsum((0,) + sizes)
    parts = [w_in[:, offs[k]:offs[k + 1]] for k in range(len(sizes))]
    wq, wkc, wks, wkw, wvc, wvs, wvw, wgate, wglu, wmerge = parts
    w_qkv = jnp.concatenate(
        [_pad_heads(wq, N_HEADS, D_QK)]
        + [_pad_heads(w, N_GROUPS, D_QK) for w in (wkc, wks, wkw)]
        + [_pad_heads(w, N_GROUPS, D_V) for w in (wvc, wvs, wvw)], axis=1)
    d = w_in.shape[0]
    wg = wgate.reshape(d, 3, N_GROUPS, HEADS_PER_GROUP).transpose(0, 2, 1, 3)
    wg = wg.reshape(d, N_GROUPS, 3 * HEADS_PER_GROUP)
    wg = jnp.pad(wg, ((0, 0), (0, 0), (0, LANE - 3 * HEADS_PER_GROUP))).reshape(d, N_GROUPS * LANE)
    w_rest = jnp.concatenate([wglu, wmerge, wg], axis=1)
    return w_qkv.astype(BF16), w_rest.astype(BF16)


def _cmp_weights(w1, w2, pos, dh):
    hidden = w1.shape[1]
    w1p = jnp.pad(w1.reshape(CMP_BLOCK, dh, hidden), ((0, 0), (0, LANE - dh), (0, 0)))
    w1p = w1p.reshape(2, CMP_STRIDE * LANE, hidden).astype(BF16)
    w2p = jnp.pad(w2, ((0, 0), (0, LANE - dh))).astype(BF16)
    posp = jnp.pad(pos, ((0, 0), (0, LANE - dh))).reshape(2, CMP_STRIDE * LANE)
    return w1p, w2p, posp


def kernel(x, c, w_ada, b_ada, g_pre_mix, g_post_mix, g_pre_ffn, g_post_ffn, rel_table, w_in,
           cmp_w1k, cmp_w2k, cmp_pos_k, cmp_w1v, cmp_w2v, cmp_pos_v, w_attn_out,
           conv_w, conv_b, conv_ln_g, conv_ln_b, w_conv_out, w_out,
           peer_wq, peer_subkeys, peer_u, peer_v):
    bsz, seq, d = x.shape
    t = bsz * seq
    assert d == D_MODEL and w_ada.shape[0] == 1, "single-layer block with D_MODEL channels"
    assert seq % PROJ_TM == 0 and seq // CMP_STRIDE == LANE and seq % EXP_TL == 0
    x2d = x.reshape(t, d)

    mod = _ada(c, w_ada[0], b_ada[0]).reshape(bsz, 6, d)
    mod_in = mod[:, 0:2]
    mod_tail = mod[:, 2:5]
    mod_out = mod[:, 5:6]

    bkt_c, bkt_t, bkt_w = _static_buckets(seq)
    bias_c = _bias_table(rel_table, bkt_c)
    bias_t = _bias_table(rel_table, bkt_t).reshape(N_HEADS, 3, TQ, TQ).transpose(1, 0, 2, 3)
    bias_w = _bias_table(rel_table, bkt_w)

    w_qkv, w_rest = _split_w_in(w_in[0])
    qkv = _project(x2d, mod_in, g_pre_mix, w_qkv, seq, heads_out=True)
    rest = _project(x2d, mod_in, g_pre_mix, w_rest, seq, heads_out=False)

    n_chunk = seq // CMP_STRIDE
    kch = qkv[:, QKV_KC0:QKV_KC0 + N_GROUPS].reshape(bsz, N_GROUPS, n_chunk, CMP_STRIDE * LANE)
    vch = qkv[:, QKV_VC0:QKV_VC0 + N_GROUPS].reshape(bsz, N_GROUPS, n_chunk, CMP_STRIDE * LANE)
    w1k, w2k, pk = _cmp_weights(cmp_w1k[0], cmp_w2k[0], cmp_pos_k[0], D_QK)
    w1v, w2v, pv = _cmp_weights(cmp_w1v[0], cmp_w2v[0], cmp_pos_v[0], D_V)
    kc, vc = _compress(kch, vch, pk, pv, w1k, w2k, w1v, w2v)

    pad = ((0, 0), (0, 0), (WINDOW, 0), (0, 0))
    kwp = jnp.pad(qkv[:, QKV_KW0:QKV_KW0 + N_GROUPS], pad)
    vwp = jnp.pad(qkv[:, QKV_VW0:QKV_VW0 + N_GROUPS], pad)

    n_idx = np.arange(LANE)[:, None] * CMP_STRIDE
    j_idx = np.arange(LANE)[None, :] * SLC_BLOCK
    overlap = ((n_idx <= j_idx + SLC_BLOCK - 1) & (n_idx + CMP_BLOCK - 1 >= j_idx)
               & (np.arange(LANE)[None, :] < seq // SLC_BLOCK)
               & (np.arange(LANE)[:, None] < n_chunk - 1)).astype(np.float32)
    expand = (np.arange(LANE)[:, None] == np.arange(seq)[None, :] // SLC_BLOCK)
    attn = _attention(qkv, kwp, vwp, kc, vc, rest, bias_c, bias_t, bias_w,
                      jnp.asarray(overlap), jnp.asarray(expand, dtype=BF16))

    cvec = jnp.stack([conv_b[0], conv_ln_g[0], conv_ln_b[0]])
    gvec = jnp.stack([g_post_mix[0], g_pre_ffn[0]])
    x1, h2 = _mixer_tail(attn.reshape(t, N_HEADS * D_V), rest, x2d, mod_tail,
                         w_attn_out[0].astype(BF16), w_conv_out[0].astype(BF16),
                         w_out[0].astype(BF16), conv_w[0, :, 0, :], cvec, gvec, seq)

    wqt = peer_wq[0].T.astype(BF16)
    sk = peer_subkeys[0].reshape(2 * PEER_HEADS, PEER_NKEYS, PEER_DQ // 2).astype(BF16)
    f, cnt, g, rb = _route(h2, wqt, sk)
    out = _experts(h2, peer_u[0].astype(BF16), peer_v[0].T.astype(BF16), f, cnt, g, rb,
                   x1, mod_out, g_post_ffn, seq)
    return out.reshape(bsz, seq, d)
```

```python
import functools
import math

import numpy as np
import jax
import jax.numpy as jnp
from jax import lax
from jax.experimental import pallas as pl
from jax.experimental.pallas import tpu as pltpu

F32 = jnp.float32
BF16 = jnp.bfloat16

D_MODEL = 1024
N_HEADS = 16
N_GROUPS = 4
HEADS_PER_GROUP = N_HEADS // N_GROUPS
D_QK = 96
D_V = 64
CMP_BLOCK = 32
CMP_STRIDE = 16
CMP_HIDDEN = 256
SLC_BLOCK = 64
SLC_TOPN = 16
WINDOW = 512
ATTN_SCALE = D_QK ** -0.5
FORCE_BONUS = 1e6
N_BUCKETS = 32
MAX_DISTANCE = 128
CONV_WIDTH = 31
PEER_HEADS = 8
PEER_NKEYS = 128
PEER_DQ = 256
PEER_TOPK = 16
NORM_EPS = 1e-6
NEG_INF = -1e30

LANE = 128
BF16_ROWS = 16
VMEM_LIMIT = 56 * 1024 * 1024

TQ = 256
SLC_CHUNK = 256
WIN_SPAN = TQ + WINDOW
PROJ_TM = 2048
PROJ_TN = 512
TAIL_TM = 256
HALO = 32
ROUTE_TL = 256
EXP_TL = 512
EXP_TE = 1024

QKV_Q0, QKV_KC0, QKV_KS0, QKV_KW0, QKV_VC0, QKV_VS0, QKV_VW0 = 0, 16, 20, 24, 28, 32, 36
QKV_HEADS = 40
REST_GLU_A, REST_GLU_G, REST_MRG_A, REST_MRG_B = 0, 1, 2, 3
REST_GATE0 = 4 * D_MODEL
REST_COLS = 4 * D_MODEL + N_GROUPS * LANE

NT_DIMS = (((1,), (1,)), ((), ()))


def _cparams(*sem):
    return pltpu.CompilerParams(dimension_semantics=sem, vmem_limit_bytes=VMEM_LIMIT)


def _gelu(x):
    return 0.5 * x * (1.0 + jnp.tanh(math.sqrt(2.0 / math.pi) * (x + 0.044715 * (x * x * x))))


def _gelu_x2(x):
    k1 = math.sqrt(2.0 / math.pi)
    return x * (1.0 + jnp.tanh(x * (k1 + (k1 * 0.044715) * (x * x))))


def _sigmoid(x):
    return 1.0 / (1.0 + jnp.exp(-x))


def _rms(x, g):
    return x * lax.rsqrt(jnp.mean(x * x, axis=-1, keepdims=True) + NORM_EPS) * g


def _ada_kernel(c_ref, w_ref, b_ref, o_ref):
    c = c_ref[...]
    c_act = c * _sigmoid(c)
    o_ref[...] = jnp.dot(c_act, w_ref[...], preferred_element_type=F32,
                         precision=lax.Precision.HIGHEST) + b_ref[...]


def _ada(c, w, b):
    bsz, d = c.shape
    n = w.shape[1]
    tn = 1024
    return pl.pallas_call(
        _ada_kernel,
        grid=(n // tn,),
        in_specs=[pl.BlockSpec((bsz, d), lambda j: (0, 0)),
                  pl.BlockSpec((d, tn), lambda j: (0, j)),
                  pl.BlockSpec((1, tn), lambda j: (0, j))],
        out_specs=pl.BlockSpec((bsz, tn), lambda j: (0, j)),
        out_shape=jax.ShapeDtypeStruct((bsz, n), F32),
        compiler_params=_cparams("parallel"),
        name="ada",
    )(c, w, b.reshape(1, n))


def _t5_bucket_np(dist):
    n = np.maximum(dist, 0)
    max_exact = N_BUCKETS // 2
    nf = np.maximum(n, 1).astype(np.float64)
    large = max_exact + (np.log(nf / max_exact) / math.log(MAX_DISTANCE / max_exact)
                         * (N_BUCKETS - max_exact)).astype(np.int64)
    return np.where(n < max_exact, n, np.minimum(large, N_BUCKETS - 1)).astype(np.int32)


def _bias_kernel(tab_ref, bkt_ref, o_ref):
    h = pl.program_id(0)
    bkt = bkt_ref[...]
    acc = jnp.where(bkt < 0, NEG_INF, 0.0).astype(F32)
    for i in range(N_BUCKETS):
        acc = jnp.where(bkt == i, tab_ref[i, h], acc)
    o_ref[0] = acc


def _bias_table(rel_table, bucket):
    rows, cols = bucket.shape
    return pl.pallas_call(
        _bias_kernel,
        grid=(N_HEADS,),
        in_specs=[pl.BlockSpec(memory_space=pltpu.SMEM),
                  pl.BlockSpec((rows, cols), lambda h: (0, 0))],
        out_specs=pl.BlockSpec((1, rows, cols), lambda h: (h, 0, 0)),
        out_shape=jax.ShapeDtypeStruct((N_HEADS, rows, cols), F32),
        compiler_params=_cparams("parallel"),
        name="bias_table",
    )(rel_table, jnp.asarray(bucket))


def _static_buckets(seq):
    t = np.arange(seq)[:, None]
    n = np.arange(LANE)[None, :]
    dist_c = t - (n * CMP_STRIDE + CMP_BLOCK - 1)
    bkt_c = np.where(dist_c >= 0, _t5_bucket_np(dist_c), -1).astype(np.int32)
    qi = np.arange(TQ)[:, None]
    kj = np.arange(TQ)[None, :]
    tiles = [_t5_bucket_np(d * TQ + qi - kj) for d in range(3)]
    bkt_t = np.concatenate(tiles, axis=0).astype(np.int32)
    kw = np.arange(WIN_SPAN)[None, :]
    dist_w = WINDOW + qi - kw
    band = (dist_w >= 0) & (dist_w < WINDOW)
    bkt_w = np.where(band, _t5_bucket_np(dist_w), -1).astype(np.int32)
    return bkt_c, bkt_t, bkt_w


def _proj_kernel(x_ref, mod_ref, g_ref, w_ref, o_ref, h_ref, *, heads_out):
    @pl.when(pl.program_id(1) == 0)
    def _():
        x = x_ref[...]
        sh = mod_ref[0, 0:1, :]
        sc = mod_ref[0, 1:2, :]
        h_ref[...] = (_rms(x, g_ref[...]) * (1.0 + sc) + sh).astype(BF16)

    res = jnp.dot(h_ref[...], w_ref[...], preferred_element_type=F32).astype(o_ref.dtype)
    if heads_out:
        for k in range(PROJ_TN // LANE):
            o_ref[0, k] = res[:, k * LANE:(k + 1) * LANE]
    else:
        o_ref[...] = res


def _project(x2d, mod3, g, w, seq, heads_out):
    t, d = x2d.shape
    n = w.shape[1]
    bsz = t // seq
    tiles_per_seq = seq // PROJ_TM
    grid = (t // PROJ_TM, n // PROJ_TN)
    if heads_out:
        hpt = PROJ_TN // LANE
        out_shape = jax.ShapeDtypeStruct((bsz, n // LANE, seq, LANE), BF16)
        out_spec = pl.BlockSpec((1, hpt, PROJ_TM, LANE),
                                lambda i, j: (i // tiles_per_seq, j, i % tiles_per_seq, 0))
    else:
        out_shape = jax.ShapeDtypeStruct((t, n), BF16)
        out_spec = pl.BlockSpec((PROJ_TM, PROJ_TN), lambda i, j: (i, j))
    return pl.pallas_call(
        functools.partial(_proj_kernel, heads_out=heads_out),
        grid=grid,
        in_specs=[pl.BlockSpec((PROJ_TM, d), lambda i, j: (i, 0)),
                  pl.BlockSpec((1, 2, d), lambda i, j: (i // tiles_per_seq, 0, 0)),
                  pl.BlockSpec((1, d), lambda i, j: (0, 0)),
                  pl.BlockSpec((d, PROJ_TN), lambda i, j: (0, j))],
        out_specs=out_spec,
        out_shape=out_shape,
        scratch_shapes=[pltpu.VMEM((PROJ_TM, d), BF16)],
        compiler_params=_cparams("parallel", "arbitrary"),
        name="proj_heads" if heads_out else "proj_rest",
    )(x2d, mod3, g, w)


def _compress_kernel(kch_ref, vch_ref, pk_ref, pv_ref, w1k_ref, w2k_ref, w1v_ref, w2v_ref,
                     kc_ref, vc_ref):
    def one(ch_ref, pos_ref, w1_ref, w2_ref, o_ref):
        a = ch_ref[0, 0].astype(F32)
        a1 = (a + pos_ref[0:1, :]).astype(BF16)
        a2 = (a + pos_ref[1:2, :]).astype(BF16)
        p1 = jnp.dot(a1, w1_ref[0], preferred_element_type=F32)
        p2 = jnp.dot(a2, w1_ref[1], preferred_element_type=F32)
        n_chunk = p2.shape[0]
        hid = _gelu(p1 + pltpu.roll(p2, n_chunk - 1, 0))
        o_ref[0, 0] = jnp.dot(hid.astype(BF16), w2_ref[...],
                              preferred_element_type=F32).astype(BF16)

    one(kch_ref, pk_ref, w1k_ref, w2k_ref, kc_ref)
    one(vch_ref, pv_ref, w1v_ref, w2v_ref, vc_ref)


def _compress(kch, vch, pk, pv, w1k, w2k, w1v, w2v):
    bsz, g, n_chunk, width = kch.shape
    assert n_chunk == LANE, "compressed keys are laid out on one 128-row tile"
    spec_in = pl.BlockSpec((1, 1, n_chunk, width), lambda b, gg: (b, gg, 0, 0))
    spec_out = pl.BlockSpec((1, 1, n_chunk, LANE), lambda b, gg: (b, gg, 0, 0))
    full = lambda a: pl.BlockSpec(a.shape, lambda b, gg: (0,) * a.ndim)
    out = jax.ShapeDtypeStruct((bsz, g, n_chunk, LANE), BF16)
    return pl.pallas_call(
        _compress_kernel,
        grid=(bsz, g),
        in_specs=[spec_in, spec_in, full(pk), full(pv), full(w1k), full(w2k), full(w1v), full(w2v)],
        out_specs=[spec_out, spec_out],
        out_shape=[out, out],
        compiler_params=_cparams("parallel", "parallel"),
        name="compress",
    )(kch, vch, pk, pv, w1k, w2k, w1v, w2v)


def _attn_kernel(q_ref, kc_ref, vc_ref, ks_ref, vs_ref, kw_ref, vw_ref, gate_ref,
                 bias_c_ref, bias_t_ref, bias_w_ref, overlap_ref, expand_ref,
                 o_ref, mask_ref):
    qi = pl.program_id(2)
    rows = HEADS_PER_GROUP * TQ
    seq = mask_ref.shape[1]
    q4 = q_ref[0].reshape(rows, LANE)

    def per_head(x):
        return jnp.broadcast_to(x[None], (HEADS_PER_GROUP,) + x.shape).reshape(rows, x.shape[-1])

    lc = lax.dot_general(q4, kc_ref[0, 0], NT_DIMS, preferred_element_type=F32)
    lc = lc + bias_c_ref[...].reshape(rows, LANE)
    valid = lc > 0.5 * NEG_INF
    mc = jnp.max(lc, axis=-1, keepdims=True)
    ec = jnp.where(valid, jnp.exp(lc - mc), 0.0)
    sc = jnp.sum(ec, axis=-1, keepdims=True)
    pc = ec / jnp.where(sc > 0.0, sc, 1.0)
    o_cmp = jnp.dot(pc.astype(BF16), vc_ref[0, 0], preferred_element_type=F32)

    psum = jnp.sum(pc.reshape(HEADS_PER_GROUP, TQ, LANE), axis=0)
    imp = jnp.dot(psum, overlap_ref[...], preferred_element_type=F32,
                  precision=lax.Precision.HIGHEST)
    t_pos = qi * TQ + lax.broadcasted_iota(jnp.int32, (TQ, LANE), 0)
    j_blk = lax.broadcasted_iota(jnp.int32, (TQ, LANE), 1)
    cur = t_pos // SLC_BLOCK
    forced = (j_blk == 0) | (j_blk == cur) | (j_blk == cur - 1)
    score = jnp.where(j_blk <= cur, imp + jnp.where(forced, FORCE_BONUS, 0.0), NEG_INF)
    n_slc = seq // SLC_BLOCK
    sc_t = score.T[:n_slc]
    j_row = lax.broadcasted_iota(jnp.int32, (n_slc, TQ), 0)
    rank = jnp.zeros((n_slc, TQ), F32)
    for jp in range(n_slc):
        row = sc_t[jp:jp + 1, :]
        ge = jnp.where(row >= sc_t, 1.0, 0.0)
        gt = jnp.where(row > sc_t, 1.0, 0.0)
        rank = rank + jnp.where(j_row > jp, ge, gt)
    sel_t = jnp.where(rank < float(min(SLC_TOPN, n_slc)), 1.0, 0.0)
    sel_t = jnp.concatenate([sel_t, jnp.zeros((LANE - n_slc, TQ), F32)], axis=0)
    sel = sel_t.T.astype(BF16)
    sel_keys = jnp.dot(sel, expand_ref[...], preferred_element_type=F32)
    k_pos = lax.broadcasted_iota(jnp.int32, (TQ, seq), 1)
    q_pos = qi * TQ + lax.broadcasted_iota(jnp.int32, (TQ, seq), 0)
    mask_ref[...] = jnp.where((sel_keys > 0.5) & (k_pos <= q_pos), 0.0, NEG_INF)

    tiles_per_chunk = SLC_CHUNK // TQ

    def slc_step(c, carry):
        m, l, acc = carry
        k0 = pl.multiple_of(c * SLC_CHUNK, SLC_CHUNK)
        kt = ks_ref[0, 0, pl.ds(k0, SLC_CHUNK), :]
        vt = vs_ref[0, 0, pl.ds(k0, SLC_CHUNK), :]
        s = lax.dot_general(q4, kt, NT_DIMS, preferred_element_type=F32)
        bias = jnp.concatenate(
            [bias_t_ref[jnp.clip(qi - (c * tiles_per_chunk + k), 0, 2)]
             for k in range(tiles_per_chunk)], axis=-1)
        s = s + bias.reshape(rows, SLC_CHUNK) + per_head(mask_ref[:, pl.ds(k0, SLC_CHUNK)])
        m_new = jnp.maximum(m, jnp.max(s, axis=-1, keepdims=True))
        alpha = jnp.exp(m - m_new)
        p = jnp.exp(s - m_new)
        l = alpha * l + jnp.sum(p, axis=-1, keepdims=True)
        acc = alpha * acc + jnp.dot(p.astype(BF16), vt, preferred_element_type=F32)
        return m_new, l, acc

    n_chunks = (qi + tiles_per_chunk) // tiles_per_chunk
    init = (jnp.full((rows, 1), -jnp.inf, F32), jnp.zeros((rows, 1), F32),
            jnp.zeros((rows, LANE), F32))
    _, l_s, acc_s = lax.fori_loop(0, n_chunks, slc_step, init)
    o_slc = acc_s / l_s

    w0 = pl.multiple_of(qi * TQ, TQ)
    kwin = kw_ref[0, 0, pl.ds(w0, WIN_SPAN), :]
    vwin = vw_ref[0, 0, pl.ds(w0, WIN_SPAN), :]
    sw = lax.dot_general(q4, kwin, NT_DIMS, preferred_element_type=F32)
    sw = sw + bias_w_ref[...].reshape(rows, WIN_SPAN)
    kj = lax.broadcasted_iota(jnp.int32, (rows, WIN_SPAN), 1)
    sw = jnp.where(kj >= WINDOW - qi * TQ, sw, NEG_INF)
    mw = jnp.max(sw, axis=-1, keepdims=True)
    pw = jnp.exp(sw - mw)
    o_win = jnp.dot(pw.astype(BF16), vwin, preferred_element_type=F32)
    o_win = o_win / jnp.sum(pw, axis=-1, keepdims=True)

    gates = _sigmoid(gate_ref[...].astype(F32))
    outs = []
    for r in range(HEADS_PER_GROUP):
        sl = slice(r * TQ, (r + 1) * TQ)
        g_c = gates[:, r:r + 1]
        g_s = gates[:, HEADS_PER_GROUP + r:HEADS_PER_GROUP + r + 1]
        g_w = gates[:, 2 * HEADS_PER_GROUP + r:2 * HEADS_PER_GROUP + r + 1]
        o = g_c * o_cmp[sl] + g_s * o_slc[sl] + g_w * o_win[sl]
        outs.append(o[:, :D_V])
    o_ref[0] = jnp.concatenate(outs, axis=-1).astype(o_ref.dtype)


def _attention(qkv, kwp, vwp, kc, vc, rest, bias_c, bias_t, bias_w, overlap, expand):
    bsz, _, seq, _ = qkv.shape
    nq = seq // TQ
    g_cols0 = REST_GATE0 // LANE
    head = lambda base: pl.BlockSpec((1, 1, seq, LANE), lambda b, g, i: (b, base + g, 0, 0))
    in_specs = [
        pl.BlockSpec((1, HEADS_PER_GROUP, TQ, LANE), lambda b, g, i: (b, g, i, 0)),
        pl.BlockSpec((1, 1, LANE, LANE), lambda b, g, i: (b, g, 0, 0)),
        pl.BlockSpec((1, 1, LANE, LANE), lambda b, g, i: (b, g, 0, 0)),
        head(QKV_KS0), head(QKV_VS0),
        pl.BlockSpec((1, 1, seq + WINDOW, LANE), lambda b, g, i: (b, g, 0, 0)),
        pl.BlockSpec((1, 1, seq + WINDOW, LANE), lambda b, g, i: (b, g, 0, 0)),
        pl.BlockSpec((TQ, LANE), lambda b, g, i: (b * nq + i, g_cols0 + g)),
        pl.BlockSpec((HEADS_PER_GROUP, TQ, LANE), lambda b, g, i: (g, i, 0)),
        pl.BlockSpec((3, HEADS_PER_GROUP, TQ, TQ), lambda b, g, i: (0, g, 0, 0)),
        pl.BlockSpec((HEADS_PER_GROUP, TQ, WIN_SPAN), lambda b, g, i: (g, 0, 0)),
        pl.BlockSpec((LANE, LANE), lambda b, g, i: (0, 0)),
        pl.BlockSpec((LANE, seq), lambda b, g, i: (0, 0)),
    ]
    return pl.pallas_call(
        _attn_kernel,
        grid=(bsz, N_GROUPS, nq),
        in_specs=in_specs,
        out_specs=pl.BlockSpec((1, TQ, HEADS_PER_GROUP * D_V), lambda b, g, i: (b, i, g)),
        out_shape=jax.ShapeDtypeStruct((bsz, seq, N_HEADS * D_V), BF16),
        scratch_shapes=[pltpu.VMEM((TQ, seq), F32)],
        compiler_params=_cparams("parallel", "parallel", "arbitrary"),
        name="attention",
    )(qkv, kc, vc, qkv, qkv, kwp, vwp, rest, bias_c, bias_t, bias_w, overlap, expand)


def _tail_kernel(attn_ref, ga_ref, gg_ref, ha_ref, hg_ref, ma_ref, mb_ref, x_ref, mod_ref,
                 wa_ref, wc_ref, wo_ref, cw_ref, cvec_ref, gvec_ref,
                 x1_ref, h2_ref, u_ref, *, tiles_per_seq):
    first = (pl.program_id(0) % tiles_per_seq) == 0
    y_a = jnp.dot(attn_ref[...], wa_ref[...], preferred_element_type=F32)

    halo = ha_ref[...].astype(F32) * _sigmoid(hg_ref[...].astype(F32))
    u_ref[0:HALO, :] = jnp.where(first, 0.0, halo)
    u_ref[HALO:, :] = ga_ref[...].astype(F32) * _sigmoid(gg_ref[...].astype(F32))

    conv_b, ln_g, ln_b = cvec_ref[0:1, :], cvec_ref[1:2, :], cvec_ref[2:3, :]
    blk = 32
    pieces = []
    for rb in range(TAIL_TM // blk):
        acc = jnp.zeros((blk, D_MODEL), F32)
        for j in range(CONV_WIDTH):
            start = HALO - (CONV_WIDTH - 1) + rb * blk + j
            acc = acc + cw_ref[j:j + 1, :] * u_ref[start:start + blk, :]
        pieces.append(acc)
    y = jnp.concatenate(pieces, axis=0) + conv_b
    yc = y - jnp.mean(y, axis=-1, keepdims=True)
    yn = yc * lax.rsqrt(jnp.mean(yc * yc, axis=-1, keepdims=True) + NORM_EPS) * ln_g + ln_b
    act = yn * _sigmoid(yn)
    y_b = jnp.dot(act.astype(BF16), wc_ref[...], preferred_element_type=F32)

    merged = (_sigmoid(ma_ref[...].astype(F32)) * y_a + _sigmoid(mb_ref[...].astype(F32)) * y_b)
    out = jnp.dot(merged.astype(BF16), wo_ref[...], preferred_element_type=F32)

    gt1, sh2, sc2 = mod_ref[0, 0:1, :], mod_ref[0, 1:2, :], mod_ref[0, 2:3, :]
    x1 = x_ref[...] + gt1 * _rms(out, gvec_ref[0:1, :])
    x1_ref[...] = x1
    h2_ref[...] = (_rms(x1, gvec_ref[1:2, :]) * (1.0 + sc2) + sh2).astype(BF16)


def _mixer_tail(attn2d, rest, x2d, mod3, wa, wc, wo, cw, cvec, gvec, seq):
    t, d = x2d.shape
    tiles_per_seq = seq // TAIL_TM
    hpt = TAIL_TM // HALO
    row = lambda cb: pl.BlockSpec((TAIL_TM, d), lambda i: (i, cb))
    halo = lambda cb: pl.BlockSpec((HALO, d), lambda i: (jnp.maximum(i * hpt - 1, 0), cb))
    full = lambda a: pl.BlockSpec(a.shape, lambda i: (0,) * a.ndim)
    return pl.pallas_call(
        functools.partial(_tail_kernel, tiles_per_seq=tiles_per_seq),
        grid=(t // TAIL_TM,),
        in_specs=[row(0), row(REST_GLU_A), row(REST_GLU_G), halo(REST_GLU_A), halo(REST_GLU_G),
                  row(REST_MRG_A), row(REST_MRG_B), row(0),
                  pl.BlockSpec((1, 3, d), lambda i: (i // tiles_per_seq, 0, 0)),
                  full(wa), full(wc), full(wo), full(cw), full(cvec), full(gvec)],
        out_specs=[row(0), row(0)],
        out_shape=[jax.ShapeDtypeStruct((t, d), F32), jax.ShapeDtypeStruct((t, d), BF16)],
        scratch_shapes=[pltpu.VMEM((HALO + TAIL_TM, d), F32)],
        compiler_params=_cparams("parallel"),
        name="mixer_tail",
    )(attn2d, rest, rest, rest, rest, rest, rest, x2d, mod3, wa, wc, wo, cw, cvec, gvec)


def _cand_pairs():
    return [(i, j) for i in range(PEER_TOPK) for j in range(PEER_TOPK)
            if (i + 1) * (j + 1) <= PEER_TOPK]


def _top16_rows(s):
    n = s.shape[0]
    kio = lax.broadcasted_iota(jnp.int32, s.shape, 0).astype(F32)
    vals, idxs = [], []
    for _ in range(PEER_TOPK):
        m = jnp.max(s, axis=0, keepdims=True)
        idx = jnp.min(jnp.where(s == m, kio, float(n)), axis=0, keepdims=True)
        vals.append(m)
        idxs.append(idx)
        s = jnp.where(kio == idx, -jnp.inf, s)
    return vals, idxs


def _route_kernel(h2_ref, wqt_ref, sk_ref, f_ref, cnt_ref, g_ref, rb_ref):
    q2t = lax.dot_general(wqt_ref[...], h2_ref[...], NT_DIMS, preferred_element_type=F32)
    tl = q2t.shape[1]
    kio = lax.broadcasted_iota(jnp.int32, (PEER_NKEYS, tl), 0).astype(F32)
    pairs = _cand_pairs()
    n_pad = -len(pairs) % 8
    flat_ids = np.array([i * PEER_TOPK + j for i, j in pairs] + [PEER_TOPK ** 2] * n_pad, np.float32)
    group_start = [min(k for k, (i, _) in enumerate(pairs) if i == ii) for ii in range(PEER_TOPK)]
    group_len = [sum(1 for (i, _) in pairs if i == ii) for ii in range(PEER_TOPK)]
    n_rows = len(pairs) + n_pad
    flat_col = lax.broadcasted_iota(jnp.int32, (n_rows, tl), 0)
    flat = jnp.zeros((n_rows, tl), F32)
    for k in range(n_rows):
        flat = jnp.where(flat_col == k, float(flat_ids[k]), flat)

    for h in range(PEER_HEADS):
        scores, tops = [], []
        for c in range(2):
            hc = 2 * h + c
            qt = q2t[hc * LANE:(hc + 1) * LANE, :].astype(BF16)
            s = jnp.dot(sk_ref[hc], qt, preferred_element_type=F32)
            scores.append(s)
            tops.append(_top16_rows(s))
        (v1, i1), (v2, i2) = tops
        cand0 = jnp.concatenate([v1[i] + v2[j] for i, j in pairs]
                                + [jnp.full((n_pad, tl), -jnp.inf, F32)], axis=0)
        cand = cand0
        for _ in range(PEER_TOPK):
            m = jnp.max(cand, axis=0, keepdims=True)
            fid = jnp.min(jnp.where(cand == m, flat, float(PEER_TOPK ** 2 + 1)), axis=0, keepdims=True)
            cand = jnp.where(flat == fid, -jnp.inf, cand)
        picked = jnp.where((cand == -jnp.inf) & (flat < float(PEER_TOPK ** 2)), 1.0, 0.0)
        top = v1[0] + v2[0]
        z = jnp.sum(picked * jnp.exp(cand0 - top), axis=0, keepdims=True)
        counts = [jnp.sum(picked[group_start[i]:group_start[i] + group_len[i]], axis=0, keepdims=True)
                  for i in range(PEER_TOPK)]

        cnt = jnp.zeros((PEER_NKEYS, tl), F32)
        rank_b = jnp.full((PEER_NKEYS, tl), float(PEER_NKEYS), F32)
        for i in range(PEER_TOPK):
            cnt = jnp.where(kio == i1[i], counts[i], cnt)
            rank_b = jnp.where(kio == i2[i], float(i), rank_b)
        outs = ((f_ref, jnp.exp(scores[0] - v1[0]) * (0.5 / z)),
                (g_ref, jnp.exp(scores[1] - v2[0])), (cnt_ref, cnt), (rb_ref, rank_b))
        for ref, val in outs:
            for ch in range(tl // LANE):
                ref[h, ch] = val[:, ch * LANE:(ch + 1) * LANE].astype(ref.dtype)


def _route(h2, wqt, sk):
    t, d = h2.shape
    spec = pl.BlockSpec((PEER_HEADS, ROUTE_TL // LANE, PEER_NKEYS, LANE), lambda i: (0, i, 0, 0))
    out = jax.ShapeDtypeStruct((PEER_HEADS, t // LANE, PEER_NKEYS, LANE), F32)
    out16 = jax.ShapeDtypeStruct((PEER_HEADS, t // LANE, PEER_NKEYS, LANE), BF16)
    return pl.pallas_call(
        _route_kernel,
        grid=(t // ROUTE_TL,),
        in_specs=[pl.BlockSpec((ROUTE_TL, d), lambda i: (i, 0)),
                  pl.BlockSpec(wqt.shape, lambda i: (0, 0)),
                  pl.BlockSpec(sk.shape, lambda i: (0, 0, 0))],
        out_specs=[spec, spec, spec, spec],
        out_shape=[out, out, out16, out16],
        compiler_params=_cparams("parallel"),
        name="peer_route",
    )(h2, wqt, sk)


def _expert_kernel(h2_ref, u_ref, vt_ref, f_ref, cnt_ref, g_ref, rb_ref, x1_ref, mod_ref, gp_ref,
                   o_ref, acc_ref, act_ref, coef_ref):
    e = pl.program_id(1)
    n_chunks = EXP_TL // LANE
    a_per_step = EXP_TE // PEER_NKEYS
    a_group = 2

    @pl.when(e == 0)
    def _():
        acc_ref[...] = jnp.zeros_like(acc_ref)

    act = lax.dot_general(u_ref[...], h2_ref[...], NT_DIMS, preferred_element_type=F32)
    for c in range(n_chunks):
        act_ref[c] = act[:, c * LANE:(c + 1) * LANE]

    def lane_chunk(c, carry):
        for ag in range(a_per_step // a_group):
            coefs = [jnp.zeros((PEER_NKEYS, LANE), BF16) for _ in range(a_group)]

            def row_tile(ref, h, a):
                row = jnp.broadcast_to(ref[h, c, pl.ds(a, 1), :], (BF16_ROWS, LANE)).astype(BF16)
                return jnp.broadcast_to(row[None], (PEER_NKEYS // BF16_ROWS, BF16_ROWS, LANE)
                                        ).reshape(PEER_NKEYS, LANE)

            for h in range(PEER_HEADS):
                rank_b = rb_ref[h, c]
                g_b = g_ref[h, c]
                for k in range(a_group):
                    a = e * a_per_step + ag * a_group + k
                    picked = jnp.maximum(jnp.minimum(row_tile(cnt_ref, h, a) - rank_b, g_b), 0.0)
                    coefs[k] = coefs[k] + row_tile(f_ref, h, a) * picked
            for k in range(a_group):
                rows = pl.ds((ag * a_group + k) * PEER_NKEYS, PEER_NKEYS)
                coef_ref[c, rows, :] = _gelu_x2(act_ref[c, rows, :]).astype(BF16) * coefs[k]
        return carry

    lax.fori_loop(0, n_chunks, lane_chunk, 0)
    coef = jnp.concatenate([coef_ref[c] for c in range(n_chunks)], axis=1)
    acc_ref[...] += jnp.dot(vt_ref[...], coef, preferred_element_type=F32)

    @pl.when(e == pl.num_programs(1) - 1)
    def _():
        y = acc_ref[...].T
        o_ref[...] = x1_ref[...] + mod_ref[0] * _rms(y, gp_ref[...])


def _experts(h2, u, vt, f, cnt, g, rb, x1, gt2, gpost, seq):
    t, d = h2.shape
    n_exp = u.shape[0]
    tiles_per_seq = seq // EXP_TL
    n_chunks = EXP_TL // LANE
    route = pl.BlockSpec((PEER_HEADS, n_chunks, PEER_NKEYS, LANE), lambda i, e: (0, i, 0, 0))
    return pl.pallas_call(
        _expert_kernel,
        grid=(t // EXP_TL, n_exp // EXP_TE),
        in_specs=[pl.BlockSpec((EXP_TL, d), lambda i, e: (i, 0)),
                  pl.BlockSpec((EXP_TE, d), lambda i, e: (e, 0)),
                  pl.BlockSpec((d, EXP_TE), lambda i, e: (0, e)),
                  route, route, route, route,
                  pl.BlockSpec((EXP_TL, d), lambda i, e: (i, 0)),
                  pl.BlockSpec((1, 1, d), lambda i, e: (i // tiles_per_seq, 0, 0)),
                  pl.BlockSpec((1, d), lambda i, e: (0, 0))],
        out_specs=pl.BlockSpec((EXP_TL, d), lambda i, e: (i, 0)),
        out_shape=jax.ShapeDtypeStruct((t, d), F32),
        scratch_shapes=[pltpu.VMEM((d, EXP_TL), F32), pltpu.VMEM((n_chunks, EXP_TE, LANE), F32),
                        pltpu.VMEM((n_chunks, EXP_TE, LANE), BF16)],
        compiler_params=_cparams("parallel", "arbitrary"),
        name="peer_experts",
    )(h2, u, vt, f, cnt, g, rb, x1, gt2, gpost)


def _pad_heads(w, n, width):
    d = w.shape[0]
    return jnp.pad(w.reshape(d, n, width), ((0, 0), (0, 0), (0, LANE - width))).reshape(d, n * LANE)


def _split_w_in(w_in):
    q_cols = N_HEADS * D_QK
    k_cols = N_GROUPS * D_QK
    v_cols = N_GROUPS * D_V
    sizes = (q_cols, k_cols, k_cols, k_cols, v_cols, v_cols, v_cols, 3 * N_HEADS, 2 * D_MODEL, 2 * D_MODEL)
    offs = np.cumsum((0,) + sizes)
    parts = [w_in[:, offs[k]:offs[k + 1]] for k in range(len(sizes))]
    wq, wkc, wks, wkw, wvc, wvs, wvw, wgate, wglu, wmerge = parts
    w_qkv = jnp.concatenate(
        [_pad_heads(wq * ATTN_SCALE, N_HEADS, D_QK)]
        + [_pad_heads(w, N_GROUPS, D_QK) for w in (wkc, wks, wkw)]
        + [_pad_heads(w, N_GROUPS, D_V) for w in (wvc, wvs, wvw)], axis=1)
    d = w_in.shape[0]
    wg = wgate.reshape(d, 3, N_GROUPS, HEADS_PER_GROUP).transpose(0, 2, 1, 3)
    wg = wg.reshape(d, N_GROUPS, 3 * HEADS_PER_GROUP)
    wg = jnp.pad(wg, ((0, 0), (0, 0), (0, LANE - 3 * HEADS_PER_GROUP))).reshape(d, N_GROUPS * LANE)
    w_rest = jnp.concatenate([wglu, wmerge, wg], axis=1)
    return w_qkv.astype(BF16), w_rest.astype(BF16)


def _cmp_weights(w1, w2, pos, dh):
    hidden = w1.shape[1]
    w1p = jnp.pad(w1.reshape(CMP_BLOCK, dh, hidden), ((0, 0), (0, LANE - dh), (0, 0)))
    w1p = w1p.reshape(2, CMP_STRIDE * LANE, hidden).astype(BF16)
    w2p = jnp.pad(w2, ((0, 0), (0, LANE - dh))).astype(BF16)
    posp = jnp.pad(pos, ((0, 0), (0, LANE - dh))).reshape(2, CMP_STRIDE * LANE)
    return w1p, w2p, posp


def kernel(x, c, w_ada, b_ada, g_pre_mix, g_post_mix, g_pre_ffn, g_post_ffn, rel_table, w_in,
           cmp_w1k, cmp_w2k, cmp_pos_k, cmp_w1v, cmp_w2v, cmp_pos_v, w_attn_out,
           conv_w, conv_b, conv_ln_g, conv_ln_b, w_conv_out, w_out,
           peer_wq, peer_subkeys, peer_u, peer_v):
    bsz, seq, d = x.shape
    t = bsz * seq
    assert d == D_MODEL and w_ada.shape[0] == 1, "single-layer block with D_MODEL channels"
    assert seq % PROJ_TM == 0 and seq // CMP_STRIDE == LANE and seq % EXP_TL == 0
    x2d = x.reshape(t, d)

    mod = _ada(c, w_ada[0], b_ada[0]).reshape(bsz, 6, d)
    mod_in = mod[:, 0:2]
    mod_tail = mod[:, 2:5]
    mod_out = mod[:, 5:6]

    bkt_c, bkt_t, bkt_w = _static_buckets(seq)
    bias_c = _bias_table(rel_table, bkt_c)
    bias_t = _bias_table(rel_table, bkt_t).reshape(N_HEADS, 3, TQ, TQ).transpose(1, 0, 2, 3)
    bias_w = _bias_table(rel_table, bkt_w)

    w_qkv, w_rest = _split_w_in(w_in[0])
    qkv = _project(x2d, mod_in, g_pre_mix, w_qkv, seq, heads_out=True)
    rest = _project(x2d, mod_in, g_pre_mix, w_rest, seq, heads_out=False)

    n_chunk = seq // CMP_STRIDE
    kch = qkv[:, QKV_KC0:QKV_KC0 + N_GROUPS].reshape(bsz, N_GROUPS, n_chunk, CMP_STRIDE * LANE)
    vch = qkv[:, QKV_VC0:QKV_VC0 + N_GROUPS].reshape(bsz, N_GROUPS, n_chunk, CMP_STRIDE * LANE)
    w1k, w2k, pk = _cmp_weights(cmp_w1k[0], cmp_w2k[0], cmp_pos_k[0], D_QK)
    w1v, w2v, pv = _cmp_weights(cmp_w1v[0], cmp_w2v[0], cmp_pos_v[0], D_V)
    kc, vc = _compress(kch, vch, pk, pv, w1k, w2k, w1v, w2v)

    pad = ((0, 0), (0, 0), (WINDOW, 0), (0, 0))
    kwp = jnp.pad(qkv[:, QKV_KW0:QKV_KW0 + N_GROUPS], pad)
    vwp = jnp.pad(qkv[:, QKV_VW0:QKV_VW0 + N_GROUPS], pad)

    n_idx = np.arange(LANE)[:, None] * CMP_STRIDE
    j_idx = np.arange(LANE)[None, :] * SLC_BLOCK
    overlap = ((n_idx <= j_idx + SLC_BLOCK - 1) & (n_idx + CMP_BLOCK - 1 >= j_idx)
               & (np.arange(LANE)[None, :] < seq // SLC_BLOCK)
               & (np.arange(LANE)[:, None] < n_chunk - 1)).astype(np.float32)
    expand = (np.arange(LANE)[:, None] == np.arange(seq)[None, :] // SLC_BLOCK)
    attn = _attention(qkv, kwp, vwp, kc, vc, rest, bias_c, bias_t, bias_w,
                      jnp.asarray(overlap), jnp.asarray(expand, dtype=BF16))

    cvec = jnp.stack([conv_b[0], conv_ln_g[0], conv_ln_b[0]])
    gvec = jnp.stack([g_post_mix[0], g_pre_ffn[0]])
    x1, h2 = _mixer_tail(attn.reshape(t, N_HEADS * D_V), rest, x2d, mod_tail,
                         w_attn_out[0].astype(BF16), w_conv_out[0].astype(BF16),
                         w_out[0].astype(BF16), conv_w[0, :, 0, :], cvec, gvec, seq)

    wqt = peer_wq[0].T.astype(BF16)
    sk = peer_subkeys[0].reshape(2 * PEER_HEADS, PEER_NKEYS, PEER_DQ // 2).astype(BF16)
    f, cnt, g, rb = _route(h2, wqt, sk)
    out = _experts(h2, peer_u[0].astype(BF16), peer_v[0].T.astype(BF16), f, cnt, g, rb,
                   x1, mod_out, g_post_ffn, seq)
    return out.reshape(bsz, seq, d)
```

```python
import functools
import math

import numpy as np
import jax
import jax.numpy as jnp
from jax import lax
from jax.experimental import pallas as pl
from jax.experimental.pallas import tpu as pltpu

F32 = jnp.float32
BF16 = jnp.bfloat16

D_MODEL = 1024
N_HEADS = 16
N_GROUPS = 4
HEADS_PER_GROUP = N_HEADS // N_GROUPS
D_QK = 96
D_V = 64
CMP_BLOCK = 32
CMP_STRIDE = 16
CMP_HIDDEN = 256
SLC_BLOCK = 64
SLC_TOPN = 16
WINDOW = 512
ATTN_SCALE = D_QK ** -0.5
FORCE_BONUS = 1e6
N_BUCKETS = 32
MAX_DISTANCE = 128
CONV_WIDTH = 31
PEER_HEADS = 8
PEER_NKEYS = 128
PEER_DQ = 256
PEER_TOPK = 16
NORM_EPS = 1e-6
NEG_INF = -1e30
LOG2E = math.log2(math.e)

LANE = 128
SUBLANE = 8
BF16_ROWS = 16
VMEM_LIMIT = 56 * 1024 * 1024

TQ = 256
SLC_CHUNK = 256
WIN_SPAN = TQ + WINDOW
PROJ_TM = 2048
PROJ_TN = 512
TAIL_TM = 256
HALO = 32
ROUTE_TL = 256
EXP_TL = 512
EXP_TE = 1024

QKV_Q0, QKV_KC0, QKV_KS0, QKV_KW0, QKV_VC0, QKV_VS0, QKV_VW0 = 0, 16, 20, 24, 28, 32, 36
QKV_HEADS = 40
REST_GLU_A, REST_GLU_G, REST_MRG_A, REST_MRG_B = 0, 1, 2, 3
REST_GATE0 = 4 * D_MODEL
REST_COLS = 4 * D_MODEL + N_GROUPS * LANE

NT_DIMS = (((1,), (1,)), ((), ()))


def _cparams(*sem):
    return pltpu.CompilerParams(dimension_semantics=sem, vmem_limit_bytes=VMEM_LIMIT)


def _gelu(x):
    return 0.5 * x * (1.0 + jnp.tanh(math.sqrt(2.0 / math.pi) * (x + 0.044715 * (x * x * x))))


def _gelu_x2(x):
    k1 = math.sqrt(2.0 / math.pi)
    return x * (1.0 + jnp.tanh(x * (k1 + (k1 * 0.044715) * (x * x))))


def _sigmoid(x):
    return 1.0 / (1.0 + jnp.exp(-x))


def _rms(x, g):
    return x * lax.rsqrt(jnp.mean(x * x, axis=-1, keepdims=True) + NORM_EPS) * g


def _ada_kernel(c_ref, w_ref, b_ref, o_ref):
    c = c_ref[...]
    c_act = c * _sigmoid(c)
    o_ref[...] = jnp.dot(c_act, w_ref[...], preferred_element_type=F32,
                         precision=lax.Precision.HIGHEST) + b_ref[...]


def _ada(c, w, b):
    bsz, d = c.shape
    n = w.shape[1]
    tn = 1024
    return pl.pallas_call(
        _ada_kernel,
        grid=(n // tn,),
        in_specs=[pl.BlockSpec((bsz, d), lambda j: (0, 0)),
                  pl.BlockSpec((d, tn), lambda j: (0, j)),
                  pl.BlockSpec((1, tn), lambda j: (0, j))],
        out_specs=pl.BlockSpec((bsz, tn), lambda j: (0, j)),
        out_shape=jax.ShapeDtypeStruct((bsz, n), F32),
        compiler_params=_cparams("parallel"),
        name="ada",
    )(c, w, b.reshape(1, n))


def _t5_bucket_np(dist):
    n = np.maximum(dist, 0)
    max_exact = N_BUCKETS // 2
    nf = np.maximum(n, 1).astype(np.float64)
    large = max_exact + (np.log(nf / max_exact) / math.log(MAX_DISTANCE / max_exact)
                         * (N_BUCKETS - max_exact)).astype(np.int64)
    return np.where(n < max_exact, n, np.minimum(large, N_BUCKETS - 1)).astype(np.int32)


def _bias_kernel(tab_ref, bkt_ref, o_ref):
    h = pl.program_id(0)
    bkt = bkt_ref[...]
    acc = jnp.where(bkt < 0, NEG_INF, 0.0).astype(F32)
    for i in range(N_BUCKETS):
        acc = jnp.where(bkt == i, tab_ref[i, h] * LOG2E, acc)
    o_ref[0] = acc


def _bias_table(rel_table, bucket):
    rows, cols = bucket.shape
    return pl.pallas_call(
        _bias_kernel,
        grid=(N_HEADS,),
        in_specs=[pl.BlockSpec(memory_space=pltpu.SMEM),
                  pl.BlockSpec((rows, cols), lambda h: (0, 0))],
        out_specs=pl.BlockSpec((1, rows, cols), lambda h: (h, 0, 0)),
        out_shape=jax.ShapeDtypeStruct((N_HEADS, rows, cols), F32),
        compiler_params=_cparams("parallel"),
        name="bias_table",
    )(rel_table, jnp.asarray(bucket))


def _static_buckets(seq):
    t = np.arange(seq)[:, None]
    n = np.arange(LANE)[None, :]
    dist_c = t - (n * CMP_STRIDE + CMP_BLOCK - 1)
    bkt_c = np.where(dist_c >= 0, _t5_bucket_np(dist_c), -1).astype(np.int32)
    qi = np.arange(TQ)[:, None]
    kj = np.arange(TQ)[None, :]
    tiles = [np.where(d * TQ + qi - kj >= 0, _t5_bucket_np(d * TQ + qi - kj), -1) for d in range(3)]
    bkt_t = np.concatenate(tiles, axis=0).astype(np.int32)
    kw = np.arange(WIN_SPAN)[None, :]
    dist_w = WINDOW + qi - kw
    band = (dist_w >= 0) & (dist_w < WINDOW)
    bkt_w = np.where(band, _t5_bucket_np(dist_w), -1).astype(np.int32)
    return bkt_c, bkt_t, bkt_w


def _proj_kernel(x_ref, mod_ref, g_ref, w_ref, add_ref, o_ref, h_ref, *, heads_out):
    @pl.when(pl.program_id(1) == 0)
    def _():
        x = x_ref[...]
        sh = mod_ref[0, 0:1, :]
        sc = mod_ref[0, 1:2, :]
        h_ref[...] = (_rms(x, g_ref[...]) * (1.0 + sc) + sh).astype(BF16)

    res = (jnp.dot(h_ref[...], w_ref[...], preferred_element_type=F32) + add_ref[...]).astype(o_ref.dtype)
    if heads_out:
        for k in range(PROJ_TN // LANE):
            o_ref[0, k] = res[:, k * LANE:(k + 1) * LANE]
    else:
        o_ref[...] = res


def _project(x2d, mod3, g, w, add, seq, heads_out):
    t, d = x2d.shape
    n = w.shape[1]
    bsz = t // seq
    tiles_per_seq = seq // PROJ_TM
    grid = (t // PROJ_TM, n // PROJ_TN)
    if heads_out:
        hpt = PROJ_TN // LANE
        out_shape = jax.ShapeDtypeStruct((bsz, n // LANE, seq, LANE), BF16)
        out_spec = pl.BlockSpec((1, hpt, PROJ_TM, LANE),
                                lambda i, j: (i // tiles_per_seq, j, i % tiles_per_seq, 0))
    else:
        out_shape = jax.ShapeDtypeStruct((t, n), BF16)
        out_spec = pl.BlockSpec((PROJ_TM, PROJ_TN), lambda i, j: (i, j))
    return pl.pallas_call(
        functools.partial(_proj_kernel, heads_out=heads_out),
        grid=grid,
        in_specs=[pl.BlockSpec((PROJ_TM, d), lambda i, j: (i, 0)),
                  pl.BlockSpec((1, 2, d), lambda i, j: (i // tiles_per_seq, 0, 0)),
                  pl.BlockSpec((1, d), lambda i, j: (0, 0)),
                  pl.BlockSpec((d, PROJ_TN), lambda i, j: (0, j)),
                  pl.BlockSpec((1, PROJ_TN), lambda i, j: (0, j))],
        out_specs=out_spec,
        out_shape=out_shape,
        scratch_shapes=[pltpu.VMEM((PROJ_TM, d), BF16)],
        compiler_params=_cparams("parallel", "arbitrary"),
        name="proj_heads" if heads_out else "proj_rest",
    )(x2d, mod3, g, w, add)


def _compress_kernel(kch_ref, vch_ref, pk_ref, pv_ref, w1k_ref, w2k_ref, w1v_ref, w2v_ref,
                     kc_ref, vc_ref):
    def one(ch_ref, pos_ref, w1_ref, w2_ref, o_ref):
        a = ch_ref[0, 0].astype(F32)
        a1 = (a + pos_ref[0:1, :]).astype(BF16)
        a2 = (a + pos_ref[1:2, :]).astype(BF16)
        p1 = jnp.dot(a1, w1_ref[0], preferred_element_type=F32)
        p2 = jnp.dot(a2, w1_ref[1], preferred_element_type=F32)
        n_chunk = p2.shape[0]
        hid = _gelu(p1 + pltpu.roll(p2, n_chunk - 1, 0))
        o_ref[0, 0] = jnp.dot(hid.astype(BF16), w2_ref[...],
                              preferred_element_type=F32).astype(BF16)

    one(kch_ref, pk_ref, w1k_ref, w2k_ref, kc_ref)
    one(vch_ref, pv_ref, w1v_ref, w2v_ref, vc_ref)


def _compress(kch, vch, pk, pv, w1k, w2k, w1v, w2v):
    bsz, g, n_chunk, width = kch.shape
    assert n_chunk == LANE, "compressed keys are laid out on one 128-row tile"
    spec_in = pl.BlockSpec((1, 1, n_chunk, width), lambda b, gg: (b, gg, 0, 0))
    spec_out = pl.BlockSpec((1, 1, n_chunk, LANE), lambda b, gg: (b, gg, 0, 0))
    full = lambda a: pl.BlockSpec(a.shape, lambda b, gg: (0,) * a.ndim)
    out = jax.ShapeDtypeStruct((bsz, g, n_chunk, LANE), BF16)
    return pl.pallas_call(
        _compress_kernel,
        grid=(bsz, g),
        in_specs=[spec_in, spec_in, full(pk), full(pv), full(w1k), full(w2k), full(w1v), full(w2v)],
        out_specs=[spec_out, spec_out],
        out_shape=[out, out],
        compiler_params=_cparams("parallel", "parallel"),
        name="compress",
    )(kch, vch, pk, pv, w1k, w2k, w1v, w2v)


def _attn_kernel(q_ref, kc_ref, vc_ref, ks_ref, vs_ref, kw_ref, vw_ref, gate_ref,
                 bias_c_ref, bias_t_ref, bias_w_ref, overlap_ref, expand_ref,
                 o_ref, mask_ref):
    qi = pl.program_id(2)
    rows = HEADS_PER_GROUP * TQ
    seq = mask_ref.shape[1]
    q4 = q_ref[0].reshape(rows, LANE)

    def per_head(x):
        return jnp.broadcast_to(x[None], (HEADS_PER_GROUP,) + x.shape).reshape(rows, x.shape[-1])

    lc = lax.dot_general(q4, kc_ref[0, 0], NT_DIMS, preferred_element_type=F32)
    lc = lc + bias_c_ref[...].reshape(rows, LANE)
    valid = lc > 0.5 * NEG_INF
    mc = jnp.max(lc, axis=-1, keepdims=True)
    ec = jnp.where(valid, jnp.exp2(lc - mc), 0.0)
    sc = jnp.sum(ec, axis=-1, keepdims=True)
    pc = ec / jnp.where(sc > 0.0, sc, 1.0)
    o_cmp = jnp.dot(pc.astype(BF16), vc_ref[0, 0], preferred_element_type=F32)

    psum = jnp.sum(pc.reshape(HEADS_PER_GROUP, TQ, LANE), axis=0)
    imp = jnp.dot(psum, overlap_ref[...], preferred_element_type=F32,
                  precision=lax.Precision.HIGHEST)
    t_pos = qi * TQ + lax.broadcasted_iota(jnp.int32, (TQ, LANE), 0)
    j_blk = lax.broadcasted_iota(jnp.int32, (TQ, LANE), 1)
    cur = t_pos // SLC_BLOCK
    forced = (j_blk == 0) | (j_blk == cur) | (j_blk == cur - 1)
    score = jnp.where(j_blk <= cur, imp + jnp.where(forced, FORCE_BONUS, 0.0), NEG_INF)
    n_slc = seq // SLC_BLOCK
    sc_t = score.T[:n_slc]
    j_row = lax.broadcasted_iota(jnp.int32, (n_slc, TQ), 0)
    rank = jnp.zeros((n_slc, TQ), F32)
    for jp in range(n_slc):
        row = sc_t[jp:jp + 1, :]
        ge = jnp.where(row >= sc_t, 1.0, 0.0)
        gt = jnp.where(row > sc_t, 1.0, 0.0)
        rank = rank + jnp.where(j_row > jp, ge, gt)
    sel_t = jnp.where(rank < float(min(SLC_TOPN, n_slc)), 1.0, 0.0)
    sel_t = jnp.concatenate([sel_t, jnp.zeros((LANE - n_slc, TQ), F32)], axis=0)
    sel = ((sel_t.T - 1.0) * (-NEG_INF)).astype(BF16)
    mask_ref[...] = jnp.dot(sel, expand_ref[...], preferred_element_type=F32)

    tiles_per_chunk = SLC_CHUNK // TQ

    def slc_step(c, carry):
        m, acc = carry
        k0 = pl.multiple_of(c * SLC_CHUNK, SLC_CHUNK)
        kt = ks_ref[0, 0, pl.ds(k0, SLC_CHUNK), :]
        vt = vs_ref[0, 0, pl.ds(k0, SLC_CHUNK), :]
        s = lax.dot_general(q4, kt, NT_DIMS, preferred_element_type=F32)
        bias = jnp.concatenate(
            [bias_t_ref[jnp.clip(qi - (c * tiles_per_chunk + k), 0, 2)]
             for k in range(tiles_per_chunk)], axis=-1)
        s = s + bias.reshape(rows, SLC_CHUNK) + per_head(mask_ref[:, pl.ds(k0, SLC_CHUNK)])
        m_new = jnp.maximum(m, jnp.max(s, axis=-1, keepdims=True))
        p = jnp.exp2(s - m_new)
        acc = jnp.exp2(m - m_new) * acc + jnp.dot(p.astype(BF16), vt, preferred_element_type=F32)
        return m_new, acc

    n_chunks = (qi + tiles_per_chunk) // tiles_per_chunk
    init = (jnp.full((rows, 1), -jnp.inf, F32), jnp.zeros((rows, LANE), F32))
    _, acc_s = lax.fori_loop(0, n_chunks, slc_step, init)
    o_slc = acc_s / pltpu.roll(acc_s, LANE - D_V, 1)

    w0 = pl.multiple_of(qi * TQ, TQ)
    kwin = kw_ref[0, 0, pl.ds(w0, WIN_SPAN), :]
    vwin = vw_ref[0, 0, pl.ds(w0, WIN_SPAN), :]
    sw = lax.dot_general(q4, kwin, NT_DIMS, preferred_element_type=F32)
    sw = sw + bias_w_ref[...].reshape(rows, WIN_SPAN)
    kj = lax.broadcasted_iota(jnp.int32, (rows, WIN_SPAN), 1)
    sw = jnp.where(kj >= WINDOW - qi * TQ, sw, NEG_INF)
    mw = jnp.max(sw, axis=-1, keepdims=True)
    pw = jnp.exp2(sw - mw)
    o_win = jnp.dot(pw.astype(BF16), vwin, preferred_element_type=F32)
    o_win = o_win / pltpu.roll(o_win, LANE - D_V, 1)

    gates = _sigmoid(gate_ref[...].astype(F32))
    outs = []
    for r in range(HEADS_PER_GROUP):
        sl = slice(r * TQ, (r + 1) * TQ)
        g_c = gates[:, r:r + 1]
        g_s = gates[:, HEADS_PER_GROUP + r:HEADS_PER_GROUP + r + 1]
        g_w = gates[:, 2 * HEADS_PER_GROUP + r:2 * HEADS_PER_GROUP + r + 1]
        o = g_c * o_cmp[sl] + g_s * o_slc[sl] + g_w * o_win[sl]
        outs.append(o[:, :D_V])
    o_ref[0] = jnp.concatenate(outs, axis=-1).astype(o_ref.dtype)


def _attention(qkv, kwp, vwp, kc, vc, rest, bias_c, bias_t, bias_w, overlap, expand):
    bsz, _, seq, _ = qkv.shape
    nq = seq // TQ
    g_cols0 = REST_GATE0 // LANE
    head = lambda base: pl.BlockSpec((1, 1, seq, LANE), lambda b, g, i: (b, base + g, 0, 0))
    in_specs = [
        pl.BlockSpec((1, HEADS_PER_GROUP, TQ, LANE), lambda b, g, i: (b, g, i, 0)),
        pl.BlockSpec((1, 1, LANE, LANE), lambda b, g, i: (b, g, 0, 0)),
        pl.BlockSpec((1, 1, LANE, LANE), lambda b, g, i: (b, g, 0, 0)),
        head(QKV_KS0), head(QKV_VS0),
        pl.BlockSpec((1, 1, seq + WINDOW, LANE), lambda b, g, i: (b, g, 0, 0)),
        pl.BlockSpec((1, 1, seq + WINDOW, LANE), lambda b, g, i: (b, g, 0, 0)),
        pl.BlockSpec((TQ, LANE), lambda b, g, i: (b * nq + i, g_cols0 + g)),
        pl.BlockSpec((HEADS_PER_GROUP, TQ, LANE), lambda b, g, i: (g, i, 0)),
        pl.BlockSpec((3, HEADS_PER_GROUP, TQ, TQ), lambda b, g, i: (0, g, 0, 0)),
        pl.BlockSpec((HEADS_PER_GROUP, TQ, WIN_SPAN), lambda b, g, i: (g, 0, 0)),
        pl.BlockSpec((LANE, LANE), lambda b, g, i: (0, 0)),
        pl.BlockSpec((LANE, seq), lambda b, g, i: (0, 0)),
    ]
    return pl.pallas_call(
        _attn_kernel,
        grid=(bsz, N_GROUPS, nq),
        in_specs=in_specs,
        out_specs=pl.BlockSpec((1, TQ, HEADS_PER_GROUP * D_V), lambda b, g, i: (b, i, g)),
        out_shape=jax.ShapeDtypeStruct((bsz, seq, N_HEADS * D_V), BF16),
        scratch_shapes=[pltpu.VMEM((TQ, seq), F32)],
        compiler_params=_cparams("parallel", "parallel", "arbitrary"),
        name="attention",
    )(qkv, kc, vc, qkv, qkv, kwp, vwp, rest, bias_c, bias_t, bias_w, overlap, expand)


def _tail_kernel(attn_ref, ga_ref, gg_ref, ha_ref, hg_ref, ma_ref, mb_ref, x_ref, mod_ref,
                 wa_ref, wc_ref, wo_ref, cw_ref, cvec_ref, gvec_ref,
                 x1_ref, h2_ref, u_ref, ush_ref, *, tiles_per_seq):
    first = (pl.program_id(0) % tiles_per_seq) == 0
    y_a = jnp.dot(attn_ref[...], wa_ref[...], preferred_element_type=F32)

    halo = ha_ref[...].astype(F32) * _sigmoid(hg_ref[...].astype(F32))
    u_ref[0:HALO, :] = jnp.where(first, 0.0, halo)
    u_ref[HALO:, :] = ga_ref[...].astype(F32) * _sigmoid(gg_ref[...].astype(F32))

    conv_b, ln_g, ln_b = cvec_ref[0:1, :], cvec_ref[1:2, :], cvec_ref[2:3, :]
    n_sh = HALO + TAIL_TM - SUBLANE
    for r in range(1, SUBLANE):
        ush_ref[r - 1, 0:n_sh, :] = u_ref[r:r + n_sh, :]
    blk = 32
    pieces = []
    for rb in range(TAIL_TM // blk):
        acc = jnp.zeros((blk, D_MODEL), F32)
        for j in range(CONV_WIDTH):
            start = HALO - (CONV_WIDTH - 1) + rb * blk + j
            r, a = start % SUBLANE, start - start % SUBLANE
            win = u_ref[a:a + blk, :] if r == 0 else ush_ref[r - 1, a:a + blk, :]
            acc = acc + cw_ref[j:j + 1, :] * win
        pieces.append(acc)
    y = jnp.concatenate(pieces, axis=0) + conv_b
    yc = y - jnp.mean(y, axis=-1, keepdims=True)
    yn = yc * lax.rsqrt(jnp.mean(yc * yc, axis=-1, keepdims=True) + NORM_EPS) * ln_g + ln_b
    act = yn * _sigmoid(yn)
    y_b = jnp.dot(act.astype(BF16), wc_ref[...], preferred_element_type=F32)

    merged = (_sigmoid(ma_ref[...].astype(F32)) * y_a + _sigmoid(mb_ref[...].astype(F32)) * y_b)
    out = jnp.dot(merged.astype(BF16), wo_ref[...], preferred_element_type=F32)

    gt1, sh2, sc2 = mod_ref[0, 0:1, :], mod_ref[0, 1:2, :], mod_ref[0, 2:3, :]
    x1 = x_ref[...] + gt1 * _rms(out, gvec_ref[0:1, :])
    x1_ref[...] = x1
    h2_ref[...] = (_rms(x1, gvec_ref[1:2, :]) * (1.0 + sc2) + sh2).astype(BF16)


def _mixer_tail(attn2d, rest, x2d, mod3, wa, wc, wo, cw, cvec, gvec, seq):
    t, d = x2d.shape
    tiles_per_seq = seq // TAIL_TM
    hpt = TAIL_TM // HALO
    row = lambda cb: pl.BlockSpec((TAIL_TM, d), lambda i: (i, cb))
    halo = lambda cb: pl.BlockSpec((HALO, d), lambda i: (jnp.maximum(i * hpt - 1, 0), cb))
    full = lambda a: pl.BlockSpec(a.shape, lambda i: (0,) * a.ndim)
    return pl.pallas_call(
        functools.partial(_tail_kernel, tiles_per_seq=tiles_per_seq),
        grid=(t // TAIL_TM,),
        in_specs=[row(0), row(REST_GLU_A), row(REST_GLU_G), halo(REST_GLU_A), halo(REST_GLU_G),
                  row(REST_MRG_A), row(REST_MRG_B), row(0),
                  pl.BlockSpec((1, 3, d), lambda i: (i // tiles_per_seq, 0, 0)),
                  full(wa), full(wc), full(wo), full(cw), full(cvec), full(gvec)],
        out_specs=[row(0), row(0)],
        out_shape=[jax.ShapeDtypeStruct((t, d), F32), jax.ShapeDtypeStruct((t, d), BF16)],
        scratch_shapes=[pltpu.VMEM((HALO + TAIL_TM, d), F32),
                        pltpu.VMEM((SUBLANE - 1, HALO + TAIL_TM, d), F32)],
        compiler_params=_cparams("parallel"),
        name="mixer_tail",
    )(attn2d, rest, rest, rest, rest, rest, rest, x2d, mod3, wa, wc, wo, cw, cvec, gvec)


def _cand_pairs():
    return [(i, j) for i in range(PEER_TOPK) for j in range(PEER_TOPK)
            if (i + 1) * (j + 1) <= PEER_TOPK]


def _top16_rows(s):
    n = s.shape[0]
    kio = lax.broadcasted_iota(jnp.int32, s.shape, 0).astype(F32)
    vals, idxs = [], []
    for _ in range(PEER_TOPK):
        m = jnp.max(s, axis=0, keepdims=True)
        idx = jnp.min(jnp.where(s == m, kio, float(n)), axis=0, keepdims=True)
        vals.append(m)
        idxs.append(idx)
        s = jnp.where(kio == idx, -jnp.inf, s)
    return vals, idxs


def _route_kernel(h2_ref, wqt_ref, sk_ref, f_ref, cnt_ref, g_ref, rb_ref):
    q2t = lax.dot_general(wqt_ref[...], h2_ref[...], NT_DIMS, preferred_element_type=F32)
    tl = q2t.shape[1]
    kio = lax.broadcasted_iota(jnp.int32, (PEER_NKEYS, tl), 0).astype(F32)
    pairs = _cand_pairs()
    n_pad = -len(pairs) % 8
    flat_ids = np.array([i * PEER_TOPK + j for i, j in pairs] + [PEER_TOPK ** 2] * n_pad, np.float32)
    group_start = [min(k for k, (i, _) in enumerate(pairs) if i == ii) for ii in range(PEER_TOPK)]
    group_len = [sum(1 for (i, _) in pairs if i == ii) for ii in range(PEER_TOPK)]
    n_rows = len(pairs) + n_pad
    flat_col = lax.broadcasted_iota(jnp.int32, (n_rows, tl), 0)
    flat = jnp.zeros((n_rows, tl), F32)
    for k in range(n_rows):
        flat = jnp.where(flat_col == k, float(flat_ids[k]), flat)

    for h in range(PEER_HEADS):
        scores, tops = [], []
        for c in range(2):
            hc = 2 * h + c
            qt = q2t[hc * LANE:(hc + 1) * LANE, :].astype(BF16)
            s = jnp.dot(sk_ref[hc], qt, preferred_element_type=F32)
            scores.append(s)
            tops.append(_top16_rows(s))
        (v1, i1), (v2, i2) = tops
        cand0 = jnp.concatenate([v1[i] + v2[j] for i, j in pairs]
                                + [jnp.full((n_pad, tl), -jnp.inf, F32)], axis=0)
        cand = cand0
        for _ in range(PEER_TOPK):
            m = jnp.max(cand, axis=0, keepdims=True)
            fid = jnp.min(jnp.where(cand == m, flat, float(PEER_TOPK ** 2 + 1)), axis=0, keepdims=True)
            cand = jnp.where(flat == fid, -jnp.inf, cand)
        picked = jnp.where((cand == -jnp.inf) & (flat < float(PEER_TOPK ** 2)), 1.0, 0.0)
        top = v1[0] + v2[0]
        z = jnp.sum(picked * jnp.exp(cand0 - top), axis=0, keepdims=True)
        counts = [jnp.sum(picked[group_start[i]:group_start[i] + group_len[i]], axis=0, keepdims=True)
                  for i in range(PEER_TOPK)]

        cnt = jnp.zeros((PEER_NKEYS, tl), F32)
        rank_b = jnp.full((PEER_NKEYS, tl), float(PEER_NKEYS), F32)
        for i in range(PEER_TOPK):
            cnt = jnp.where(kio == i1[i], counts[i], cnt)
            rank_b = jnp.where(kio == i2[i], float(i), rank_b)
        outs = ((f_ref, jnp.exp(scores[0] - v1[0]) * (0.5 / z)),
                (g_ref, jnp.exp(scores[1] - v2[0])), (cnt_ref, cnt), (rb_ref, rank_b))
        for ref, val in outs:
            for ch in range(tl // LANE):
                ref[h, ch] = val[:, ch * LANE:(ch + 1) * LANE].astype(ref.dtype)


def _route(h2, wqt, sk):
    t, d = h2.shape
    spec = pl.BlockSpec((PEER_HEADS, ROUTE_TL // LANE, PEER_NKEYS, LANE), lambda i: (0, i, 0, 0))
    out = jax.ShapeDtypeStruct((PEER_HEADS, t // LANE, PEER_NKEYS, LANE), F32)
    out16 = jax.ShapeDtypeStruct((PEER_HEADS, t // LANE, PEER_NKEYS, LANE), BF16)
    return pl.pallas_call(
        _route_kernel,
        grid=(t // ROUTE_TL,),
        in_specs=[pl.BlockSpec((ROUTE_TL, d), lambda i: (i, 0)),
                  pl.BlockSpec(wqt.shape, lambda i: (0, 0)),
                  pl.BlockSpec(sk.shape, lambda i: (0, 0, 0))],
        out_specs=[spec, spec, spec, spec],
        out_shape=[out, out, out16, out16],
        compiler_params=_cparams("parallel"),
        name="peer_route",
    )(h2, wqt, sk)


def _expert_kernel(h2_ref, u_ref, vt_ref, f_ref, cnt_ref, g_ref, rb_ref, x1_ref, mod_ref, gp_ref,
                   o_ref, acc_ref, act_ref, coef_ref):
    e = pl.program_id(1)
    n_chunks = EXP_TL // LANE
    a_per_step = EXP_TE // PEER_NKEYS
    a_group = 2

    @pl.when(e == 0)
    def _():
        acc_ref[...] = jnp.zeros_like(acc_ref)

    act = lax.dot_general(u_ref[...], h2_ref[...], NT_DIMS, preferred_element_type=F32)
    for c in range(n_chunks):
        act_ref[c] = act[:, c * LANE:(c + 1) * LANE]

    def lane_chunk(c, carry):
        for ag in range(a_per_step // a_group):
            coefs = [jnp.zeros((PEER_NKEYS, LANE), BF16) for _ in range(a_group)]

            def row_tile(ref, h, a):
                row = jnp.broadcast_to(ref[h, c, pl.ds(a, 1), :], (BF16_ROWS, LANE)).astype(BF16)
                return jnp.broadcast_to(row[None], (PEER_NKEYS // BF16_ROWS, BF16_ROWS, LANE)
                                        ).reshape(PEER_NKEYS, LANE)

            for h in range(PEER_HEADS):
                rank_b = rb_ref[h, c]
                g_b = g_ref[h, c]
                for k in range(a_group):
                    a = e * a_per_step + ag * a_group + k
                    picked = jnp.maximum(jnp.minimum(row_tile(cnt_ref, h, a) - rank_b, g_b), 0.0)
                    coefs[k] = coefs[k] + row_tile(f_ref, h, a) * picked
            for k in range(a_group):
                rows = pl.ds((ag * a_group + k) * PEER_NKEYS, PEER_NKEYS)
                coef_ref[c, rows, :] = _gelu_x2(act_ref[c, rows, :]).astype(BF16) * coefs[k]
        return carry

    lax.fori_loop(0, n_chunks, lane_chunk, 0)
    coef = jnp.concatenate([coef_ref[c] for c in range(n_chunks)], axis=1)
    acc_ref[...] += jnp.dot(vt_ref[...], coef, preferred_element_type=F32)

    @pl.when(e == pl.num_programs(1) - 1)
    def _():
        y = acc_ref[...].T
        o_ref[...] = x1_ref[...] + mod_ref[0] * _rms(y, gp_ref[...])


def _experts(h2, u, vt, f, cnt, g, rb, x1, gt2, gpost, seq):
    t, d = h2.shape
    n_exp = u.shape[0]
    tiles_per_seq = seq // EXP_TL
    n_chunks = EXP_TL // LANE
    route = pl.BlockSpec((PEER_HEADS, n_chunks, PEER_NKEYS, LANE), lambda i, e: (0, i, 0, 0))
    return pl.pallas_call(
        _expert_kernel,
        grid=(t // EXP_TL, n_exp // EXP_TE),
        in_specs=[pl.BlockSpec((EXP_TL, d), lambda i, e: (i, 0)),
                  pl.BlockSpec((EXP_TE, d), lambda i, e: (e, 0)),
                  pl.BlockSpec((d, EXP_TE), lambda i, e: (0, e)),
                  route, route, route, route,
                  pl.BlockSpec((EXP_TL, d), lambda i, e: (i, 0)),
                  pl.BlockSpec((1, 1, d), lambda i, e: (i // tiles_per_seq, 0, 0)),
                  pl.BlockSpec((1, d), lambda i, e: (0, 0))],
        out_specs=pl.BlockSpec((EXP_TL, d), lambda i, e: (i, 0)),
        out_shape=jax.ShapeDtypeStruct((t, d), F32),
        scratch_shapes=[pltpu.VMEM((d, EXP_TL), F32), pltpu.VMEM((n_chunks, EXP_TE, LANE), F32),
                        pltpu.VMEM((n_chunks, EXP_TE, LANE), BF16)],
        compiler_params=_cparams("parallel", "arbitrary"),
        name="peer_experts",
    )(h2, u, vt, f, cnt, g, rb, x1, gt2, gpost)


def _pad_heads(w, n, width):
    d = w.shape[0]
    return jnp.pad(w.reshape(d, n, width), ((0, 0), (0, 0), (0, LANE - width))).reshape(d, n * LANE)


def _split_w_in(w_in):
    q_cols = N_HEADS * D_QK
    k_cols = N_GROUPS * D_QK
    v_cols = N_GROUPS * D_V
    sizes = (q_cols, k_cols, k_cols, k_cols, v_cols, v_cols, v_cols, 3 * N_HEADS, 2 * D_MODEL, 2 * D_MODEL)
    offs = np.cumsum((0,) + sizes)
    parts = [w_in[:, offs[k]:offs[k + 1]] for k in range(len(sizes))]
    wq, wkc, wks, wkw, wvc, wvs, wvw, wgate, wglu, wmerge = parts
    w_qkv = jnp.concatenate(
        [_pad_heads(wq * (ATTN_SCALE * LOG2E), N_HEADS, D_QK)]
        + [_pad_heads(w, N_GROUPS, D_QK) for w in (wkc, wks, wkw)]
        + [_pad_heads(w, N_GROUPS, D_V) for w in (wvc, wvs, wvw)], axis=1)
    d = w_in.shape[0]
    wg = wgate.reshape(d, 3, N_GROUPS, HEADS_PER_GROUP).transpose(0, 2, 1, 3)
    wg = wg.reshape(d, N_GROUPS, 3 * HEADS_PER_GROUP)
    wg = jnp.pad(wg, ((0, 0), (0, 0), (0, LANE - 3 * HEADS_PER_GROUP))).reshape(d, N_GROUPS * LANE)
    w_rest = jnp.concatenate([wglu, wmerge, wg], axis=1)
    return w_qkv.astype(BF16), w_rest.astype(BF16)


def _cmp_weights(w1, w2, pos, dh):
    hidden = w1.shape[1]
    w1p = jnp.pad(w1.reshape(CMP_BLOCK, dh, hidden), ((0, 0), (0, LANE - dh), (0, 0)))
    w1p = w1p.reshape(2, CMP_STRIDE * LANE, hidden).astype(BF16)
    w2p = jnp.pad(w2, ((0, 0), (0, LANE - dh))).astype(BF16)
    posp = jnp.pad(pos, ((0, 0), (0, LANE - dh))).reshape(2, CMP_STRIDE * LANE)
    return w1p, w2p, posp


def kernel(x, c, w_ada, b_ada, g_pre_mix, g_post_mix, g_pre_ffn, g_post_ffn, rel_table, w_in,
           cmp_w1k, cmp_w2k, cmp_pos_k, cmp_w1v, cmp_w2v, cmp_pos_v, w_attn_out,
           conv_w, conv_b, conv_ln_g, conv_ln_b, w_conv_out, w_out,
           peer_wq, peer_subkeys, peer_u, peer_v):
    bsz, seq, d = x.shape
    t = bsz * seq
    assert d == D_MODEL and w_ada.shape[0] == 1, "single-layer block with D_MODEL channels"
    assert seq % PROJ_TM == 0 and seq // CMP_STRIDE == LANE and seq % EXP_TL == 0
    x2d = x.reshape(t, d)

    mod = _ada(c, w_ada[0], b_ada[0]).reshape(bsz, 6, d)
    mod_in = mod[:, 0:2]
    mod_tail = mod[:, 2:5]
    mod_out = mod[:, 5:6]

    bkt_c, bkt_t, bkt_w = _static_buckets(seq)
    bias_c = _bias_table(rel_table, bkt_c)
    bias_t = _bias_table(rel_table, bkt_t).reshape(N_HEADS, 3, TQ, TQ).transpose(1, 0, 2, 3)
    bias_w = _bias_table(rel_table, bkt_w)

    w_qkv, w_rest = _split_w_in(w_in[0])
    add_qkv = np.zeros((QKV_HEADS, LANE), np.float32)
    add_qkv[QKV_VS0:QKV_VS0 + N_GROUPS, D_V:] = 1.0
    add_qkv[QKV_VW0:QKV_VW0 + N_GROUPS, D_V:] = 1.0
    qkv = _project(x2d, mod_in, g_pre_mix, w_qkv, jnp.asarray(add_qkv.reshape(1, -1)), seq,
                   heads_out=True)
    rest = _project(x2d, mod_in, g_pre_mix, w_rest, jnp.zeros((1, REST_COLS), F32), seq,
                    heads_out=False)

    n_chunk = seq // CMP_STRIDE
    kch = qkv[:, QKV_KC0:QKV_KC0 + N_GROUPS].reshape(bsz, N_GROUPS, n_chunk, CMP_STRIDE * LANE)
    vch = qkv[:, QKV_VC0:QKV_VC0 + N_GROUPS].reshape(bsz, N_GROUPS, n_chunk, CMP_STRIDE * LANE)
    w1k, w2k, pk = _cmp_weights(cmp_w1k[0], cmp_w2k[0], cmp_pos_k[0], D_QK)
    w1v, w2v, pv = _cmp_weights(cmp_w1v[0], cmp_w2v[0], cmp_pos_v[0], D_V)
    kc, vc = _compress(kch, vch, pk, pv, w1k, w2k, w1v, w2v)

    pad = ((0, 0), (0, 0), (WINDOW, 0), (0, 0))
    kwp = jnp.pad(qkv[:, QKV_KW0:QKV_KW0 + N_GROUPS], pad)
    vwp = jnp.pad(qkv[:, QKV_VW0:QKV_VW0 + N_GROUPS], pad)

    n_idx = np.arange(LANE)[:, None] * CMP_STRIDE
    j_idx = np.arange(LANE)[None, :] * SLC_BLOCK
    overlap = ((n_idx <= j_idx + SLC_BLOCK - 1) & (n_idx + CMP_BLOCK - 1 >= j_idx)
               & (np.arange(LANE)[None, :] < seq // SLC_BLOCK)
               & (np.arange(LANE)[:, None] < n_chunk - 1)).astype(np.float32)
    expand = (np.arange(LANE)[:, None] == np.arange(seq)[None, :] // SLC_BLOCK)
    attn = _attention(qkv, kwp, vwp, kc, vc, rest, bias_c, bias_t, bias_w,
                      jnp.asarray(overlap), jnp.asarray(expand, dtype=BF16))

    cvec = jnp.stack([conv_b[0], conv_ln_g[0], conv_ln_b[0]])
    gvec = jnp.stack([g_post_mix[0], g_pre_ffn[0]])
    x1, h2 = _mixer_tail(attn.reshape(t, N_HEADS * D_V), rest, x2d, mod_tail,
                         w_attn_out[0].astype(BF16), w_conv_out[0].astype(BF16),
                         w_out[0].astype(BF16), conv_w[0, :, 0, :], cvec, gvec, seq)

    wqt = peer_wq[0].T.astype(BF16)
    sk = peer_subkeys[0].reshape(2 * PEER_HEADS, PEER_NKEYS, PEER_DQ // 2).astype(BF16)
    f, cnt, g, rb = _route(h2, wqt, sk)
    out = _experts(h2, peer_u[0].astype(BF16), peer_v[0].T.astype(BF16), f, cnt, g, rb,
                   x1, mod_out, g_post_ffn, seq)
    return out.reshape(bsz, seq, d)
```

```python
import functools
import math

import numpy as np
import jax
import jax.numpy as jnp
from jax import lax
from jax.experimental import pallas as pl
from jax.experimental.pallas import tpu as pltpu

F32 = jnp.float32
BF16 = jnp.bfloat16

D_MODEL = 1024
N_HEADS = 16
N_GROUPS = 4
HEADS_PER_GROUP = N_HEADS // N_GROUPS
D_QK = 96
D_V = 64
CMP_BLOCK = 32
CMP_STRIDE = 16
CMP_HIDDEN = 256
SLC_BLOCK = 64
SLC_TOPN = 16
WINDOW = 512
ATTN_SCALE = D_QK ** -0.5
FORCE_BONUS = 1e6
N_BUCKETS = 32
MAX_DISTANCE = 128
CONV_WIDTH = 31
PEER_HEADS = 8
PEER_NKEYS = 128
PEER_DQ = 256
PEER_TOPK = 16
NORM_EPS = 1e-6
NEG_INF = -1e30
LOG2E = math.log2(math.e)

LANE = 128
SUBLANE = 8
BF16_ROWS = 16
VMEM_LIMIT = 56 * 1024 * 1024

TQ = 256
SLC_CHUNK = 256
WIN_SPAN = TQ + WINDOW
PROJ_TM = 2048
PROJ_TN = 512
TAIL_TM = 256
HALO = 32
ROUTE_TL = 256
EXP_TL = 512
EXP_TE = 2048

QKV_Q0, QKV_KC0, QKV_KS0, QKV_KW0, QKV_VC0, QKV_VS0, QKV_VW0 = 0, 16, 20, 24, 28, 32, 36
QKV_HEADS = 40
REST_GLU_A, REST_GLU_G, REST_MRG_A, REST_MRG_B = 0, 1, 2, 3
REST_GATE0 = 4 * D_MODEL
REST_COLS = 4 * D_MODEL + N_GROUPS * LANE

NT_DIMS = (((1,), (1,)), ((), ()))


def _cparams(*sem):
    return pltpu.CompilerParams(dimension_semantics=sem, vmem_limit_bytes=VMEM_LIMIT)


def _gelu(x):
    return 0.5 * x * (1.0 + jnp.tanh(math.sqrt(2.0 / math.pi) * (x + 0.044715 * (x * x * x))))


def _gelu_x2(x):
    k1 = math.sqrt(2.0 / math.pi)
    return x * (1.0 + jnp.tanh(x * (k1 + (k1 * 0.044715) * (x * x))))


def _sigmoid(x):
    return 1.0 / (1.0 + jnp.exp(-x))


def _rms(x, g):
    return x * lax.rsqrt(jnp.mean(x * x, axis=-1, keepdims=True) + NORM_EPS) * g


def _ada_kernel(c_ref, w_ref, b_ref, o_ref):
    c = c_ref[...]
    c_act = c * _sigmoid(c)
    o_ref[...] = jnp.dot(c_act, w_ref[...], preferred_element_type=F32,
                         precision=lax.Precision.HIGHEST) + b_ref[...]


def _ada(c, w, b):
    bsz, d = c.shape
    n = w.shape[1]
    tn = 1024
    return pl.pallas_call(
        _ada_kernel,
        grid=(n // tn,),
        in_specs=[pl.BlockSpec((bsz, d), lambda j: (0, 0)),
                  pl.BlockSpec((d, tn), lambda j: (0, j)),
                  pl.BlockSpec((1, tn), lambda j: (0, j))],
        out_specs=pl.BlockSpec((bsz, tn), lambda j: (0, j)),
        out_shape=jax.ShapeDtypeStruct((bsz, n), F32),
        compiler_params=_cparams("parallel"),
        name="ada",
    )(c, w, b.reshape(1, n))


def _t5_bucket_np(dist):
    n = np.maximum(dist, 0)
    max_exact = N_BUCKETS // 2
    nf = np.maximum(n, 1).astype(np.float64)
    large = max_exact + (np.log(nf / max_exact) / math.log(MAX_DISTANCE / max_exact)
                         * (N_BUCKETS - max_exact)).astype(np.int64)
    return np.where(n < max_exact, n, np.minimum(large, N_BUCKETS - 1)).astype(np.int32)


def _bias_kernel(tab_ref, bkt_ref, o_ref):
    h = pl.program_id(0)
    bkt = bkt_ref[...]
    acc = jnp.where(bkt < 0, NEG_INF, 0.0).astype(F32)
    for i in range(N_BUCKETS):
        acc = jnp.where(bkt == i, tab_ref[i, h] * LOG2E, acc)
    o_ref[0] = acc


def _bias_table(rel_table, bucket):
    rows, cols = bucket.shape
    return pl.pallas_call(
        _bias_kernel,
        grid=(N_HEADS,),
        in_specs=[pl.BlockSpec(memory_space=pltpu.SMEM),
                  pl.BlockSpec((rows, cols), lambda h: (0, 0))],
        out_specs=pl.BlockSpec((1, rows, cols), lambda h: (h, 0, 0)),
        out_shape=jax.ShapeDtypeStruct((N_HEADS, rows, cols), F32),
        compiler_params=_cparams("parallel"),
        name="bias_table",
    )(rel_table, jnp.asarray(bucket))


def _static_buckets(seq):
    t = np.arange(seq)[:, None]
    n = np.arange(LANE)[None, :]
    dist_c = t - (n * CMP_STRIDE + CMP_BLOCK - 1)
    bkt_c = np.where(dist_c >= 0, _t5_bucket_np(dist_c), -1).astype(np.int32)
    qi = np.arange(TQ)[:, None]
    kj = np.arange(TQ)[None, :]
    tiles = [np.where(d * TQ + qi - kj >= 0, _t5_bucket_np(d * TQ + qi - kj), -1) for d in range(3)]
    bkt_t = np.concatenate(tiles, axis=0).astype(np.int32)
    kw = np.arange(WIN_SPAN)[None, :]
    dist_w = WINDOW + qi - kw
    band = (dist_w >= 0) & (dist_w < WINDOW)
    bkt_w = np.where(band, _t5_bucket_np(dist_w), -1).astype(np.int32)
    return bkt_c, bkt_t, bkt_w


def _proj_kernel(x_ref, mod_ref, g_ref, w_ref, add_ref, o_ref, h_ref, *, heads_out):
    @pl.when(pl.program_id(1) == 0)
    def _():
        x = x_ref[...]
        sh = mod_ref[0, 0:1, :]
        sc = mod_ref[0, 1:2, :]
        h_ref[...] = (_rms(x, g_ref[...]) * (1.0 + sc) + sh).astype(BF16)

    res = (jnp.dot(h_ref[...], w_ref[...], preferred_element_type=F32) + add_ref[...]).astype(o_ref.dtype)
    if heads_out:
        for k in range(PROJ_TN // LANE):
            o_ref[0, k] = res[:, k * LANE:(k + 1) * LANE]
    else:
        o_ref[...] = res


def _project(x2d, mod3, g, w, add, seq, heads_out):
    t, d = x2d.shape
    n = w.shape[1]
    bsz = t // seq
    tiles_per_seq = seq // PROJ_TM
    grid = (t // PROJ_TM, n // PROJ_TN)
    if heads_out:
        hpt = PROJ_TN // LANE
        out_shape = jax.ShapeDtypeStruct((bsz, n // LANE, seq, LANE), BF16)
        out_spec = pl.BlockSpec((1, hpt, PROJ_TM, LANE),
                                lambda i, j: (i // tiles_per_seq, j, i % tiles_per_seq, 0))
    else:
        out_shape = jax.ShapeDtypeStruct((t, n), BF16)
        out_spec = pl.BlockSpec((PROJ_TM, PROJ_TN), lambda i, j: (i, j))
    return pl.pallas_call(
        functools.partial(_proj_kernel, heads_out=heads_out),
        grid=grid,
        in_specs=[pl.BlockSpec((PROJ_TM, d), lambda i, j: (i, 0)),
                  pl.BlockSpec((1, 2, d), lambda i, j: (i // tiles_per_seq, 0, 0)),
                  pl.BlockSpec((1, d), lambda i, j: (0, 0)),
                  pl.BlockSpec((d, PROJ_TN), lambda i, j: (0, j)),
                  pl.BlockSpec((1, PROJ_TN), lambda i, j: (0, j))],
        out_specs=out_spec,
        out_shape=out_shape,
        scratch_shapes=[pltpu.VMEM((PROJ_TM, d), BF16)],
        compiler_params=_cparams("parallel", "arbitrary"),
        name="proj_heads" if heads_out else "proj_rest",
    )(x2d, mod3, g, w, add)


def _compress_kernel(kch_ref, vch_ref, pk_ref, pv_ref, w1k_ref, w2k_ref, w1v_ref, w2v_ref,
                     kc_ref, vc_ref):
    def one(ch_ref, pos_ref, w1_ref, w2_ref, o_ref):
        a = ch_ref[0, 0].astype(F32)
        a1 = (a + pos_ref[0:1, :]).astype(BF16)
        a2 = (a + pos_ref[1:2, :]).astype(BF16)
        p1 = jnp.dot(a1, w1_ref[0], preferred_element_type=F32)
        p2 = jnp.dot(a2, w1_ref[1], preferred_element_type=F32)
        n_chunk = p2.shape[0]
        hid = _gelu(p1 + pltpu.roll(p2, n_chunk - 1, 0))
        o_ref[0, 0] = jnp.dot(hid.astype(BF16), w2_ref[...],
                              preferred_element_type=F32).astype(BF16)

    one(kch_ref, pk_ref, w1k_ref, w2k_ref, kc_ref)
    one(vch_ref, pv_ref, w1v_ref, w2v_ref, vc_ref)


def _compress(kch, vch, pk, pv, w1k, w2k, w1v, w2v):
    bsz, g, n_chunk, width = kch.shape
    assert n_chunk == LANE, "compressed keys are laid out on one 128-row tile"
    spec_in = pl.BlockSpec((1, 1, n_chunk, width), lambda b, gg: (b, gg, 0, 0))
    spec_out = pl.BlockSpec((1, 1, n_chunk, LANE), lambda b, gg: (b, gg, 0, 0))
    full = lambda a: pl.BlockSpec(a.shape, lambda b, gg: (0,) * a.ndim)
    out = jax.ShapeDtypeStruct((bsz, g, n_chunk, LANE), BF16)
    return pl.pallas_call(
        _compress_kernel,
        grid=(bsz, g),
        in_specs=[spec_in, spec_in, full(pk), full(pv), full(w1k), full(w2k), full(w1v), full(w2v)],
        out_specs=[spec_out, spec_out],
        out_shape=[out, out],
        compiler_params=_cparams("parallel", "parallel"),
        name="compress",
    )(kch, vch, pk, pv, w1k, w2k, w1v, w2v)


def _attn_kernel(q_ref, kc_ref, vc_ref, ks_ref, vs_ref, kw_ref, vw_ref, gate_ref,
                 bias_c_ref, bias_t_ref, bias_w_ref, overlap_ref, expand_ref,
                 o_ref, mask_ref):
    qi = pl.program_id(2)
    rows = HEADS_PER_GROUP * TQ
    seq = mask_ref.shape[1]
    q4 = q_ref[0].reshape(rows, LANE)

    def per_head(x):
        return jnp.broadcast_to(x[None], (HEADS_PER_GROUP,) + x.shape).reshape(rows, x.shape[-1])

    lc = lax.dot_general(q4, kc_ref[0, 0], NT_DIMS, preferred_element_type=F32)
    lc = lc + bias_c_ref[...].reshape(rows, LANE)
    valid = lc > 0.5 * NEG_INF
    mc = jnp.max(lc, axis=-1, keepdims=True)
    ec = jnp.where(valid, jnp.exp2(lc - mc), 0.0)
    sc = jnp.sum(ec, axis=-1, keepdims=True)
    pc = ec / jnp.where(sc > 0.0, sc, 1.0)
    o_cmp = jnp.dot(pc.astype(BF16), vc_ref[0, 0], preferred_element_type=F32)

    psum = jnp.sum(pc.reshape(HEADS_PER_GROUP, TQ, LANE), axis=0)
    imp = jnp.dot(psum, overlap_ref[...], preferred_element_type=F32,
                  precision=lax.Precision.HIGHEST)
    t_pos = qi * TQ + lax.broadcasted_iota(jnp.int32, (TQ, LANE), 0)
    j_blk = lax.broadcasted_iota(jnp.int32, (TQ, LANE), 1)
    cur = t_pos // SLC_BLOCK
    forced = (j_blk == 0) | (j_blk == cur) | (j_blk == cur - 1)
    score = jnp.where(j_blk <= cur, imp + jnp.where(forced, FORCE_BONUS, 0.0), NEG_INF)
    n_slc = seq // SLC_BLOCK
    sc_t = score.T[:n_slc]
    j_row = lax.broadcasted_iota(jnp.int32, (n_slc, TQ), 0)
    rank = jnp.zeros((n_slc, TQ), F32)
    for jp in range(n_slc):
        row = sc_t[jp:jp + 1, :]
        ge = jnp.where(row >= sc_t, 1.0, 0.0)
        gt = jnp.where(row > sc_t, 1.0, 0.0)
        rank = rank + jnp.where(j_row > jp, ge, gt)
    sel_t = jnp.where(rank < float(min(SLC_TOPN, n_slc)), 1.0, 0.0)
    sel_t = jnp.concatenate([sel_t, jnp.zeros((LANE - n_slc, TQ), F32)], axis=0)
    sel = ((sel_t.T - 1.0) * (-NEG_INF)).astype(BF16)
    mask_ref[...] = jnp.dot(sel, expand_ref[...], preferred_element_type=F32)

    tiles_per_chunk = SLC_CHUNK // TQ

    def slc_step(c, carry):
        m, acc = carry
        k0 = pl.multiple_of(c * SLC_CHUNK, SLC_CHUNK)
        kt = ks_ref[0, 0, pl.ds(k0, SLC_CHUNK), :]
        vt = vs_ref[0, 0, pl.ds(k0, SLC_CHUNK), :]
        s = lax.dot_general(q4, kt, NT_DIMS, preferred_element_type=F32)
        bias = jnp.concatenate(
            [bias_t_ref[jnp.clip(qi - (c * tiles_per_chunk + k), 0, 2)]
             for k in range(tiles_per_chunk)], axis=-1)
        s = s + bias.reshape(rows, SLC_CHUNK) + per_head(mask_ref[:, pl.ds(k0, SLC_CHUNK)])
        m_new = jnp.maximum(m, jnp.max(s, axis=-1, keepdims=True))
        p = jnp.exp2(s - m_new)
        acc = jnp.exp2(m - m_new) * acc + jnp.dot(p.astype(BF16), vt, preferred_element_type=F32)
        return m_new, acc

    n_chunks = (qi + tiles_per_chunk) // tiles_per_chunk
    init = (jnp.full((rows, 1), -jnp.inf, F32), jnp.zeros((rows, LANE), F32))
    _, acc_s = lax.fori_loop(0, n_chunks, slc_step, init)
    o_slc = acc_s / pltpu.roll(acc_s, LANE - D_V, 1)

    w0 = pl.multiple_of(qi * TQ, TQ)
    kwin = kw_ref[0, 0, pl.ds(w0, WIN_SPAN), :]
    vwin = vw_ref[0, 0, pl.ds(w0, WIN_SPAN), :]
    sw = lax.dot_general(q4, kwin, NT_DIMS, preferred_element_type=F32)
    sw = sw + bias_w_ref[...].reshape(rows, WIN_SPAN)
    kj = lax.broadcasted_iota(jnp.int32, (1, WIN_SPAN), 1)
    sw = sw + jnp.where(kj >= WINDOW - qi * TQ, 0.0, NEG_INF)
    mw = jnp.max(sw, axis=-1, keepdims=True)
    pw = jnp.exp2(sw - mw)
    o_win = jnp.dot(pw.astype(BF16), vwin, preferred_element_type=F32)
    o_win = o_win / pltpu.roll(o_win, LANE - D_V, 1)

    gates = _sigmoid(gate_ref[...].astype(F32))
    outs = []
    for r in range(HEADS_PER_GROUP):
        sl = slice(r * TQ, (r + 1) * TQ)
        g_c = gates[:, r:r + 1]
        g_s = gates[:, HEADS_PER_GROUP + r:HEADS_PER_GROUP + r + 1]
        g_w = gates[:, 2 * HEADS_PER_GROUP + r:2 * HEADS_PER_GROUP + r + 1]
        o = g_c * o_cmp[sl] + g_s * o_slc[sl] + g_w * o_win[sl]
        outs.append(o[:, :D_V])
    o_ref[0] = jnp.concatenate(outs, axis=-1).astype(o_ref.dtype)


def _attention(qkv, kwp, vwp, kc, vc, rest, bias_c, bias_t, bias_w, overlap, expand):
    bsz, _, seq, _ = qkv.shape
    nq = seq // TQ
    g_cols0 = REST_GATE0 // LANE
    head = lambda base: pl.BlockSpec((1, 1, seq, LANE), lambda b, g, i: (b, base + g, 0, 0))
    in_specs = [
        pl.BlockSpec((1, HEADS_PER_GROUP, TQ, LANE), lambda b, g, i: (b, g, i, 0)),
        pl.BlockSpec((1, 1, LANE, LANE), lambda b, g, i: (b, g, 0, 0)),
        pl.BlockSpec((1, 1, LANE, LANE), lambda b, g, i: (b, g, 0, 0)),
        head(QKV_KS0), head(QKV_VS0),
        pl.BlockSpec((1, 1, seq + WINDOW, LANE), lambda b, g, i: (b, g, 0, 0)),
        pl.BlockSpec((1, 1, seq + WINDOW, LANE), lambda b, g, i: (b, g, 0, 0)),
        pl.BlockSpec((TQ, LANE), lambda b, g, i: (b * nq + i, g_cols0 + g)),
        pl.BlockSpec((HEADS_PER_GROUP, TQ, LANE), lambda b, g, i: (g, i, 0)),
        pl.BlockSpec((3, HEADS_PER_GROUP, TQ, TQ), lambda b, g, i: (0, g, 0, 0)),
        pl.BlockSpec((HEADS_PER_GROUP, TQ, WIN_SPAN), lambda b, g, i: (g, 0, 0)),
        pl.BlockSpec((LANE, LANE), lambda b, g, i: (0, 0)),
        pl.BlockSpec((LANE, seq), lambda b, g, i: (0, 0)),
    ]
    return pl.pallas_call(
        _attn_kernel,
        grid=(bsz, N_GROUPS, nq),
        in_specs=in_specs,
        out_specs=pl.BlockSpec((1, TQ, HEADS_PER_GROUP * D_V), lambda b, g, i: (b, i, g)),
        out_shape=jax.ShapeDtypeStruct((bsz, seq, N_HEADS * D_V), BF16),
        scratch_shapes=[pltpu.VMEM((TQ, seq), F32)],
        compiler_params=_cparams("parallel", "parallel", "arbitrary"),
        name="attention",
    )(qkv, kc, vc, qkv, qkv, kwp, vwp, rest, bias_c, bias_t, bias_w, overlap, expand)


def _tail_kernel(attn_ref, ga_ref, gg_ref, ha_ref, hg_ref, ma_ref, mb_ref, x_ref, mod_ref,
                 wa_ref, wc_ref, wo_ref, cw_ref, cvec_ref, gvec_ref,
                 x1_ref, h2_ref, u_ref, ush_ref, *, tiles_per_seq):
    first = (pl.program_id(0) % tiles_per_seq) == 0
    y_a = jnp.dot(attn_ref[...], wa_ref[...], preferred_element_type=F32)

    halo = ha_ref[...].astype(F32) * _sigmoid(hg_ref[...].astype(F32))
    u_ref[0:HALO, :] = jnp.where(first, 0.0, halo)
    u_ref[HALO:, :] = ga_ref[...].astype(F32) * _sigmoid(gg_ref[...].astype(F32))

    conv_b, ln_g, ln_b = cvec_ref[0:1, :], cvec_ref[1:2, :], cvec_ref[2:3, :]
    n_sh = HALO + TAIL_TM - SUBLANE
    for r in range(1, SUBLANE):
        ush_ref[r - 1, 0:n_sh, :] = u_ref[r:r + n_sh, :]
    blk = 32
    pieces = []
    for rb in range(TAIL_TM // blk):
        acc = jnp.zeros((blk, D_MODEL), F32)
        for j in range(CONV_WIDTH):
            start = HALO - (CONV_WIDTH - 1) + rb * blk + j
            r, a = start % SUBLANE, start - start % SUBLANE
            win = u_ref[a:a + blk, :] if r == 0 else ush_ref[r - 1, a:a + blk, :]
            acc = acc + cw_ref[j:j + 1, :] * win
        pieces.append(acc)
    y = jnp.concatenate(pieces, axis=0) + conv_b
    yc = y - jnp.mean(y, axis=-1, keepdims=True)
    yn = yc * lax.rsqrt(jnp.mean(yc * yc, axis=-1, keepdims=True) + NORM_EPS) * ln_g + ln_b
    act = yn * _sigmoid(yn)
    y_b = jnp.dot(act.astype(BF16), wc_ref[...], preferred_element_type=F32)

    merged = (_sigmoid(ma_ref[...].astype(F32)) * y_a + _sigmoid(mb_ref[...].astype(F32)) * y_b)
    out = jnp.dot(merged.astype(BF16), wo_ref[...], preferred_element_type=F32)

    gt1, sh2, sc2 = mod_ref[0, 0:1, :], mod_ref[0, 1:2, :], mod_ref[0, 2:3, :]
    x1 = x_ref[...] + gt1 * _rms(out, gvec_ref[0:1, :])
    x1_ref[...] = x1
    h2_ref[...] = (_rms(x1, gvec_ref[1:2, :]) * (1.0 + sc2) + sh2).astype(BF16)


def _mixer_tail(attn2d, rest, x2d, mod3, wa, wc, wo, cw, cvec, gvec, seq):
    t, d = x2d.shape
    tiles_per_seq = seq // TAIL_TM
    hpt = TAIL_TM // HALO
    row = lambda cb: pl.BlockSpec((TAIL_TM, d), lambda i: (i, cb))
    halo = lambda cb: pl.BlockSpec((HALO, d), lambda i: (jnp.maximum(i * hpt - 1, 0), cb))
    full = lambda a: pl.BlockSpec(a.shape, lambda i: (0,) * a.ndim)
    return pl.pallas_call(
        functools.partial(_tail_kernel, tiles_per_seq=tiles_per_seq),
        grid=(t // TAIL_TM,),
        in_specs=[row(0), row(REST_GLU_A), row(REST_GLU_G), halo(REST_GLU_A), halo(REST_GLU_G),
                  row(REST_MRG_A), row(REST_MRG_B), row(0),
                  pl.BlockSpec((1, 3, d), lambda i: (i // tiles_per_seq, 0, 0)),
                  full(wa), full(wc), full(wo), full(cw), full(cvec), full(gvec)],
        out_specs=[row(0), row(0)],
        out_shape=[jax.ShapeDtypeStruct((t, d), F32), jax.ShapeDtypeStruct((t, d), BF16)],
        scratch_shapes=[pltpu.VMEM((HALO + TAIL_TM, d), F32),
                        pltpu.VMEM((SUBLANE - 1, HALO + TAIL_TM, d), F32)],
        compiler_params=_cparams("parallel"),
        name="mixer_tail",
    )(attn2d, rest, rest, rest, rest, rest, rest, x2d, mod3, wa, wc, wo, cw, cvec, gvec)


def _cand_pairs():
    return [(i, j) for i in range(PEER_TOPK) for j in range(PEER_TOPK)
            if (i + 1) * (j + 1) <= PEER_TOPK]


def _top16_rows(s):
    n = s.shape[0]
    kio = lax.broadcasted_iota(jnp.int32, s.shape, 0).astype(F32)
    vals, idxs = [], []
    for _ in range(PEER_TOPK):
        m = jnp.max(s, axis=0, keepdims=True)
        idx = jnp.min(jnp.where(s == m, kio, float(n)), axis=0, keepdims=True)
        vals.append(m)
        idxs.append(idx)
        s = jnp.where(kio == idx, -jnp.inf, s)
    return vals, idxs


def _route_kernel(h2_ref, wqt_ref, sk_ref, f_ref, cnt_ref, g_ref, rb_ref):
    q2t = lax.dot_general(wqt_ref[...], h2_ref[...], NT_DIMS, preferred_element_type=F32)
    tl = q2t.shape[1]
    kio = lax.broadcasted_iota(jnp.int32, (PEER_NKEYS, tl), 0).astype(F32)
    pairs = _cand_pairs()
    n_pad = -len(pairs) % 8
    flat_ids = np.array([i * PEER_TOPK + j for i, j in pairs] + [PEER_TOPK ** 2] * n_pad, np.float32)
    group_start = [min(k for k, (i, _) in enumerate(pairs) if i == ii) for ii in range(PEER_TOPK)]
    group_len = [sum(1 for (i, _) in pairs if i == ii) for ii in range(PEER_TOPK)]
    n_rows = len(pairs) + n_pad
    flat_col = lax.broadcasted_iota(jnp.int32, (n_rows, tl), 0)
    flat = jnp.zeros((n_rows, tl), F32)
    for k in range(n_rows):
        flat = jnp.where(flat_col == k, float(flat_ids[k]), flat)

    for h in range(PEER_HEADS):
        scores, tops = [], []
        for c in range(2):
            hc = 2 * h + c
            qt = q2t[hc * LANE:(hc + 1) * LANE, :].astype(BF16)
            s = jnp.dot(sk_ref[hc], qt, preferred_element_type=F32)
            scores.append(s)
            tops.append(_top16_rows(s))
        (v1, i1), (v2, i2) = tops
        cand0 = jnp.concatenate([v1[i] + v2[j] for i, j in pairs]
                                + [jnp.full((n_pad, tl), -jnp.inf, F32)], axis=0)
        cand = cand0
        for _ in range(PEER_TOPK):
            m = jnp.max(cand, axis=0, keepdims=True)
            fid = jnp.min(jnp.where(cand == m, flat, float(PEER_TOPK ** 2 + 1)), axis=0, keepdims=True)
            cand = jnp.where(flat == fid, -jnp.inf, cand)
        picked = jnp.where((cand == -jnp.inf) & (flat < float(PEER_TOPK ** 2)), 1.0, 0.0)
        top = v1[0] + v2[0]
        z = jnp.sum(picked * jnp.exp(cand0 - top), axis=0, keepdims=True)
        counts = [jnp.sum(picked[group_start[i]:group_start[i] + group_len[i]], axis=0, keepdims=True)
                  for i in range(PEER_TOPK)]

        cnt = jnp.zeros((PEER_NKEYS, tl), F32)
        rank_b = jnp.full((PEER_NKEYS, tl), float(PEER_NKEYS), F32)
        for i in range(PEER_TOPK):
            cnt = jnp.where(kio == i1[i], counts[i], cnt)
            rank_b = jnp.where(kio == i2[i], float(i), rank_b)
        outs = ((f_ref, jnp.exp(scores[0] - v1[0]) * (0.5 / z)),
                (g_ref, jnp.exp(scores[1] - v2[0])), (cnt_ref, cnt), (rb_ref, rank_b))
        for ref, val in outs:
            for ch in range(tl // LANE):
                ref[h, ch] = val[:, ch * LANE:(ch + 1) * LANE].astype(ref.dtype)


def _route(h2, wqt, sk):
    t, d = h2.shape
    spec = pl.BlockSpec((PEER_HEADS, ROUTE_TL // LANE, PEER_NKEYS, LANE), lambda i: (0, i, 0, 0))
    out = jax.ShapeDtypeStruct((PEER_HEADS, t // LANE, PEER_NKEYS, LANE), F32)
    out16 = jax.ShapeDtypeStruct((PEER_HEADS, t // LANE, PEER_NKEYS, LANE), BF16)
    return pl.pallas_call(
        _route_kernel,
        grid=(t // ROUTE_TL,),
        in_specs=[pl.BlockSpec((ROUTE_TL, d), lambda i: (i, 0)),
                  pl.BlockSpec(wqt.shape, lambda i: (0, 0)),
                  pl.BlockSpec(sk.shape, lambda i: (0, 0, 0))],
        out_specs=[spec, spec, spec, spec],
        out_shape=[out, out, out16, out16],
        compiler_params=_cparams("parallel"),
        name="peer_route",
    )(h2, wqt, sk)


def _expert_kernel(h2_ref, u_ref, vt_ref, f_ref, cnt_ref, g_ref, rb_ref, x1_ref, mod_ref, gp_ref,
                   o_ref, acc_ref, act_ref, coef_ref):
    e = pl.program_id(1)
    n_chunks = EXP_TL // LANE
    a_per_step = EXP_TE // PEER_NKEYS
    a_group = 2

    @pl.when(e == 0)
    def _():
        acc_ref[...] = jnp.zeros_like(acc_ref)

    act = lax.dot_general(u_ref[...], h2_ref[...], NT_DIMS, preferred_element_type=F32)
    for c in range(n_chunks):
        act_ref[c] = act[:, c * LANE:(c + 1) * LANE]

    def lane_chunk(c, carry):
        for ag in range(a_per_step // a_group):
            coefs = [jnp.zeros((PEER_NKEYS, LANE), BF16) for _ in range(a_group)]

            def row_tile(ref, h, a):
                row = jnp.broadcast_to(ref[h, c, pl.ds(a, 1), :], (BF16_ROWS, LANE)).astype(BF16)
                return jnp.broadcast_to(row[None], (PEER_NKEYS // BF16_ROWS, BF16_ROWS, LANE)
                                        ).reshape(PEER_NKEYS, LANE)

            for h in range(PEER_HEADS):
                rank_b = rb_ref[h, c]
                g_b = g_ref[h, c]
                for k in range(a_group):
                    a = e * a_per_step + ag * a_group + k
                    picked = jnp.maximum(jnp.minimum(row_tile(cnt_ref, h, a) - rank_b, g_b), 0.0)
                    coefs[k] = coefs[k] + row_tile(f_ref, h, a) * picked
            for k in range(a_group):
                rows = pl.ds((ag * a_group + k) * PEER_NKEYS, PEER_NKEYS)
                coef_ref[c, rows, :] = _gelu_x2(act_ref[c, rows, :]).astype(BF16) * coefs[k]
        return carry

    lax.fori_loop(0, n_chunks, lane_chunk, 0)
    coef = jnp.concatenate([coef_ref[c] for c in range(n_chunks)], axis=1)
    acc_ref[...] += jnp.dot(vt_ref[0], coef, preferred_element_type=F32)

    @pl.when(e == pl.num_programs(1) - 1)
    def _():
        y = acc_ref[...].T
        o_ref[...] = x1_ref[...] + mod_ref[0] * _rms(y, gp_ref[...])


def _experts(h2, u, vt, f, cnt, g, rb, x1, gt2, gpost, seq):
    t, d = h2.shape
    n_exp = u.shape[0]
    tiles_per_seq = seq // EXP_TL
    n_chunks = EXP_TL // LANE
    route = pl.BlockSpec((PEER_HEADS, n_chunks, PEER_NKEYS, LANE), lambda i, e: (0, i, 0, 0))
    return pl.pallas_call(
        _expert_kernel,
        grid=(t // EXP_TL, n_exp // EXP_TE),
        in_specs=[pl.BlockSpec((EXP_TL, d), lambda i, e: (i, 0)),
                  pl.BlockSpec((EXP_TE, d), lambda i, e: (e, 0)),
                  pl.BlockSpec((1, d, EXP_TE), lambda i, e: (e, 0, 0)),
                  route, route, route, route,
                  pl.BlockSpec((EXP_TL, d), lambda i, e: (i, 0)),
                  pl.BlockSpec((1, 1, d), lambda i, e: (i // tiles_per_seq, 0, 0)),
                  pl.BlockSpec((1, d), lambda i, e: (0, 0))],
        out_specs=pl.BlockSpec((EXP_TL, d), lambda i, e: (i, 0)),
        out_shape=jax.ShapeDtypeStruct((t, d), F32),
        scratch_shapes=[pltpu.VMEM((d, EXP_TL), F32), pltpu.VMEM((n_chunks, EXP_TE, LANE), F32),
                        pltpu.VMEM((n_chunks, EXP_TE, LANE), BF16)],
        compiler_params=_cparams("parallel", "arbitrary"),
        name="peer_experts",
    )(h2, u, vt, f, cnt, g, rb, x1, gt2, gpost)


def _pad_heads(w, n, width):
    d = w.shape[0]
    return jnp.pad(w.reshape(d, n, width), ((0, 0), (0, 0), (0, LANE - width))).reshape(d, n * LANE)


def _split_w_in(w_in):
    q_cols = N_HEADS * D_QK
    k_cols = N_GROUPS * D_QK
    v_cols = N_GROUPS * D_V
    sizes = (q_cols, k_cols, k_cols, k_cols, v_cols, v_cols, v_cols, 3 * N_HEADS, 2 * D_MODEL, 2 * D_MODEL)
    offs = np.cumsum((0,) + sizes)
    parts = [w_in[:, offs[k]:offs[k + 1]] for k in range(len(sizes))]
    wq, wkc, wks, wkw, wvc, wvs, wvw, wgate, wglu, wmerge = parts
    w_qkv = jnp.concatenate(
        [_pad_heads(wq * (ATTN_SCALE * LOG2E), N_HEADS, D_QK)]
        + [_pad_heads(w, N_GROUPS, D_QK) for w in (wkc, wks, wkw)]
        + [_pad_heads(w, N_GROUPS, D_V) for w in (wvc, wvs, wvw)], axis=1)
    d = w_in.shape[0]
    wg = wgate.reshape(d, 3, N_GROUPS, HEADS_PER_GROUP).transpose(0, 2, 1, 3)
    wg = wg.reshape(d, N_GROUPS, 3 * HEADS_PER_GROUP)
    wg = jnp.pad(wg, ((0, 0), (0, 0), (0, LANE - 3 * HEADS_PER_GROUP))).reshape(d, N_GROUPS * LANE)
    w_rest = jnp.concatenate([wglu, wmerge, wg], axis=1)
    return w_qkv.astype(BF16), w_rest.astype(BF16)


def _cmp_weights(w1, w2, pos, dh):
    hidden = w1.shape[1]
    w1p = jnp.pad(w1.reshape(CMP_BLOCK, dh, hidden), ((0, 0), (0, LANE - dh), (0, 0)))
    w1p = w1p.reshape(2, CMP_STRIDE * LANE, hidden).astype(BF16)
    w2p = jnp.pad(w2, ((0, 0), (0, LANE - dh))).astype(BF16)
    posp = jnp.pad(pos, ((0, 0), (0, LANE - dh))).reshape(2, CMP_STRIDE * LANE)
    return w1p, w2p, posp


def kernel(x, c, w_ada, b_ada, g_pre_mix, g_post_mix, g_pre_ffn, g_post_ffn, rel_table, w_in,
           cmp_w1k, cmp_w2k, cmp_pos_k, cmp_w1v, cmp_w2v, cmp_pos_v, w_attn_out,
           conv_w, conv_b, conv_ln_g, conv_ln_b, w_conv_out, w_out,
           peer_wq, peer_subkeys, peer_u, peer_v):
    bsz, seq, d = x.shape
    t = bsz * seq
    assert d == D_MODEL and w_ada.shape[0] == 1, "single-layer block with D_MODEL channels"
    assert seq % PROJ_TM == 0 and seq // CMP_STRIDE == LANE and seq % EXP_TL == 0
    x2d = x.reshape(t, d)

    mod = _ada(c, w_ada[0], b_ada[0]).reshape(bsz, 6, d)
    mod_in = mod[:, 0:2]
    mod_tail = mod[:, 2:5]
    mod_out = mod[:, 5:6]

    bkt_c, bkt_t, bkt_w = _static_buckets(seq)
    bias_c = _bias_table(rel_table, bkt_c)
    bias_t = _bias_table(rel_table, bkt_t).reshape(N_HEADS, 3, TQ, TQ).transpose(1, 0, 2, 3)
    bias_w = _bias_table(rel_table, bkt_w)

    w_qkv, w_rest = _split_w_in(w_in[0])
    add_qkv = np.zeros((QKV_HEADS, LANE), np.float32)
    add_qkv[QKV_VS0:QKV_VS0 + N_GROUPS, D_V:] = 1.0
    add_qkv[QKV_VW0:QKV_VW0 + N_GROUPS, D_V:] = 1.0
    qkv = _project(x2d, mod_in, g_pre_mix, w_qkv, jnp.asarray(add_qkv.reshape(1, -1)), seq,
                   heads_out=True)
    rest = _project(x2d, mod_in, g_pre_mix, w_rest, jnp.zeros((1, REST_COLS), F32), seq,
                    heads_out=False)

    n_chunk = seq // CMP_STRIDE
    kch = qkv[:, QKV_KC0:QKV_KC0 + N_GROUPS].reshape(bsz, N_GROUPS, n_chunk, CMP_STRIDE * LANE)
    vch = qkv[:, QKV_VC0:QKV_VC0 + N_GROUPS].reshape(bsz, N_GROUPS, n_chunk, CMP_STRIDE * LANE)
    w1k, w2k, pk = _cmp_weights(cmp_w1k[0], cmp_w2k[0], cmp_pos_k[0], D_QK)
    w1v, w2v, pv = _cmp_weights(cmp_w1v[0], cmp_w2v[0], cmp_pos_v[0], D_V)
    kc, vc = _compress(kch, vch, pk, pv, w1k, w2k, w1v, w2v)

    pad = ((0, 0), (0, 0), (WINDOW, 0), (0, 0))
    kwp = jnp.pad(qkv[:, QKV_KW0:QKV_KW0 + N_GROUPS], pad)
    vwp = jnp.pad(qkv[:, QKV_VW0:QKV_VW0 + N_GROUPS], pad)

    n_idx = np.arange(LANE)[:, None] * CMP_STRIDE
    j_idx = np.arange(LANE)[None, :] * SLC_BLOCK
    overlap = ((n_idx <= j_idx + SLC_BLOCK - 1) & (n_idx + CMP_BLOCK - 1 >= j_idx)
               & (np.arange(LANE)[None, :] < seq // SLC_BLOCK)
               & (np.arange(LANE)[:, None] < n_chunk - 1)).astype(np.float32)
    expand = (np.arange(LANE)[:, None] == np.arange(seq)[None, :] // SLC_BLOCK)
    attn = _attention(qkv, kwp, vwp, kc, vc, rest, bias_c, bias_t, bias_w,
                      jnp.asarray(overlap), jnp.asarray(expand, dtype=BF16))

    cvec = jnp.stack([conv_b[0], conv_ln_g[0], conv_ln_b[0]])
    gvec = jnp.stack([g_post_mix[0], g_pre_ffn[0]])
    x1, h2 = _mixer_tail(attn.reshape(t, N_HEADS * D_V), rest, x2d, mod_tail,
                         w_attn_out[0].astype(BF16), w_conv_out[0].astype(BF16),
                         w_out[0].astype(BF16), conv_w[0, :, 0, :], cvec, gvec, seq)

    wqt = peer_wq[0].T.astype(BF16)
    sk = peer_subkeys[0].reshape(2 * PEER_HEADS, PEER_NKEYS, PEER_DQ // 2).astype(BF16)
    f, cnt, g, rb = _route(h2, wqt, sk)
    n_exp = peer_v.shape[1]
    vt = peer_v[0].reshape(n_exp // EXP_TE, EXP_TE, d).transpose(0, 2, 1).astype(BF16)
    out = _experts(h2, peer_u[0].astype(BF16), vt, f, cnt, g, rb, x1, mod_out, g_post_ffn, seq)
    return out.reshape(bsz, seq, d)
```

```python
import functools
import math

import numpy as np
import jax
import jax.numpy as jnp
from jax import lax
from jax.experimental import pallas as pl
from jax.experimental.pallas import tpu as pltpu

F32 = jnp.float32
BF16 = jnp.bfloat16

D_MODEL = 1024
N_HEADS = 16
N_GROUPS = 4
HEADS_PER_GROUP = N_HEADS // N_GROUPS
D_QK = 96
D_V = 64
CMP_BLOCK = 32
CMP_STRIDE = 16
CMP_HIDDEN = 256
SLC_BLOCK = 64
SLC_TOPN = 16
WINDOW = 512
ATTN_SCALE = D_QK ** -0.5
FORCE_BONUS = 1e6
N_BUCKETS = 32
MAX_DISTANCE = 128
CONV_WIDTH = 31
PEER_HEADS = 8
PEER_NKEYS = 128
PEER_DQ = 256
PEER_TOPK = 16
NORM_EPS = 1e-6
NEG_INF = -1e30
LOG2E = math.log2(math.e)

LANE = 128
SUBLANE = 8
BF16_ROWS = 16
VMEM_LIMIT = 56 * 1024 * 1024

TQ = 256
SLC_CHUNK = 256
WIN_SPAN = TQ + WINDOW
PROJ_TM = 2048
PROJ_TN_HEADS = 1024
PROJ_TN_REST = 768
TAIL_TM = 256
HALO = 32
ROUTE_TL = 256
EXP_TL = 512
EXP_TE = 2048

QKV_Q0, QKV_KC0, QKV_KS0, QKV_KW0, QKV_VC0, QKV_VS0, QKV_VW0 = 0, 16, 20, 24, 28, 32, 36
QKV_HEADS = 40
REST_GLU_A, REST_GLU_G, REST_MRG_A, REST_MRG_B = 0, 1, 2, 3
REST_GATE0 = 4 * D_MODEL
REST_COLS = 4 * D_MODEL + N_GROUPS * LANE

NT_DIMS = (((1,), (1,)), ((), ()))


def _cparams(*sem):
    return pltpu.CompilerParams(dimension_semantics=sem, vmem_limit_bytes=VMEM_LIMIT)


def _gelu(x):
    return 0.5 * x * (1.0 + jnp.tanh(math.sqrt(2.0 / math.pi) * (x + 0.044715 * (x * x * x))))


def _gelu_x2(x):
    k1 = math.sqrt(2.0 / math.pi)
    return x * (1.0 + jnp.tanh(x * (k1 + (k1 * 0.044715) * (x * x))))


def _sigmoid(x):
    return 1.0 / (1.0 + jnp.exp(-x))


def _rms(x, g):
    return x * lax.rsqrt(jnp.mean(x * x, axis=-1, keepdims=True) + NORM_EPS) * g


def _ada_kernel(c_ref, w_ref, b_ref, o_ref):
    c = c_ref[...]
    c_act = c * _sigmoid(c)
    o_ref[...] = jnp.dot(c_act, w_ref[...], preferred_element_type=F32,
                         precision=lax.Precision.HIGHEST) + b_ref[...]


def _ada(c, w, b):
    bsz, d = c.shape
    n = w.shape[1]
    tn = 1024
    return pl.pallas_call(
        _ada_kernel,
        grid=(n // tn,),
        in_specs=[pl.BlockSpec((bsz, d), lambda j: (0, 0)),
                  pl.BlockSpec((d, tn), lambda j: (0, j)),
                  pl.BlockSpec((1, tn), lambda j: (0, j))],
        out_specs=pl.BlockSpec((bsz, tn), lambda j: (0, j)),
        out_shape=jax.ShapeDtypeStruct((bsz, n), F32),
        compiler_params=_cparams("parallel"),
        name="ada",
    )(c, w, b.reshape(1, n))


def _t5_bucket_np(dist):
    n = np.maximum(dist, 0)
    max_exact = N_BUCKETS // 2
    nf = np.maximum(n, 1).astype(np.float64)
    large = max_exact + (np.log(nf / max_exact) / math.log(MAX_DISTANCE / max_exact)
                         * (N_BUCKETS - max_exact)).astype(np.int64)
    return np.where(n < max_exact, n, np.minimum(large, N_BUCKETS - 1)).astype(np.int32)


def _bias_kernel(tab_ref, bkt_ref, o_ref):
    h = pl.program_id(0)
    bkt = bkt_ref[...]
    acc = jnp.where(bkt < 0, NEG_INF, 0.0).astype(F32)
    for i in range(N_BUCKETS):
        acc = jnp.where(bkt == i, tab_ref[i, h] * LOG2E, acc)
    o_ref[0] = acc


def _bias_table(rel_table, bucket):
    rows, cols = bucket.shape
    return pl.pallas_call(
        _bias_kernel,
        grid=(N_HEADS,),
        in_specs=[pl.BlockSpec(memory_space=pltpu.SMEM),
                  pl.BlockSpec((rows, cols), lambda h: (0, 0))],
        out_specs=pl.BlockSpec((1, rows, cols), lambda h: (h, 0, 0)),
        out_shape=jax.ShapeDtypeStruct((N_HEADS, rows, cols), F32),
        compiler_params=_cparams("parallel"),
        name="bias_table",
    )(rel_table, jnp.asarray(bucket))


def _static_buckets(seq):
    t = np.arange(seq)[:, None]
    n = np.arange(LANE)[None, :]
    dist_c = t - (n * CMP_STRIDE + CMP_BLOCK - 1)
    bkt_c = np.where(dist_c >= 0, _t5_bucket_np(dist_c), -1).astype(np.int32)
    qi = np.arange(TQ)[:, None]
    kj = np.arange(TQ)[None, :]
    tiles = [np.where(d * TQ + qi - kj >= 0, _t5_bucket_np(d * TQ + qi - kj), -1) for d in range(3)]
    bkt_t = np.concatenate(tiles, axis=0).astype(np.int32)
    kw = np.arange(WIN_SPAN)[None, :]
    dist_w = WINDOW + qi - kw
    band = (dist_w >= 0) & (dist_w < WINDOW)
    bkt_w = np.where(band, _t5_bucket_np(dist_w), -1).astype(np.int32)
    return bkt_c, bkt_t, bkt_w


def _proj_kernel(x_ref, mod_ref, g_ref, w_ref, add_ref, o_ref, h_ref, *, heads_out, tn):
    @pl.when(pl.program_id(1) == 0)
    def _():
        x = x_ref[...]
        sh = mod_ref[0, 0:1, :]
        sc = mod_ref[0, 1:2, :]
        h_ref[...] = (_rms(x, g_ref[...]) * (1.0 + sc) + sh).astype(BF16)

    res = (jnp.dot(h_ref[...], w_ref[...], preferred_element_type=F32) + add_ref[...]).astype(o_ref.dtype)
    if heads_out:
        for k in range(tn // LANE):
            o_ref[0, k] = res[:, k * LANE:(k + 1) * LANE]
    else:
        o_ref[...] = res


def _project(x2d, mod3, g, w, add, seq, heads_out):
    t, d = x2d.shape
    n = w.shape[1]
    bsz = t // seq
    tiles_per_seq = seq // PROJ_TM
    tn = PROJ_TN_HEADS if heads_out else PROJ_TN_REST
    assert n % tn == 0
    grid = (t // PROJ_TM, n // tn)
    if heads_out:
        hpt = tn // LANE
        out_shape = jax.ShapeDtypeStruct((bsz, n // LANE, seq, LANE), BF16)
        out_spec = pl.BlockSpec((1, hpt, PROJ_TM, LANE),
                                lambda i, j: (i // tiles_per_seq, j, i % tiles_per_seq, 0))
    else:
        out_shape = jax.ShapeDtypeStruct((t, n), BF16)
        out_spec = pl.BlockSpec((PROJ_TM, tn), lambda i, j: (i, j))
    return pl.pallas_call(
        functools.partial(_proj_kernel, heads_out=heads_out, tn=tn),
        grid=grid,
        in_specs=[pl.BlockSpec((PROJ_TM, d), lambda i, j: (i, 0)),
                  pl.BlockSpec((1, 2, d), lambda i, j: (i // tiles_per_seq, 0, 0)),
                  pl.BlockSpec((1, d), lambda i, j: (0, 0)),
                  pl.BlockSpec((d, tn), lambda i, j: (0, j)),
                  pl.BlockSpec((1, tn), lambda i, j: (0, j))],
        out_specs=out_spec,
        out_shape=out_shape,
        scratch_shapes=[pltpu.VMEM((PROJ_TM, d), BF16)],
        compiler_params=_cparams("parallel", "arbitrary"),
        name="proj_heads" if heads_out else "proj_rest",
    )(x2d, mod3, g, w, add)


def _compress_kernel(kch_ref, vch_ref, pk_ref, pv_ref, w1k_ref, w2k_ref, w1v_ref, w2v_ref,
                     kc_ref, vc_ref):
    def one(ch_ref, pos_ref, w1_ref, w2_ref, o_ref):
        a = ch_ref[0, 0].astype(F32)
        a1 = (a + pos_ref[0:1, :]).astype(BF16)
        a2 = (a + pos_ref[1:2, :]).astype(BF16)
        p1 = jnp.dot(a1, w1_ref[0], preferred_element_type=F32)
        p2 = jnp.dot(a2, w1_ref[1], preferred_element_type=F32)
        n_chunk = p2.shape[0]
        hid = _gelu(p1 + pltpu.roll(p2, n_chunk - 1, 0))
        o_ref[0, 0] = jnp.dot(hid.astype(BF16), w2_ref[...],
                              preferred_element_type=F32).astype(BF16)

    one(kch_ref, pk_ref, w1k_ref, w2k_ref, kc_ref)
    one(vch_ref, pv_ref, w1v_ref, w2v_ref, vc_ref)


def _compress(kch, vch, pk, pv, w1k, w2k, w1v, w2v):
    bsz, g, n_chunk, width = kch.shape
    assert n_chunk == LANE, "compressed keys are laid out on one 128-row tile"
    spec_in = pl.BlockSpec((1, 1, n_chunk, width), lambda b, gg: (b, gg, 0, 0))
    spec_out = pl.BlockSpec((1, 1, n_chunk, LANE), lambda b, gg: (b, gg, 0, 0))
    full = lambda a: pl.BlockSpec(a.shape, lambda b, gg: (0,) * a.ndim)
    out = jax.ShapeDtypeStruct((bsz, g, n_chunk, LANE), BF16)
    return pl.pallas_call(
        _compress_kernel,
        grid=(bsz, g),
        in_specs=[spec_in, spec_in, full(pk), full(pv), full(w1k), full(w2k), full(w1v), full(w2v)],
        out_specs=[spec_out, spec_out],
        out_shape=[out, out],
        compiler_params=_cparams("parallel", "parallel"),
        name="compress",
    )(kch, vch, pk, pv, w1k, w2k, w1v, w2v)


def _attn_kernel(q_ref, kc_ref, vc_ref, ks_ref, vs_ref, kw_ref, vw_ref, gate_ref,
                 bias_c_ref, bias_t_ref, bias_w_ref, overlap_ref, expand_ref,
                 o_ref, mask_ref):
    qi = pl.program_id(2)
    rows = HEADS_PER_GROUP * TQ
    seq = mask_ref.shape[1]
    q4 = q_ref[0].reshape(rows, LANE)

    def per_head(x):
        return jnp.broadcast_to(x[None], (HEADS_PER_GROUP,) + x.shape).reshape(rows, x.shape[-1])

    lc = lax.dot_general(q4, kc_ref[0, 0], NT_DIMS, preferred_element_type=F32)
    lc = lc + bias_c_ref[...].reshape(rows, LANE)
    valid = lc > 0.5 * NEG_INF
    mc = jnp.max(lc, axis=-1, keepdims=True)
    ec = jnp.where(valid, jnp.exp2(lc - mc), 0.0)
    sc = jnp.sum(ec, axis=-1, keepdims=True)
    pc = ec / jnp.where(sc > 0.0, sc, 1.0)
    o_cmp = jnp.dot(pc.astype(BF16), vc_ref[0, 0], preferred_element_type=F32)

    psum = jnp.sum(pc.reshape(HEADS_PER_GROUP, TQ, LANE), axis=0)
    imp = jnp.dot(psum, overlap_ref[...], preferred_element_type=F32,
                  precision=lax.Precision.HIGHEST)
    t_pos = qi * TQ + lax.broadcasted_iota(jnp.int32, (TQ, LANE), 0)
    j_blk = lax.broadcasted_iota(jnp.int32, (TQ, LANE), 1)
    cur = t_pos // SLC_BLOCK
    forced = (j_blk == 0) | (j_blk == cur) | (j_blk == cur - 1)
    score = jnp.where(j_blk <= cur, imp + jnp.where(forced, FORCE_BONUS, 0.0), NEG_INF)
    n_slc = seq // SLC_BLOCK
    sc_t = score.T[:n_slc]
    j_row = lax.broadcasted_iota(jnp.int32, (n_slc, TQ), 0)
    rank = jnp.zeros((n_slc, TQ), F32)
    for jp in range(n_slc):
        row = sc_t[jp:jp + 1, :]
        ge = jnp.where(row >= sc_t, 1.0, 0.0)
        gt = jnp.where(row > sc_t, 1.0, 0.0)
        rank = rank + jnp.where(j_row > jp, ge, gt)
    sel_t = jnp.where(rank < float(min(SLC_TOPN, n_slc)), 1.0, 0.0)
    sel_t = jnp.concatenate([sel_t, jnp.zeros((LANE - n_slc, TQ), F32)], axis=0)
    sel = ((sel_t.T - 1.0) * (-NEG_INF)).astype(BF16)
    mask_ref[...] = jnp.dot(sel, expand_ref[...], preferred_element_type=F32)

    tiles_per_chunk = SLC_CHUNK // TQ

    def slc_step(c, carry):
        m, acc = carry
        k0 = pl.multiple_of(c * SLC_CHUNK, SLC_CHUNK)
        kt = ks_ref[0, 0, pl.ds(k0, SLC_CHUNK), :]
        vt = vs_ref[0, 0, pl.ds(k0, SLC_CHUNK), :]
        s = lax.dot_general(q4, kt, NT_DIMS, preferred_element_type=F32)
        bias = jnp.concatenate(
            [bias_t_ref[jnp.clip(qi - (c * tiles_per_chunk + k), 0, 2)]
             for k in range(tiles_per_chunk)], axis=-1)
        s = s + bias.reshape(rows, SLC_CHUNK) + per_head(mask_ref[:, pl.ds(k0, SLC_CHUNK)])
        m_new = jnp.maximum(m, jnp.max(s, axis=-1, keepdims=True))
        p = jnp.exp2(s - m_new)
        acc = jnp.exp2(m - m_new) * acc + jnp.dot(p.astype(BF16), vt, preferred_element_type=F32)
        return m_new, acc

    n_chunks = (qi + tiles_per_chunk) // tiles_per_chunk
    init = (jnp.full((rows, 1), -jnp.inf, F32), jnp.zeros((rows, LANE), F32))
    _, acc_s = lax.fori_loop(0, n_chunks, slc_step, init)
    o_slc = acc_s / pltpu.roll(acc_s, LANE - D_V, 1)

    w0 = pl.multiple_of(qi * TQ, TQ)
    kwin = kw_ref[0, 0, pl.ds(w0, WIN_SPAN), :]
    vwin = vw_ref[0, 0, pl.ds(w0, WIN_SPAN), :]
    sw = lax.dot_general(q4, kwin, NT_DIMS, preferred_element_type=F32)
    sw = sw + bias_w_ref[...].reshape(rows, WIN_SPAN)
    kj = lax.broadcasted_iota(jnp.int32, (1, WIN_SPAN), 1)
    sw = sw + jnp.where(kj >= WINDOW - qi * TQ, 0.0, NEG_INF)
    mw = jnp.max(sw, axis=-1, keepdims=True)
    pw = jnp.exp2(sw - mw)
    o_win = jnp.dot(pw.astype(BF16), vwin, preferred_element_type=F32)
    o_win = o_win / pltpu.roll(o_win, LANE - D_V, 1)

    gates = _sigmoid(gate_ref[...].astype(F32))
    outs = []
    for r in range(HEADS_PER_GROUP):
        sl = slice(r * TQ, (r + 1) * TQ)
        g_c = gates[:, r:r + 1]
        g_s = gates[:, HEADS_PER_GROUP + r:HEADS_PER_GROUP + r + 1]
        g_w = gates[:, 2 * HEADS_PER_GROUP + r:2 * HEADS_PER_GROUP + r + 1]
        o = g_c * o_cmp[sl] + g_s * o_slc[sl] + g_w * o_win[sl]
        outs.append(o[:, :D_V])
    o_ref[0] = jnp.concatenate(outs, axis=-1).astype(o_ref.dtype)


def _attention(qkv, kwp, vwp, kc, vc, rest, bias_c, bias_t, bias_w, overlap, expand):
    bsz, _, seq, _ = qkv.shape
    nq = seq // TQ
    g_cols0 = REST_GATE0 // LANE
    head = lambda base: pl.BlockSpec((1, 1, seq, LANE), lambda b, g, i: (b, base + g, 0, 0))
    in_specs = [
        pl.BlockSpec((1, HEADS_PER_GROUP, TQ, LANE), lambda b, g, i: (b, g, i, 0)),
        pl.BlockSpec((1, 1, LANE, LANE), lambda b, g, i: (b, g, 0, 0)),
        pl.BlockSpec((1, 1, LANE, LANE), lambda b, g, i: (b, g, 0, 0)),
        head(QKV_KS0), head(QKV_VS0),
        pl.BlockSpec((1, 1, seq + WINDOW, LANE), lambda b, g, i: (b, g, 0, 0)),
        pl.BlockSpec((1, 1, seq + WINDOW, LANE), lambda b, g, i: (b, g, 0, 0)),
        pl.BlockSpec((TQ, LANE), lambda b, g, i: (b * nq + i, g_cols0 + g)),
        pl.BlockSpec((HEADS_PER_GROUP, TQ, LANE), lambda b, g, i: (g, i, 0)),
        pl.BlockSpec((3, HEADS_PER_GROUP, TQ, TQ), lambda b, g, i: (0, g, 0, 0)),
        pl.BlockSpec((HEADS_PER_GROUP, TQ, WIN_SPAN), lambda b, g, i: (g, 0, 0)),
        pl.BlockSpec((LANE, LANE), lambda b, g, i: (0, 0)),
        pl.BlockSpec((LANE, seq), lambda b, g, i: (0, 0)),
    ]
    return pl.pallas_call(
        _attn_kernel,
        grid=(bsz, N_GROUPS, nq),
        in_specs=in_specs,
        out_specs=pl.BlockSpec((1, TQ, HEADS_PER_GROUP * D_V), lambda b, g, i: (b, i, g)),
        out_shape=jax.ShapeDtypeStruct((bsz, seq, N_HEADS * D_V), BF16),
        scratch_shapes=[pltpu.VMEM((TQ, seq), F32)],
        compiler_params=_cparams("parallel", "parallel", "arbitrary"),
        name="attention",
    )(qkv, kc, vc, qkv, qkv, kwp, vwp, rest, bias_c, bias_t, bias_w, overlap, expand)


def _tail_kernel(attn_ref, ga_ref, gg_ref, ha_ref, hg_ref, ma_ref, mb_ref, x_ref, mod_ref,
                 wa_ref, wc_ref, wo_ref, cw_ref, cvec_ref, gvec_ref,
                 x1_ref, h2t_ref, u_ref, ush_ref, *, tiles_per_seq):
    first = (pl.program_id(0) % tiles_per_seq) == 0
    y_a = jnp.dot(attn_ref[...], wa_ref[...], preferred_element_type=F32)

    halo = ha_ref[...].astype(F32) * _sigmoid(hg_ref[...].astype(F32))
    u_ref[0:HALO, :] = jnp.where(first, 0.0, halo)
    u_ref[HALO:, :] = ga_ref[...].astype(F32) * _sigmoid(gg_ref[...].astype(F32))

    conv_b, ln_g, ln_b = cvec_ref[0:1, :], cvec_ref[1:2, :], cvec_ref[2:3, :]
    n_sh = HALO + TAIL_TM - SUBLANE
    for r in range(1, SUBLANE):
        ush_ref[r - 1, 0:n_sh, :] = u_ref[r:r + n_sh, :]
    blk = 32
    pieces = []
    for rb in range(TAIL_TM // blk):
        acc = jnp.zeros((blk, D_MODEL), F32)
        for j in range(CONV_WIDTH):
            start = HALO - (CONV_WIDTH - 1) + rb * blk + j
            r, a = start % SUBLANE, start - start % SUBLANE
            win = u_ref[a:a + blk, :] if r == 0 else ush_ref[r - 1, a:a + blk, :]
            acc = acc + cw_ref[j:j + 1, :] * win
        pieces.append(acc)
    y = jnp.concatenate(pieces, axis=0) + conv_b
    yc = y - jnp.mean(y, axis=-1, keepdims=True)
    yn = yc * lax.rsqrt(jnp.mean(yc * yc, axis=-1, keepdims=True) + NORM_EPS) * ln_g + ln_b
    act = yn * _sigmoid(yn)
    y_b = jnp.dot(act.astype(BF16), wc_ref[...], preferred_element_type=F32)

    merged = (_sigmoid(ma_ref[...].astype(F32)) * y_a + _sigmoid(mb_ref[...].astype(F32)) * y_b)
    out = jnp.dot(merged.astype(BF16), wo_ref[...], preferred_element_type=F32)

    gt1, sh2, sc2 = mod_ref[0, 0:1, :], mod_ref[0, 1:2, :], mod_ref[0, 2:3, :]
    x1 = x_ref[...] + gt1 * _rms(out, gvec_ref[0:1, :])
    x1_ref[...] = x1
    h2 = _rms(x1, gvec_ref[1:2, :]) * (1.0 + sc2) + sh2
    h2t_ref[...] = h2.T.astype(BF16)


def _mixer_tail(attn2d, rest, x2d, mod3, wa, wc, wo, cw, cvec, gvec, seq):
    t, d = x2d.shape
    tiles_per_seq = seq // TAIL_TM
    hpt = TAIL_TM // HALO
    row = lambda cb: pl.BlockSpec((TAIL_TM, d), lambda i: (i, cb))
    halo = lambda cb: pl.BlockSpec((HALO, d), lambda i: (jnp.maximum(i * hpt - 1, 0), cb))
    full = lambda a: pl.BlockSpec(a.shape, lambda i: (0,) * a.ndim)
    return pl.pallas_call(
        functools.partial(_tail_kernel, tiles_per_seq=tiles_per_seq),
        grid=(t // TAIL_TM,),
        in_specs=[row(0), row(REST_GLU_A), row(REST_GLU_G), halo(REST_GLU_A), halo(REST_GLU_G),
                  row(REST_MRG_A), row(REST_MRG_B), row(0),
                  pl.BlockSpec((1, 3, d), lambda i: (i // tiles_per_seq, 0, 0)),
                  full(wa), full(wc), full(wo), full(cw), full(cvec), full(gvec)],
        out_specs=[row(0), pl.BlockSpec((d, TAIL_TM), lambda i: (0, i))],
        out_shape=[jax.ShapeDtypeStruct((t, d), F32), jax.ShapeDtypeStruct((d, t), BF16)],
        scratch_shapes=[pltpu.VMEM((HALO + TAIL_TM, d), F32),
                        pltpu.VMEM((SUBLANE - 1, HALO + TAIL_TM, d), F32)],
        compiler_params=_cparams("parallel"),
        name="mixer_tail",
    )(attn2d, rest, rest, rest, rest, rest, rest, x2d, mod3, wa, wc, wo, cw, cvec, gvec)


def _cand_pairs():
    return [(i, j) for i in range(PEER_TOPK) for j in range(PEER_TOPK)
            if (i + 1) * (j + 1) <= PEER_TOPK]


def _top16_rows(s):
    n = s.shape[0]
    kio = lax.broadcasted_iota(jnp.int32, s.shape, 0).astype(F32)
    vals, idxs = [], []
    for _ in range(PEER_TOPK):
        m = jnp.max(s, axis=0, keepdims=True)
        idx = jnp.min(jnp.where(s == m, kio, float(n)), axis=0, keepdims=True)
        vals.append(m)
        idxs.append(idx)
        s = jnp.where(kio == idx, -jnp.inf, s)
    return vals, idxs


def _route_kernel(h2t_ref, wqt_ref, sk_ref, f_ref, cnt_ref, g_ref, rb_ref):
    q2t = jnp.dot(wqt_ref[...], h2t_ref[...], preferred_element_type=F32)
    tl = q2t.shape[1]
    kio = lax.broadcasted_iota(jnp.int32, (PEER_NKEYS, tl), 0).astype(F32)
    pairs = _cand_pairs()
    n_pad = -len(pairs) % 8
    flat_ids = np.array([i * PEER_TOPK + j for i, j in pairs] + [PEER_TOPK ** 2] * n_pad, np.float32)
    group_start = [min(k for k, (i, _) in enumerate(pairs) if i == ii) for ii in range(PEER_TOPK)]
    group_len = [sum(1 for (i, _) in pairs if i == ii) for ii in range(PEER_TOPK)]
    n_rows = len(pairs) + n_pad
    flat_col = lax.broadcasted_iota(jnp.int32, (n_rows, tl), 0)
    flat = jnp.zeros((n_rows, tl), F32)
    for k in range(n_rows):
        flat = jnp.where(flat_col == k, float(flat_ids[k]), flat)

    for h in range(PEER_HEADS):
        scores, tops = [], []
        for c in range(2):
            hc = 2 * h + c
            qt = q2t[hc * LANE:(hc + 1) * LANE, :].astype(BF16)
            s = jnp.dot(sk_ref[hc], qt, preferred_element_type=F32)
            scores.append(s)
            tops.append(_top16_rows(s))
        (v1, i1), (v2, i2) = tops
        cand0 = jnp.concatenate([v1[i] + v2[j] for i, j in pairs]
                                + [jnp.full((n_pad, tl), -jnp.inf, F32)], axis=0)
        cand = cand0
        for _ in range(PEER_TOPK):
            m = jnp.max(cand, axis=0, keepdims=True)
            fid = jnp.min(jnp.where(cand == m, flat, float(PEER_TOPK ** 2 + 1)), axis=0, keepdims=True)
            cand = jnp.where(flat == fid, -jnp.inf, cand)
        picked = jnp.where((cand == -jnp.inf) & (flat < float(PEER_TOPK ** 2)), 1.0, 0.0)
        top = v1[0] + v2[0]
        z = jnp.sum(picked * jnp.exp(cand0 - top), axis=0, keepdims=True)
        counts = [jnp.sum(picked[group_start[i]:group_start[i] + group_len[i]], axis=0, keepdims=True)
                  for i in range(PEER_TOPK)]

        cnt = jnp.zeros((PEER_NKEYS, tl), F32)
        rank_b = jnp.full((PEER_NKEYS, tl), float(PEER_NKEYS), F32)
        for i in range(PEER_TOPK):
            cnt = jnp.where(kio == i1[i], counts[i], cnt)
            rank_b = jnp.where(kio == i2[i], float(i), rank_b)
        outs = ((f_ref, jnp.exp(scores[0] - v1[0]) * (0.5 / z)),
                (g_ref, jnp.exp(scores[1] - v2[0])), (cnt_ref, cnt), (rb_ref, rank_b))
        for ref, val in outs:
            for ch in range(tl // LANE):
                ref[h, ch] = val[:, ch * LANE:(ch + 1) * LANE].astype(ref.dtype)


def _route(h2t, wqt, sk):
    d, t = h2t.shape
    spec = pl.BlockSpec((PEER_HEADS, ROUTE_TL // LANE, PEER_NKEYS, LANE), lambda i: (0, i, 0, 0))
    out = jax.ShapeDtypeStruct((PEER_HEADS, t // LANE, PEER_NKEYS, LANE), F32)
    out16 = jax.ShapeDtypeStruct((PEER_HEADS, t // LANE, PEER_NKEYS, LANE), BF16)
    return pl.pallas_call(
        _route_kernel,
        grid=(t // ROUTE_TL,),
        in_specs=[pl.BlockSpec((d, ROUTE_TL), lambda i: (0, i)),
                  pl.BlockSpec(wqt.shape, lambda i: (0, 0)),
                  pl.BlockSpec(sk.shape, lambda i: (0, 0, 0))],
        out_specs=[spec, spec, spec, spec],
        out_shape=[out, out, out16, out16],
        compiler_params=_cparams("parallel"),
        name="peer_route",
    )(h2t, wqt, sk)


def _expert_kernel(h2t_ref, u_ref, vt_ref, f_ref, cnt_ref, g_ref, rb_ref, x1_ref, mod_ref, gp_ref,
                   o_ref, acc_ref, act_ref, coef_ref):
    e = pl.program_id(1)
    n_chunks = EXP_TL // LANE
    a_per_step = EXP_TE // PEER_NKEYS
    a_group = 2

    @pl.when(e == 0)
    def _():
        acc_ref[...] = jnp.zeros_like(acc_ref)

    act = jnp.dot(u_ref[...], h2t_ref[...], preferred_element_type=F32)
    for c in range(n_chunks):
        act_ref[c] = act[:, c * LANE:(c + 1) * LANE]

    def lane_chunk(c, carry):
        for ag in range(a_per_step // a_group):
            coefs = [jnp.zeros((PEER_NKEYS, LANE), BF16) for _ in range(a_group)]

            def row_tile(ref, h, a):
                row = jnp.broadcast_to(ref[h, c, pl.ds(a, 1), :], (BF16_ROWS, LANE)).astype(BF16)
                return jnp.broadcast_to(row[None], (PEER_NKEYS // BF16_ROWS, BF16_ROWS, LANE)
                                        ).reshape(PEER_NKEYS, LANE)

            for h in range(PEER_HEADS):
                rank_b = rb_ref[h, c]
                g_b = g_ref[h, c]
                for k in range(a_group):
                    a = e * a_per_step + ag * a_group + k
                    picked = jnp.maximum(jnp.minimum(row_tile(cnt_ref, h, a) - rank_b, g_b), 0.0)
                    coefs[k] = coefs[k] + row_tile(f_ref, h, a) * picked
            for k in range(a_group):
                rows = pl.ds((ag * a_group + k) * PEER_NKEYS, PEER_NKEYS)
                coef_ref[c, rows, :] = _gelu_x2(act_ref[c, rows, :]).astype(BF16) * coefs[k]
        return carry

    lax.fori_loop(0, n_chunks, lane_chunk, 0)
    coef = jnp.concatenate([coef_ref[c] for c in range(n_chunks)], axis=1)
    acc_ref[...] += jnp.dot(vt_ref[0], coef, preferred_element_type=F32)

    @pl.when(e == pl.num_programs(1) - 1)
    def _():
        y = acc_ref[...].T
        o_ref[...] = x1_ref[...] + mod_ref[0] * _rms(y, gp_ref[...])


def _experts(h2t, u, vt, f, cnt, g, rb, x1, gt2, gpost, seq):
    d, t = h2t.shape
    n_exp = u.shape[0]
    tiles_per_seq = seq // EXP_TL
    n_chunks = EXP_TL // LANE
    route = pl.BlockSpec((PEER_HEADS, n_chunks, PEER_NKEYS, LANE), lambda i, e: (0, i, 0, 0))
    return pl.pallas_call(
        _expert_kernel,
        grid=(t // EXP_TL, n_exp // EXP_TE),
        in_specs=[pl.BlockSpec((d, EXP_TL), lambda i, e: (0, i)),
                  pl.BlockSpec((EXP_TE, d), lambda i, e: (e, 0)),
                  pl.BlockSpec((1, d, EXP_TE), lambda i, e: (e, 0, 0)),
                  route, route, route, route,
                  pl.BlockSpec((EXP_TL, d), lambda i, e: (i, 0)),
                  pl.BlockSpec((1, 1, d), lambda i, e: (i // tiles_per_seq, 0, 0)),
                  pl.BlockSpec((1, d), lambda i, e: (0, 0))],
        out_specs=pl.BlockSpec((EXP_TL, d), lambda i, e: (i, 0)),
        out_shape=jax.ShapeDtypeStruct((t, d), F32),
        scratch_shapes=[pltpu.VMEM((d, EXP_TL), F32), pltpu.VMEM((n_chunks, EXP_TE, LANE), F32),
                        pltpu.VMEM((n_chunks, EXP_TE, LANE), BF16)],
        compiler_params=_cparams("parallel", "arbitrary"),
        name="peer_experts",
    )(h2t, u, vt, f, cnt, g, rb, x1, gt2, gpost)


def _pad_heads(w, n, width):
    d = w.shape[0]
    return jnp.pad(w.reshape(d, n, width), ((0, 0), (0, 0), (0, LANE - width))).reshape(d, n * LANE)


def _split_w_in(w_in):
    q_cols = N_HEADS * D_QK
    k_cols = N_GROUPS * D_QK
    v_cols = N_GROUPS * D_V
    sizes = (q_cols, k_cols, k_cols, k_cols, v_cols, v_cols, v_cols, 3 * N_HEADS, 2 * D_MODEL, 2 * D_MODEL)
    offs = np.cumsum((0,) + sizes)
    parts = [w_in[:, offs[k]:offs[k + 1]] for k in range(len(sizes))]
    wq, wkc, wks, wkw, wvc, wvs, wvw, wgate, wglu, wmerge = parts
    w_qkv = jnp.concatenate(
        [_pad_heads(wq * (ATTN_SCALE * LOG2E), N_HEADS, D_QK)]
        + [_pad_heads(w, N_GROUPS, D_QK) for w in (wkc, wks, wkw)]
        + [_pad_heads(w, N_GROUPS, D_V) for w in (wvc, wvs, wvw)], axis=1)
    d = w_in.shape[0]
    wg = wgate.reshape(d, 3, N_GROUPS, HEADS_PER_GROUP).transpose(0, 2, 1, 3)
    wg = wg.reshape(d, N_GROUPS, 3 * HEADS_PER_GROUP)
    wg = jnp.pad(wg, ((0, 0), (0, 0), (0, LANE - 3 * HEADS_PER_GROUP))).reshape(d, N_GROUPS * LANE)
    w_rest = jnp.concatenate([wglu, wmerge, wg], axis=1)
    return w_qkv.astype(BF16), w_rest.astype(BF16)


def _cmp_weights(w1, w2, pos, dh):
    hidden = w1.shape[1]
    w1p = jnp.pad(w1.reshape(CMP_BLOCK, dh, hidden), ((0, 0), (0, LANE - dh), (0, 0)))
    w1p = w1p.reshape(2, CMP_STRIDE * LANE, hidden).astype(BF16)
    w2p = jnp.pad(w2, ((0, 0), (0, LANE - dh))).astype(BF16)
    posp = jnp.pad(pos, ((0, 0), (0, LANE - dh))).reshape(2, CMP_STRIDE * LANE)
    return w1p, w2p, posp


def kernel(x, c, w_ada, b_ada, g_pre_mix, g_post_mix, g_pre_ffn, g_post_ffn, rel_table, w_in,
           cmp_w1k, cmp_w2k, cmp_pos_k, cmp_w1v, cmp_w2v, cmp_pos_v, w_attn_out,
           conv_w, conv_b, conv_ln_g, conv_ln_b, w_conv_out, w_out,
           peer_wq, peer_subkeys, peer_u, peer_v):
    bsz, seq, d = x.shape
    t = bsz * seq
    assert d == D_MODEL and w_ada.shape[0] == 1, "single-layer block with D_MODEL channels"
    assert seq % PROJ_TM == 0 and seq // CMP_STRIDE == LANE and seq % EXP_TL == 0
    x2d = x.reshape(t, d)

    mod = _ada(c, w_ada[0], b_ada[0]).reshape(bsz, 6, d)
    mod_in = mod[:, 0:2]
    mod_tail = mod[:, 2:5]
    mod_out = mod[:, 5:6]

    bkt_c, bkt_t, bkt_w = _static_buckets(seq)
    bias_c = _bias_table(rel_table, bkt_c)
    bias_t = _bias_table(rel_table, bkt_t).reshape(N_HEADS, 3, TQ, TQ).transpose(1, 0, 2, 3)
    bias_w = _bias_table(rel_table, bkt_w)

    w_qkv, w_rest = _split_w_in(w_in[0])
    add_qkv = np.zeros((QKV_HEADS, LANE), np.float32)
    add_qkv[QKV_VS0:QKV_VS0 + N_GROUPS, D_V:] = 1.0
    add_qkv[QKV_VW0:QKV_VW0 + N_GROUPS, D_V:] = 1.0
    qkv = _project(x2d, mod_in, g_pre_mix, w_qkv, jnp.asarray(add_qkv.reshape(1, -1)), seq,
                   heads_out=True)
    rest = _project(x2d, mod_in, g_pre_mix, w_rest, jnp.zeros((1, REST_COLS), F32), seq,
                    heads_out=False)

    n_chunk = seq // CMP_STRIDE
    kch = qkv[:, QKV_KC0:QKV_KC0 + N_GROUPS].reshape(bsz, N_GROUPS, n_chunk, CMP_STRIDE * LANE)
    vch = qkv[:, QKV_VC0:QKV_VC0 + N_GROUPS].reshape(bsz, N_GROUPS, n_chunk, CMP_STRIDE * LANE)
    w1k, w2k, pk = _cmp_weights(cmp_w1k[0], cmp_w2k[0], cmp_pos_k[0], D_QK)
    w1v, w2v, pv = _cmp_weights(cmp_w1v[0], cmp_w2v[0], cmp_pos_v[0], D_V)
    kc, vc = _compress(kch, vch, pk, pv, w1k, w2k, w1v, w2v)

    pad = ((0, 0), (0, 0), (WINDOW, 0), (0, 0))
    kwp = jnp.pad(qkv[:, QKV_KW0:QKV_KW0 + N_GROUPS], pad)
    vwp = jnp.pad(qkv[:, QKV_VW0:QKV_VW0 + N_GROUPS], pad)

    n_idx = np.arange(LANE)[:, None] * CMP_STRIDE
    j_idx = np.arange(LANE)[None, :] * SLC_BLOCK
    overlap = ((n_idx <= j_idx + SLC_BLOCK - 1) & (n_idx + CMP_BLOCK - 1 >= j_idx)
               & (np.arange(LANE)[None, :] < seq // SLC_BLOCK)
               & (np.arange(LANE)[:, None] < n_chunk - 1)).astype(np.float32)
    expand = (np.arange(LANE)[:, None] == np.arange(seq)[None, :] // SLC_BLOCK)
    attn = _attention(qkv, kwp, vwp, kc, vc, rest, bias_c, bias_t, bias_w,
                      jnp.asarray(overlap), jnp.asarray(expand, dtype=BF16))

    cvec = jnp.stack([conv_b[0], conv_ln_g[0], conv_ln_b[0]])
    gvec = jnp.stack([g_post_mix[0], g_pre_ffn[0]])
    x1, h2t = _mixer_tail(attn.reshape(t, N_HEADS * D_V), rest, x2d, mod_tail,
                          w_attn_out[0].astype(BF16), w_conv_out[0].astype(BF16),
                          w_out[0].astype(BF16), conv_w[0, :, 0, :], cvec, gvec, seq)

    wqt = peer_wq[0].T.astype(BF16)
    sk = peer_subkeys[0].reshape(2 * PEER_HEADS, PEER_NKEYS, PEER_DQ // 2).astype(BF16)
    f, cnt, g, rb = _route(h2t, wqt, sk)
    n_exp = peer_v.shape[1]
    vt = peer_v[0].reshape(n_exp // EXP_TE, EXP_TE, d).transpose(0, 2, 1).astype(BF16)
    out = _experts(h2t, peer_u[0].astype(BF16), vt, f, cnt, g, rb, x1, mod_out, g_post_ffn, seq)
    return out.reshape(bsz, seq, d)
```

```python
import functools
import math

import numpy as np
import jax
import jax.numpy as jnp
from jax import lax
from jax.experimental import pallas as pl
from jax.experimental.pallas import tpu as pltpu

F32 = jnp.float32
BF16 = jnp.bfloat16

D_MODEL = 1024
N_HEADS = 16
N_GROUPS = 4
HEADS_PER_GROUP = N_HEADS // N_GROUPS
D_QK = 96
D_V = 64
CMP_BLOCK = 32
CMP_STRIDE = 16
CMP_HIDDEN = 256
SLC_BLOCK = 64
SLC_TOPN = 16
WINDOW = 512
ATTN_SCALE = D_QK ** -0.5
FORCE_BONUS = 1e6
N_BUCKETS = 32
MAX_DISTANCE = 128
CONV_WIDTH = 31
PEER_HEADS = 8
PEER_NKEYS = 128
PEER_DQ = 256
PEER_TOPK = 16
NORM_EPS = 1e-6
NEG_INF = -1e30
LOG2E = math.log2(math.e)

LANE = 128
SUBLANE = 8
BF16_ROWS = 16
VMEM_LIMIT = 56 * 1024 * 1024

TQ = 256
SLC_CHUNK = 256
WIN_SPAN = TQ + WINDOW
PROJ_TM = 2048
PROJ_TN_HEADS = 1024
PROJ_TN_REST = 768
TAIL_TM = 256
HALO = 32
ROUTE_TL = 256
EXP_TL = 512
EXP_TE = 2048

QKV_Q0, QKV_KC0, QKV_KS0, QKV_KW0, QKV_VC0, QKV_VS0, QKV_VW0 = 0, 16, 20, 24, 28, 32, 36
QKV_HEADS = 40
REST_GLU_A, REST_GLU_G, REST_MRG_A, REST_MRG_B = 0, 1, 2, 3
REST_GATE0 = 4 * D_MODEL
REST_COLS = 4 * D_MODEL + N_GROUPS * LANE

NT_DIMS = (((1,), (1,)), ((), ()))


def _cparams(*sem):
    return pltpu.CompilerParams(dimension_semantics=sem, vmem_limit_bytes=VMEM_LIMIT)


def _gelu(x):
    return 0.5 * x * (1.0 + jnp.tanh(math.sqrt(2.0 / math.pi) * (x + 0.044715 * (x * x * x))))


def _gelu_x2(x):
    k1 = math.sqrt(2.0 / math.pi)
    return x * (1.0 + jnp.tanh(x * (k1 + (k1 * 0.044715) * (x * x))))


def _sigmoid(x):
    return 1.0 / (1.0 + jnp.exp(-x))


def _rms(x, g):
    return x * lax.rsqrt(jnp.mean(x * x, axis=-1, keepdims=True) + NORM_EPS) * g


def _ada_kernel(c_ref, w_ref, b_ref, o_ref):
    c = c_ref[...]
    c_act = c * _sigmoid(c)
    o_ref[...] = jnp.dot(c_act, w_ref[...], preferred_element_type=F32,
                         precision=lax.Precision.HIGHEST) + b_ref[...]


def _ada(c, w, b):
    bsz, d = c.shape
    n = w.shape[1]
    tn = 1024
    return pl.pallas_call(
        _ada_kernel,
        grid=(n // tn,),
        in_specs=[pl.BlockSpec((bsz, d), lambda j: (0, 0)),
                  pl.BlockSpec((d, tn), lambda j: (0, j)),
                  pl.BlockSpec((1, tn), lambda j: (0, j))],
        out_specs=pl.BlockSpec((bsz, tn), lambda j: (0, j)),
        out_shape=jax.ShapeDtypeStruct((bsz, n), F32),
        compiler_params=_cparams("parallel"),
        name="ada",
    )(c, w, b.reshape(1, n))


def _t5_bucket_np(dist):
    n = np.maximum(dist, 0)
    max_exact = N_BUCKETS // 2
    nf = np.maximum(n, 1).astype(np.float64)
    large = max_exact + (np.log(nf / max_exact) / math.log(MAX_DISTANCE / max_exact)
                         * (N_BUCKETS - max_exact)).astype(np.int64)
    return np.where(n < max_exact, n, np.minimum(large, N_BUCKETS - 1)).astype(np.int32)


def _bias_kernel(tab_ref, bkt_ref, o_ref):
    h = pl.program_id(0)
    bkt = bkt_ref[...]
    acc = jnp.where(bkt < 0, NEG_INF, 0.0).astype(F32)
    for i in range(N_BUCKETS):
        acc = jnp.where(bkt == i, tab_ref[i, h] * LOG2E, acc)
    o_ref[0] = acc


def _bias_table(rel_table, bucket):
    rows, cols = bucket.shape
    return pl.pallas_call(
        _bias_kernel,
        grid=(N_HEADS,),
        in_specs=[pl.BlockSpec(memory_space=pltpu.SMEM),
                  pl.BlockSpec((rows, cols), lambda h: (0, 0))],
        out_specs=pl.BlockSpec((1, rows, cols), lambda h: (h, 0, 0)),
        out_shape=jax.ShapeDtypeStruct((N_HEADS, rows, cols), F32),
        compiler_params=_cparams("parallel"),
        name="bias_table",
    )(rel_table, jnp.asarray(bucket))


def _static_buckets(seq):
    t = np.arange(seq)[:, None]
    n = np.arange(LANE)[None, :]
    dist_c = t - (n * CMP_STRIDE + CMP_BLOCK - 1)
    bkt_c = np.where(dist_c >= 0, _t5_bucket_np(dist_c), -1).astype(np.int32)
    qi = np.arange(TQ)[:, None]
    kj = np.arange(TQ)[None, :]
    tiles = [np.where(d * TQ + qi - kj >= 0, _t5_bucket_np(d * TQ + qi - kj), -1) for d in range(3)]
    bkt_t = np.concatenate(tiles, axis=0).astype(np.int32)
    kw = np.arange(WIN_SPAN)[None, :]
    dist_w = WINDOW + qi - kw
    band = (dist_w >= 0) & (dist_w < WINDOW)
    bkt_w = np.where(band, _t5_bucket_np(dist_w), -1).astype(np.int32)
    return bkt_c, bkt_t, bkt_w


def _proj_kernel(x_ref, mod_ref, g_ref, w_ref, add_ref, o_ref, h_ref, *, heads_out, tn):
    @pl.when(pl.program_id(1) == 0)
    def _():
        x = x_ref[...]
        sh = mod_ref[0, 0:1, :]
        sc = mod_ref[0, 1:2, :]
        h_ref[...] = (_rms(x, g_ref[...]) * (1.0 + sc) + sh).astype(BF16)

    res = (jnp.dot(h_ref[...], w_ref[...], preferred_element_type=F32) + add_ref[...]).astype(o_ref.dtype)
    if heads_out:
        for k in range(tn // LANE):
            o_ref[0, k] = res[:, k * LANE:(k + 1) * LANE]
    else:
        o_ref[...] = res


def _project(x2d, mod3, g, w, add, seq, heads_out):
    t, d = x2d.shape
    n = w.shape[1]
    bsz = t // seq
    tiles_per_seq = seq // PROJ_TM
    tn = PROJ_TN_HEADS if heads_out else PROJ_TN_REST
    assert n % tn == 0
    grid = (t // PROJ_TM, n // tn)
    if heads_out:
        hpt = tn // LANE
        out_shape = jax.ShapeDtypeStruct((bsz, n // LANE, seq, LANE), BF16)
        out_spec = pl.BlockSpec((1, hpt, PROJ_TM, LANE),
                                lambda i, j: (i // tiles_per_seq, j, i % tiles_per_seq, 0))
    else:
        out_shape = jax.ShapeDtypeStruct((t, n), BF16)
        out_spec = pl.BlockSpec((PROJ_TM, tn), lambda i, j: (i, j))
    return pl.pallas_call(
        functools.partial(_proj_kernel, heads_out=heads_out, tn=tn),
        grid=grid,
        in_specs=[pl.BlockSpec((PROJ_TM, d), lambda i, j: (i, 0)),
                  pl.BlockSpec((1, 2, d), lambda i, j: (i // tiles_per_seq, 0, 0)),
                  pl.BlockSpec((1, d), lambda i, j: (0, 0)),
                  pl.BlockSpec((d, tn), lambda i, j: (0, j)),
                  pl.BlockSpec((1, tn), lambda i, j: (0, j))],
        out_specs=out_spec,
        out_shape=out_shape,
        scratch_shapes=[pltpu.VMEM((PROJ_TM, d), BF16)],
        compiler_params=_cparams("parallel", "arbitrary"),
        name="proj_heads" if heads_out else "proj_rest",
    )(x2d, mod3, g, w, add)


def _compress_kernel(kch_ref, vch_ref, pk_ref, pv_ref, w1k_ref, w2k_ref, w1v_ref, w2v_ref,
                     kc_ref, vc_ref):
    def one(ch_ref, pos_ref, w1_ref, w2_ref, o_ref):
        a = ch_ref[0, 0].astype(F32)
        a1 = (a + pos_ref[0:1, :]).astype(BF16)
        a2 = (a + pos_ref[1:2, :]).astype(BF16)
        p1 = jnp.dot(a1, w1_ref[0], preferred_element_type=F32)
        p2 = jnp.dot(a2, w1_ref[1], preferred_element_type=F32)
        n_chunk = p2.shape[0]
        hid = _gelu(p1 + pltpu.roll(p2, n_chunk - 1, 0))
        o_ref[0, 0] = jnp.dot(hid.astype(BF16), w2_ref[...],
                              preferred_element_type=F32).astype(BF16)

    one(kch_ref, pk_ref, w1k_ref, w2k_ref, kc_ref)
    one(vch_ref, pv_ref, w1v_ref, w2v_ref, vc_ref)


def _compress(kch, vch, pk, pv, w1k, w2k, w1v, w2v):
    bsz, g, n_chunk, width = kch.shape
    assert n_chunk == LANE, "compressed keys are laid out on one 128-row tile"
    spec_in = pl.BlockSpec((1, 1, n_chunk, width), lambda b, gg: (b, gg, 0, 0))
    spec_out = pl.BlockSpec((1, 1, n_chunk, LANE), lambda b, gg: (b, gg, 0, 0))
    full = lambda a: pl.BlockSpec(a.shape, lambda b, gg: (0,) * a.ndim)
    out = jax.ShapeDtypeStruct((bsz, g, n_chunk, LANE), BF16)
    return pl.pallas_call(
        _compress_kernel,
        grid=(bsz, g),
        in_specs=[spec_in, spec_in, full(pk), full(pv), full(w1k), full(w2k), full(w1v), full(w2v)],
        out_specs=[spec_out, spec_out],
        out_shape=[out, out],
        compiler_params=_cparams("parallel", "parallel"),
        name="compress",
    )(kch, vch, pk, pv, w1k, w2k, w1v, w2v)


def _attn_kernel(q_ref, kc_ref, vc_ref, ks_ref, vs_ref, kw_ref, vw_ref, gate_ref,
                 bias_c_ref, bias_t_ref, bias_w_ref, overlap_ref, expand_ref,
                 o_ref, mask_ref):
    qi = pl.program_id(2)
    rows = HEADS_PER_GROUP * TQ
    seq = mask_ref.shape[1]
    q4 = q_ref[0].reshape(rows, LANE)

    def per_head(x):
        return jnp.broadcast_to(x[None], (HEADS_PER_GROUP,) + x.shape).reshape(rows, x.shape[-1])

    lc = lax.dot_general(q4, kc_ref[0, 0], NT_DIMS, preferred_element_type=F32)
    lc = lc + bias_c_ref[...].reshape(rows, LANE)
    valid = lc > 0.5 * NEG_INF
    mc = jnp.max(lc, axis=-1, keepdims=True)
    ec = jnp.where(valid, jnp.exp2(lc - mc), 0.0)
    sc = jnp.sum(ec, axis=-1, keepdims=True)
    pc = ec / jnp.where(sc > 0.0, sc, 1.0)
    o_cmp = jnp.dot(pc.astype(BF16), vc_ref[0, 0], preferred_element_type=F32)

    psum = jnp.sum(pc.reshape(HEADS_PER_GROUP, TQ, LANE), axis=0)
    imp = jnp.dot(psum, overlap_ref[...], preferred_element_type=F32,
                  precision=lax.Precision.HIGHEST)
    t_pos = qi * TQ + lax.broadcasted_iota(jnp.int32, (TQ, LANE), 0)
    j_blk = lax.broadcasted_iota(jnp.int32, (TQ, LANE), 1)
    cur = t_pos // SLC_BLOCK
    forced = (j_blk == 0) | (j_blk == cur) | (j_blk == cur - 1)
    score = jnp.where(j_blk <= cur, imp + jnp.where(forced, FORCE_BONUS, 0.0), NEG_INF)
    n_slc = seq // SLC_BLOCK
    sc_t = score.T[:n_slc]
    j_row = lax.broadcasted_iota(jnp.int32, (n_slc, TQ), 0)
    rank = jnp.zeros((n_slc, TQ), F32)
    for jp in range(n_slc):
        row = sc_t[jp:jp + 1, :]
        ge = jnp.where(row >= sc_t, 1.0, 0.0)
        gt = jnp.where(row > sc_t, 1.0, 0.0)
        rank = rank + jnp.where(j_row > jp, ge, gt)
    sel_t = jnp.where(rank < float(min(SLC_TOPN, n_slc)), 1.0, 0.0)
    sel_t = jnp.concatenate([sel_t, jnp.zeros((LANE - n_slc, TQ), F32)], axis=0)
    sel = ((sel_t.T - 1.0) * (-NEG_INF)).astype(BF16)
    mask_ref[...] = jnp.dot(sel, expand_ref[...], preferred_element_type=F32)

    tiles_per_chunk = SLC_CHUNK // TQ

    def slc_step(c, carry):
        m, acc = carry
        k0 = pl.multiple_of(c * SLC_CHUNK, SLC_CHUNK)
        kt = ks_ref[0, 0, pl.ds(k0, SLC_CHUNK), :]
        vt = vs_ref[0, 0, pl.ds(k0, SLC_CHUNK), :]
        s = lax.dot_general(q4, kt, NT_DIMS, preferred_element_type=F32)
        bias = jnp.concatenate(
            [bias_t_ref[jnp.clip(qi - (c * tiles_per_chunk + k), 0, 2)]
             for k in range(tiles_per_chunk)], axis=-1)
        s = s + bias.reshape(rows, SLC_CHUNK) + per_head(mask_ref[:, pl.ds(k0, SLC_CHUNK)])
        m_new = jnp.maximum(m, jnp.max(s, axis=-1, keepdims=True))
        p = jnp.exp2(s - m_new)
        acc = jnp.exp2(m - m_new) * acc + jnp.dot(p.astype(BF16), vt, preferred_element_type=F32)
        return m_new, acc

    n_chunks = (qi + tiles_per_chunk) // tiles_per_chunk
    init = (jnp.full((rows, 1), -jnp.inf, F32), jnp.zeros((rows, LANE), F32))
    _, acc_s = lax.fori_loop(0, n_chunks, slc_step, init)
    o_slc = acc_s / pltpu.roll(acc_s, LANE - D_V, 1)

    w0 = pl.multiple_of(qi * TQ, TQ)
    kwin = kw_ref[0, 0, pl.ds(w0, WIN_SPAN), :]
    vwin = vw_ref[0, 0, pl.ds(w0, WIN_SPAN), :]
    sw = lax.dot_general(q4, kwin, NT_DIMS, preferred_element_type=F32)
    sw = sw + bias_w_ref[...].reshape(rows, WIN_SPAN)
    kj = lax.broadcasted_iota(jnp.int32, (1, WIN_SPAN), 1)
    sw = sw + jnp.where(kj >= WINDOW - qi * TQ, 0.0, NEG_INF)
    mw = jnp.max(sw, axis=-1, keepdims=True)
    pw = jnp.exp2(sw - mw)
    o_win = jnp.dot(pw.astype(BF16), vwin, preferred_element_type=F32)
    o_win = o_win / pltpu.roll(o_win, LANE - D_V, 1)

    gates = _sigmoid(gate_ref[...].astype(F32))
    outs = []
    for r in range(HEADS_PER_GROUP):
        sl = slice(r * TQ, (r + 1) * TQ)
        g_c = gates[:, r:r + 1]
        g_s = gates[:, HEADS_PER_GROUP + r:HEADS_PER_GROUP + r + 1]
        g_w = gates[:, 2 * HEADS_PER_GROUP + r:2 * HEADS_PER_GROUP + r + 1]
        o = g_c * o_cmp[sl] + g_s * o_slc[sl] + g_w * o_win[sl]
        outs.append(o[:, :D_V])
    o_ref[0] = jnp.concatenate(outs, axis=-1).astype(o_ref.dtype)


def _attention(qkv, kwp, vwp, kc, vc, rest, bias_c, bias_t, bias_w, overlap, expand):
    bsz, _, seq, _ = qkv.shape
    nq = seq // TQ
    g_cols0 = REST_GATE0 // LANE
    head = lambda base: pl.BlockSpec((1, 1, seq, LANE), lambda b, g, i: (b, base + g, 0, 0))
    in_specs = [
        pl.BlockSpec((1, HEADS_PER_GROUP, TQ, LANE), lambda b, g, i: (b, g, i, 0)),
        pl.BlockSpec((1, 1, LANE, LANE), lambda b, g, i: (b, g, 0, 0)),
        pl.BlockSpec((1, 1, LANE, LANE), lambda b, g, i: (b, g, 0, 0)),
        head(QKV_KS0), head(QKV_VS0),
        pl.BlockSpec((1, 1, seq + WINDOW, LANE), lambda b, g, i: (b, g, 0, 0)),
        pl.BlockSpec((1, 1, seq + WINDOW, LANE), lambda b, g, i: (b, g, 0, 0)),
        pl.BlockSpec((TQ, LANE), lambda b, g, i: (b * nq + i, g_cols0 + g)),
        pl.BlockSpec((HEADS_PER_GROUP, TQ, LANE), lambda b, g, i: (g, i, 0)),
        pl.BlockSpec((3, HEADS_PER_GROUP, TQ, TQ), lambda b, g, i: (0, g, 0, 0)),
        pl.BlockSpec((HEADS_PER_GROUP, TQ, WIN_SPAN), lambda b, g, i: (g, 0, 0)),
        pl.BlockSpec((LANE, LANE), lambda b, g, i: (0, 0)),
        pl.BlockSpec((LANE, seq), lambda b, g, i: (0, 0)),
    ]
    return pl.pallas_call(
        _attn_kernel,
        grid=(bsz, N_GROUPS, nq),
        in_specs=in_specs,
        out_specs=pl.BlockSpec((1, TQ, HEADS_PER_GROUP * D_V), lambda b, g, i: (b, i, g)),
        out_shape=jax.ShapeDtypeStruct((bsz, seq, N_HEADS * D_V), BF16),
        scratch_shapes=[pltpu.VMEM((TQ, seq), F32)],
        compiler_params=_cparams("parallel", "parallel", "arbitrary"),
        name="attention",
    )(qkv, kc, vc, qkv, qkv, kwp, vwp, rest, bias_c, bias_t, bias_w, overlap, expand)


def _tail_kernel(attn_ref, ga_ref, gg_ref, ha_ref, hg_ref, ma_ref, mb_ref, x_ref, mod_ref,
                 wa_ref, wc_ref, wo_ref, cw_ref, cvec_ref, gvec_ref,
                 x1_ref, h2t_ref, u_ref, ush_ref, *, tiles_per_seq):
    first = (pl.program_id(0) % tiles_per_seq) == 0
    y_a = jnp.dot(attn_ref[...], wa_ref[...], preferred_element_type=F32)

    halo = ha_ref[...].astype(F32) * _sigmoid(hg_ref[...].astype(F32))
    u_ref[0:HALO, :] = jnp.where(first, 0.0, halo)
    u_ref[HALO:, :] = ga_ref[...].astype(F32) * _sigmoid(gg_ref[...].astype(F32))

    conv_b, ln_g, ln_b = cvec_ref[0:1, :], cvec_ref[1:2, :], cvec_ref[2:3, :]
    n_sh = HALO + TAIL_TM - SUBLANE
    for r in range(1, SUBLANE):
        ush_ref[r - 1, 0:n_sh, :] = u_ref[r:r + n_sh, :]
    blk = 32
    pieces = []
    for rb in range(TAIL_TM // blk):
        acc = jnp.zeros((blk, D_MODEL), F32)
        for j in range(CONV_WIDTH):
            start = HALO - (CONV_WIDTH - 1) + rb * blk + j
            r, a = start % SUBLANE, start - start % SUBLANE
            win = u_ref[a:a + blk, :] if r == 0 else ush_ref[r - 1, a:a + blk, :]
            acc = acc + cw_ref[j:j + 1, :] * win
        pieces.append(acc)
    y = jnp.concatenate(pieces, axis=0) + conv_b
    yc = y - jnp.mean(y, axis=-1, keepdims=True)
    yn = yc * lax.rsqrt(jnp.mean(yc * yc, axis=-1, keepdims=True) + NORM_EPS) * ln_g + ln_b
    act = yn * _sigmoid(yn)
    y_b = jnp.dot(act.astype(BF16), wc_ref[...], preferred_element_type=F32)

    merged = (_sigmoid(ma_ref[...].astype(F32)) * y_a + _sigmoid(mb_ref[...].astype(F32)) * y_b)
    out = jnp.dot(merged.astype(BF16), wo_ref[...], preferred_element_type=F32)

    gt1, sh2, sc2 = mod_ref[0, 0:1, :], mod_ref[0, 1:2, :], mod_ref[0, 2:3, :]
    x1 = x_ref[...] + gt1 * _rms(out, gvec_ref[0:1, :])
    x1_ref[...] = x1
    h2 = _rms(x1, gvec_ref[1:2, :]) * (1.0 + sc2) + sh2
    h2t_ref[...] = h2.T.astype(BF16)


def _mixer_tail(attn2d, rest, x2d, mod3, wa, wc, wo, cw, cvec, gvec, seq):
    t, d = x2d.shape
    tiles_per_seq = seq // TAIL_TM
    hpt = TAIL_TM // HALO
    row = lambda cb: pl.BlockSpec((TAIL_TM, d), lambda i: (i, cb))
    halo = lambda cb: pl.BlockSpec((HALO, d), lambda i: (jnp.maximum(i * hpt - 1, 0), cb))
    full = lambda a: pl.BlockSpec(a.shape, lambda i: (0,) * a.ndim)
    return pl.pallas_call(
        functools.partial(_tail_kernel, tiles_per_seq=tiles_per_seq),
        grid=(t // TAIL_TM,),
        in_specs=[row(0), row(REST_GLU_A), row(REST_GLU_G), halo(REST_GLU_A), halo(REST_GLU_G),
                  row(REST_MRG_A), row(REST_MRG_B), row(0),
                  pl.BlockSpec((1, 3, d), lambda i: (i // tiles_per_seq, 0, 0)),
                  full(wa), full(wc), full(wo), full(cw), full(cvec), full(gvec)],
        out_specs=[row(0), pl.BlockSpec((d, TAIL_TM), lambda i: (0, i))],
        out_shape=[jax.ShapeDtypeStruct((t, d), F32), jax.ShapeDtypeStruct((d, t), BF16)],
        scratch_shapes=[pltpu.VMEM((HALO + TAIL_TM, d), F32),
                        pltpu.VMEM((SUBLANE - 1, HALO + TAIL_TM, d), F32)],
        compiler_params=_cparams("parallel"),
        name="mixer_tail",
    )(attn2d, rest, rest, rest, rest, rest, rest, x2d, mod3, wa, wc, wo, cw, cvec, gvec)


def _cand_pairs():
    return [(i, j) for i in range(PEER_TOPK) for j in range(PEER_TOPK)
            if (i + 1) * (j + 1) <= PEER_TOPK]


def _top16_rows(s):
    n = s.shape[0]
    kio = lax.broadcasted_iota(jnp.int32, s.shape, 0).astype(F32)
    vals, idxs = [], []
    for _ in range(PEER_TOPK):
        m = jnp.max(s, axis=0, keepdims=True)
        idx = jnp.min(jnp.where(s == m, kio, float(n)), axis=0, keepdims=True)
        vals.append(m)
        idxs.append(idx)
        s = jnp.where(kio == idx, -jnp.inf, s)
    return vals, idxs


def _sort16_network():
    def merge(lo, hi, r):
        step = r * 2
        if step < hi - lo:
            yield from merge(lo, hi, step)
            yield from merge(lo + r, hi, step)
            yield from [(i, i + r) for i in range(lo + r, hi - r, step)]
        else:
            yield (lo, lo + r)

    def sort(lo, hi):
        if hi - lo >= 1:
            mid = lo + (hi - lo) // 2
            yield from sort(lo, mid)
            yield from sort(mid + 1, hi)
            yield from merge(lo, hi, 1)

    return list(sort(0, PEER_TOPK - 1))


def _top16_values(s):
    n_grp = s.shape[0] // SUBLANE
    assert n_grp == PEER_TOPK
    lst = [s[SUBLANE * v:SUBLANE * (v + 1), :] for v in range(n_grp)]
    for a, b in _sort16_network():
        lst[a], lst[b] = jnp.maximum(lst[a], lst[b]), jnp.minimum(lst[a], lst[b])
    for shift in (4, 2, 1):
        other = [pltpu.roll(x, shift, 0) for x in lst]
        lst = [jnp.maximum(lst[j], other[n_grp - 1 - j]) for j in range(n_grp)]
        dist = n_grp // 2
        while dist >= 1:
            for j in range(n_grp):
                if (j // dist) % 2 == 0:
                    lst[j], lst[j + dist] = (jnp.maximum(lst[j], lst[j + dist]),
                                             jnp.minimum(lst[j], lst[j + dist]))
            dist //= 2
    return [x[0:1, :] for x in lst]


def _route_kernel(h2t_ref, wqt_ref, sk_ref, f_ref, cnt_ref, g_ref, rb_ref):
    q2t = jnp.dot(wqt_ref[...], h2t_ref[...], preferred_element_type=F32)
    tl = q2t.shape[1]
    kio = lax.broadcasted_iota(jnp.int32, (PEER_NKEYS, tl), 0).astype(F32)
    pairs = _cand_pairs()
    n_pad = -len(pairs) % 8
    flat_ids = np.array([i * PEER_TOPK + j for i, j in pairs] + [PEER_TOPK ** 2] * n_pad, np.float32)
    group_start = [min(k for k, (i, _) in enumerate(pairs) if i == ii) for ii in range(PEER_TOPK)]
    group_len = [sum(1 for (i, _) in pairs if i == ii) for ii in range(PEER_TOPK)]
    n_rows = len(pairs) + n_pad
    flat_col = lax.broadcasted_iota(jnp.int32, (n_rows, tl), 0)
    flat = jnp.zeros((n_rows, tl), F32)
    for k in range(n_rows):
        flat = jnp.where(flat_col == k, float(flat_ids[k]), flat)

    def head_scores(h):
        out = []
        for c in range(2):
            hc = 2 * h + c
            qt = q2t[hc * LANE:(hc + 1) * LANE, :].astype(BF16)
            out.append(jnp.dot(sk_ref[hc], qt, preferred_element_type=F32))
        return out

    def write_tables(h, scores, v1, v2, is_rank1, is_rank2):
        cand0 = jnp.concatenate([v1[i] + v2[j] for i, j in pairs]
                                + [jnp.full((n_pad, tl), -jnp.inf, F32)], axis=0)
        cand = cand0
        for _ in range(PEER_TOPK):
            m = jnp.max(cand, axis=0, keepdims=True)
            fid = jnp.min(jnp.where(cand == m, flat, float(PEER_TOPK ** 2 + 1)), axis=0, keepdims=True)
            cand = jnp.where(flat == fid, -jnp.inf, cand)
        picked = jnp.where((cand == -jnp.inf) & (flat < float(PEER_TOPK ** 2)), 1.0, 0.0)
        top = v1[0] + v2[0]
        z = jnp.sum(picked * jnp.exp(cand0 - top), axis=0, keepdims=True)
        counts = [jnp.sum(picked[group_start[i]:group_start[i] + group_len[i]], axis=0, keepdims=True)
                  for i in range(PEER_TOPK)]
        cnt = jnp.zeros((PEER_NKEYS, tl), F32)
        rank_b = jnp.full((PEER_NKEYS, tl), float(PEER_NKEYS), F32)
        for i in range(PEER_TOPK):
            cnt = jnp.where(is_rank1(i), counts[i], cnt)
            rank_b = jnp.where(is_rank2(i), float(i), rank_b)
        outs = ((f_ref, jnp.exp(scores[0] - v1[0]) * (0.5 / z)),
                (g_ref, jnp.exp(scores[1] - v2[0])), (cnt_ref, cnt), (rb_ref, rank_b))
        for ref, val in outs:
            for ch in range(tl // LANE):
                ref[h, ch] = val[:, ch * LANE:(ch + 1) * LANE].astype(ref.dtype)

    tied = jnp.zeros((1, tl), F32)
    for h in range(PEER_HEADS):
        scores = head_scores(h)
        tops = [_top16_values(sc) for sc in scores]
        for sc, v in zip(scores, tops):
            n_ge = jnp.sum(jnp.where(sc >= v[PEER_TOPK - 1], 1.0, 0.0), axis=0, keepdims=True)
            tied = jnp.maximum(tied, jnp.where(n_ge == float(PEER_TOPK), 0.0, 1.0))
            for i in range(PEER_TOPK - 1):
                tied = jnp.maximum(tied, jnp.where(v[i] > v[i + 1], 0.0, 1.0))
        write_tables(h, scores, tops[0], tops[1],
                     lambda i, sc=scores[0], v=tops[0]: sc == v[i],
                     lambda i, sc=scores[1], v=tops[1]: sc == v[i])

    @pl.when(jnp.max(tied) > 0.0)
    def _():
        for h in range(PEER_HEADS):
            scores = head_scores(h)
            (v1, i1), (v2, i2) = [_top16_rows(sc) for sc in scores]
            write_tables(h, scores, v1, v2, lambda i, i1=i1: kio == i1[i], lambda i, i2=i2: kio == i2[i])


def _route(h2t, wqt, sk):
    d, t = h2t.shape
    spec = pl.BlockSpec((PEER_HEADS, ROUTE_TL // LANE, PEER_NKEYS, LANE), lambda i: (0, i, 0, 0))
    out = jax.ShapeDtypeStruct((PEER_HEADS, t // LANE, PEER_NKEYS, LANE), F32)
    out16 = jax.ShapeDtypeStruct((PEER_HEADS, t // LANE, PEER_NKEYS, LANE), BF16)
    return pl.pallas_call(
        _route_kernel,
        grid=(t // ROUTE_TL,),
        in_specs=[pl.BlockSpec((d, ROUTE_TL), lambda i: (0, i)),
                  pl.BlockSpec(wqt.shape, lambda i: (0, 0)),
                  pl.BlockSpec(sk.shape, lambda i: (0, 0, 0))],
        out_specs=[spec, spec, spec, spec],
        out_shape=[out, out, out16, out16],
        compiler_params=_cparams("parallel"),
        name="peer_route",
    )(h2t, wqt, sk)


def _expert_kernel(h2t_ref, u_ref, vt_ref, f_ref, cnt_ref, g_ref, rb_ref, x1_ref, mod_ref, gp_ref,
                   o_ref, acc_ref, act_ref, coef_ref):
    e = pl.program_id(1)
    n_chunks = EXP_TL // LANE
    a_per_step = EXP_TE // PEER_NKEYS
    a_group = 2

    @pl.when(e == 0)
    def _():
        acc_ref[...] = jnp.zeros_like(acc_ref)

    act = jnp.dot(u_ref[...], h2t_ref[...], preferred_element_type=F32)
    for c in range(n_chunks):
        act_ref[c] = act[:, c * LANE:(c + 1) * LANE]

    def lane_chunk(c, carry):
        for ag in range(a_per_step // a_group):
            coefs = [jnp.zeros((PEER_NKEYS, LANE), BF16) for _ in range(a_group)]

            def row_tile(ref, h, a):
                row = jnp.broadcast_to(ref[h, c, pl.ds(a, 1), :], (BF16_ROWS, LANE)).astype(BF16)
                return jnp.broadcast_to(row[None], (PEER_NKEYS // BF16_ROWS, BF16_ROWS, LANE)
                                        ).reshape(PEER_NKEYS, LANE)

            for h in range(PEER_HEADS):
                rank_b = rb_ref[h, c]
                g_b = g_ref[h, c]
                for k in range(a_group):
                    a = e * a_per_step + ag * a_group + k
                    picked = jnp.maximum(jnp.minimum(row_tile(cnt_ref, h, a) - rank_b, g_b), 0.0)
                    coefs[k] = coefs[k] + row_tile(f_ref, h, a) * picked
            for k in range(a_group):
                rows = pl.ds((ag * a_group + k) * PEER_NKEYS, PEER_NKEYS)
                coef_ref[c, rows, :] = _gelu_x2(act_ref[c, rows, :]).astype(BF16) * coefs[k]
        return carry

    lax.fori_loop(0, n_chunks, lane_chunk, 0)
    coef = jnp.concatenate([coef_ref[c] for c in range(n_chunks)], axis=1)
    acc_ref[...] += jnp.dot(vt_ref[0], coef, preferred_element_type=F32)

    @pl.when(e == pl.num_programs(1) - 1)
    def _():
        y = acc_ref[...].T
        o_ref[...] = x1_ref[...] + mod_ref[0] * _rms(y, gp_ref[...])


def _experts(h2t, u, vt, f, cnt, g, rb, x1, gt2, gpost, seq):
    d, t = h2t.shape
    n_exp = u.shape[0]
    tiles_per_seq = seq // EXP_TL
    n_chunks = EXP_TL // LANE
    route = pl.BlockSpec((PEER_HEADS, n_chunks, PEER_NKEYS, LANE), lambda i, e: (0, i, 0, 0))
    return pl.pallas_call(
        _expert_kernel,
        grid=(t // EXP_TL, n_exp // EXP_TE),
        in_specs=[pl.BlockSpec((d, EXP_TL), lambda i, e: (0, i)),
                  pl.BlockSpec((EXP_TE, d), lambda i, e: (e, 0)),
                  pl.BlockSpec((1, d, EXP_TE), lambda i, e: (e, 0, 0)),
                  route, route, route, route,
                  pl.BlockSpec((EXP_TL, d), lambda i, e: (i, 0)),
                  pl.BlockSpec((1, 1, d), lambda i, e: (i // tiles_per_seq, 0, 0)),
                  pl.BlockSpec((1, d), lambda i, e: (0, 0))],
        out_specs=pl.BlockSpec((EXP_TL, d), lambda i, e: (i, 0)),
        out_shape=jax.ShapeDtypeStruct((t, d), F32),
        scratch_shapes=[pltpu.VMEM((d, EXP_TL), F32), pltpu.VMEM((n_chunks, EXP_TE, LANE), F32),
                        pltpu.VMEM((n_chunks, EXP_TE, LANE), BF16)],
        compiler_params=_cparams("parallel", "arbitrary"),
        name="peer_experts",
    )(h2t, u, vt, f, cnt, g, rb, x1, gt2, gpost)


def _pad_heads(w, n, width):
    d = w.shape[0]
    return jnp.pad(w.reshape(d, n, width), ((0, 0), (0, 0), (0, LANE - width))).reshape(d, n * LANE)


def _split_w_in(w_in):
    q_cols = N_HEADS * D_QK
    k_cols = N_GROUPS * D_QK
    v_cols = N_GROUPS * D_V
    sizes = (q_cols, k_cols, k_cols, k_cols, v_cols, v_cols, v_cols, 3 * N_HEADS, 2 * D_MODEL, 2 * D_MODEL)
    offs = np.cumsum((0,) + sizes)
    parts = [w_in[:, offs[k]:offs[k + 1]] for k in range(len(sizes))]
    wq, wkc, wks, wkw, wvc, wvs, wvw, wgate, wglu, wmerge = parts
    w_qkv = jnp.concatenate(
        [_pad_heads(wq * (ATTN_SCALE * LOG2E), N_HEADS, D_QK)]
        + [_pad_heads(w, N_GROUPS, D_QK) for w in (wkc, wks, wkw)]
        + [_pad_heads(w, N_GROUPS, D_V) for w in (wvc, wvs, wvw)], axis=1)
    d = w_in.shape[0]
    wg = wgate.reshape(d, 3, N_GROUPS, HEADS_PER_GROUP).transpose(0, 2, 1, 3)
    wg = wg.reshape(d, N_GROUPS, 3 * HEADS_PER_GROUP)
    wg = jnp.pad(wg, ((0, 0), (0, 0), (0, LANE - 3 * HEADS_PER_GROUP))).reshape(d, N_GROUPS * LANE)
    w_rest = jnp.concatenate([wglu, wmerge, wg], axis=1)
    return w_qkv.astype(BF16), w_rest.astype(BF16)


def _cmp_weights(w1, w2, pos, dh):
    hidden = w1.shape[1]
    w1p = jnp.pad(w1.reshape(CMP_BLOCK, dh, hidden), ((0, 0), (0, LANE - dh), (0, 0)))
    w1p = w1p.reshape(2, CMP_STRIDE * LANE, hidden).astype(BF16)
    w2p = jnp.pad(w2, ((0, 0), (0, LANE - dh))).astype(BF16)
    posp = jnp.pad(pos, ((0, 0), (0, LANE - dh))).reshape(2, CMP_STRIDE * LANE)
    return w1p, w2p, posp


def kernel(x, c, w_ada, b_ada, g_pre_mix, g_post_mix, g_pre_ffn, g_post_ffn, rel_table, w_in,
           cmp_w1k, cmp_w2k, cmp_pos_k, cmp_w1v, cmp_w2v, cmp_pos_v, w_attn_out,
           conv_w, conv_b, conv_ln_g, conv_ln_b, w_conv_out, w_out,
           peer_wq, peer_subkeys, peer_u, peer_v):
    bsz, seq, d = x.shape
    t = bsz * seq
    assert d == D_MODEL and w_ada.shape[0] == 1, "single-layer block with D_MODEL channels"
    assert seq % PROJ_TM == 0 and seq // CMP_STRIDE == LANE and seq % EXP_TL == 0
    x2d = x.reshape(t, d)

    mod = _ada(c, w_ada[0], b_ada[0]).reshape(bsz, 6, d)
    mod_in = mod[:, 0:2]
    mod_tail = mod[:, 2:5]
    mod_out = mod[:, 5:6]

    bkt_c, bkt_t, bkt_w = _static_buckets(seq)
    bias_c = _bias_table(rel_table, bkt_c)
    bias_t = _bias_table(rel_table, bkt_t).reshape(N_HEADS, 3, TQ, TQ).transpose(1, 0, 2, 3)
    bias_w = _bias_table(rel_table, bkt_w)

    w_qkv, w_rest = _split_w_in(w_in[0])
    add_qkv = np.zeros((QKV_HEADS, LANE), np.float32)
    add_qkv[QKV_VS0:QKV_VS0 + N_GROUPS, D_V:] = 1.0
    add_qkv[QKV_VW0:QKV_VW0 + N_GROUPS, D_V:] = 1.0
    qkv = _project(x2d, mod_in, g_pre_mix, w_qkv, jnp.asarray(add_qkv.reshape(1, -1)), seq,
                   heads_out=True)
    rest = _project(x2d, mod_in, g_pre_mix, w_rest, jnp.zeros((1, REST_COLS), F32), seq,
                    heads_out=False)

    n_chunk = seq // CMP_STRIDE
    kch = qkv[:, QKV_KC0:QKV_KC0 + N_GROUPS].reshape(bsz, N_GROUPS, n_chunk, CMP_STRIDE * LANE)
    vch = qkv[:, QKV_VC0:QKV_VC0 + N_GROUPS].reshape(bsz, N_GROUPS, n_chunk, CMP_STRIDE * LANE)
    w1k, w2k, pk = _cmp_weights(cmp_w1k[0], cmp_w2k[0], cmp_pos_k[0], D_QK)
    w1v, w2v, pv = _cmp_weights(cmp_w1v[0], cmp_w2v[0], cmp_pos_v[0], D_V)
    kc, vc = _compress(kch, vch, pk, pv, w1k, w2k, w1v, w2v)

    pad = ((0, 0), (0, 0), (WINDOW, 0), (0, 0))
    kwp = jnp.pad(qkv[:, QKV_KW0:QKV_KW0 + N_GROUPS], pad)
    vwp = jnp.pad(qkv[:, QKV_VW0:QKV_VW0 + N_GROUPS], pad)

    n_idx = np.arange(LANE)[:, None] * CMP_STRIDE
    j_idx = np.arange(LANE)[None, :] * SLC_BLOCK
    overlap = ((n_idx <= j_idx + SLC_BLOCK - 1) & (n_idx + CMP_BLOCK - 1 >= j_idx)
               & (np.arange(LANE)[None, :] < seq // SLC_BLOCK)
               & (np.arange(LANE)[:, None] < n_chunk - 1)).astype(np.float32)
    expand = (np.arange(LANE)[:, None] == np.arange(seq)[None, :] // SLC_BLOCK)
    attn = _attention(qkv, kwp, vwp, kc, vc, rest, bias_c, bias_t, bias_w,
                      jnp.asarray(overlap), jnp.asarray(expand, dtype=BF16))

    cvec = jnp.stack([conv_b[0], conv_ln_g[0], conv_ln_b[0]])
    gvec = jnp.stack([g_post_mix[0], g_pre_ffn[0]])
    x1, h2t = _mixer_tail(attn.reshape(t, N_HEADS * D_V), rest, x2d, mod_tail,
                          w_attn_out[0].astype(BF16), w_conv_out[0].astype(BF16),
                          w_out[0].astype(BF16), conv_w[0, :, 0, :], cvec, gvec, seq)

    wqt = peer_wq[0].T.astype(BF16)
    sk = peer_subkeys[0].reshape(2 * PEER_HEADS, PEER_NKEYS, PEER_DQ // 2).astype(BF16)
    f, cnt, g, rb = _route(h2t, wqt, sk)
    n_exp = peer_v.shape[1]
    vt = peer_v[0].reshape(n_exp // EXP_TE, EXP_TE, d).transpose(0, 2, 1).astype(BF16)
    out = _experts(h2t, peer_u[0].astype(BF16), vt, f, cnt, g, rb, x1, mod_out, g_post_ffn, seq)
    return out.reshape(bsz, seq, d)
```

```python
import functools
import math

import numpy as np
import jax
import jax.numpy as jnp
from jax import lax
from jax.experimental import pallas as pl
from jax.experimental.pallas import tpu as pltpu

F32 = jnp.float32
BF16 = jnp.bfloat16

D_MODEL = 1024
N_HEADS = 16
N_GROUPS = 4
HEADS_PER_GROUP = N_HEADS // N_GROUPS
D_QK = 96
D_V = 64
CMP_BLOCK = 32
CMP_STRIDE = 16
CMP_HIDDEN = 256
SLC_BLOCK = 64
SLC_TOPN = 16
WINDOW = 512
ATTN_SCALE = D_QK ** -0.5
FORCE_BONUS = 1e6
N_BUCKETS = 32
MAX_DISTANCE = 128
CONV_WIDTH = 31
PEER_HEADS = 8
PEER_NKEYS = 128
PEER_DQ = 256
PEER_TOPK = 16
NORM_EPS = 1e-6
NEG_INF = -1e30
LOG2E = math.log2(math.e)

LANE = 128
SUBLANE = 8
BF16_ROWS = 16
VMEM_LIMIT = 56 * 1024 * 1024

TQ = 256
SLC_CHUNK = 256
WIN_SPAN = TQ + WINDOW
PROJ_TM = 2048
PROJ_TN_HEADS = 1024
PROJ_TN_REST = 768
TAIL_TM = 256
HALO = 32
ROUTE_TL = 256
EXP_TL = 512
EXP_TE = 2048

QKV_Q0, QKV_KC0, QKV_KS0, QKV_KW0, QKV_VC0, QKV_VS0, QKV_VW0 = 0, 16, 20, 24, 28, 32, 36
QKV_HEADS = 40
REST_GLU_A, REST_GLU_G, REST_MRG_A, REST_MRG_B = 0, 1, 2, 3
REST_GATE0 = 4 * D_MODEL
REST_COLS = 4 * D_MODEL + N_GROUPS * LANE

NT_DIMS = (((1,), (1,)), ((), ()))


def _cparams(*sem):
    return pltpu.CompilerParams(dimension_semantics=sem, vmem_limit_bytes=VMEM_LIMIT)


def _gelu(x):
    return 0.5 * x * (1.0 + jnp.tanh(math.sqrt(2.0 / math.pi) * (x + 0.044715 * (x * x * x))))


def _gelu_x2(x):
    k1 = math.sqrt(2.0 / math.pi)
    return x * (1.0 + jnp.tanh(x * (k1 + (k1 * 0.044715) * (x * x))))


def _sigmoid(x):
    return 1.0 / (1.0 + jnp.exp(-x))


def _rms(x, g):
    return x * lax.rsqrt(jnp.mean(x * x, axis=-1, keepdims=True) + NORM_EPS) * g


def _ada_kernel(c_ref, w_ref, b_ref, o_ref):
    c = c_ref[...]
    c_act = c * _sigmoid(c)
    o_ref[...] = jnp.dot(c_act, w_ref[...], preferred_element_type=F32,
                         precision=lax.Precision.HIGHEST) + b_ref[...]


def _ada(c, w, b):
    bsz, d = c.shape
    n = w.shape[1]
    tn = 1024
    return pl.pallas_call(
        _ada_kernel,
        grid=(n // tn,),
        in_specs=[pl.BlockSpec((bsz, d), lambda j: (0, 0)),
                  pl.BlockSpec((d, tn), lambda j: (0, j)),
                  pl.BlockSpec((1, tn), lambda j: (0, j))],
        out_specs=pl.BlockSpec((bsz, tn), lambda j: (0, j)),
        out_shape=jax.ShapeDtypeStruct((bsz, n), F32),
        compiler_params=_cparams("parallel"),
        name="ada",
    )(c, w, b.reshape(1, n))


def _t5_bucket_np(dist):
    n = np.maximum(dist, 0)
    max_exact = N_BUCKETS // 2
    nf = np.maximum(n, 1).astype(np.float64)
    large = max_exact + (np.log(nf / max_exact) / math.log(MAX_DISTANCE / max_exact)
                         * (N_BUCKETS - max_exact)).astype(np.int64)
    return np.where(n < max_exact, n, np.minimum(large, N_BUCKETS - 1)).astype(np.int32)


def _bias_kernel(tab_ref, bkt_ref, o_ref):
    h = pl.program_id(0)
    bkt = bkt_ref[...]
    acc = jnp.where(bkt < 0, NEG_INF, 0.0).astype(F32)
    for i in range(N_BUCKETS):
        acc = jnp.where(bkt == i, tab_ref[i, h] * LOG2E, acc)
    o_ref[0] = acc


def _bias_table(rel_table, bucket):
    rows, cols = bucket.shape
    return pl.pallas_call(
        _bias_kernel,
        grid=(N_HEADS,),
        in_specs=[pl.BlockSpec(memory_space=pltpu.SMEM),
                  pl.BlockSpec((rows, cols), lambda h: (0, 0))],
        out_specs=pl.BlockSpec((1, rows, cols), lambda h: (h, 0, 0)),
        out_shape=jax.ShapeDtypeStruct((N_HEADS, rows, cols), F32),
        compiler_params=_cparams("parallel"),
        name="bias_table",
    )(rel_table, jnp.asarray(bucket))


def _static_buckets(seq):
    t = np.arange(seq)[:, None]
    n = np.arange(LANE)[None, :]
    dist_c = t - (n * CMP_STRIDE + CMP_BLOCK - 1)
    bkt_c = np.where(dist_c >= 0, _t5_bucket_np(dist_c), -1).astype(np.int32)
    qi = np.arange(TQ)[:, None]
    kj = np.arange(TQ)[None, :]
    tiles = [np.where(d * TQ + qi - kj >= 0, _t5_bucket_np(d * TQ + qi - kj), -1) for d in range(3)]
    bkt_t = np.concatenate(tiles, axis=0).astype(np.int32)
    kw = np.arange(WIN_SPAN)[None, :]
    dist_w = WINDOW + qi - kw
    band = (dist_w >= 0) & (dist_w < WINDOW)
    bkt_w = np.where(band, _t5_bucket_np(dist_w), -1).astype(np.int32)
    return bkt_c, bkt_t, bkt_w


def _proj_kernel(x_ref, mod_ref, g_ref, w_ref, add_ref, o_ref, h_ref, *, heads_out, tn):
    @pl.when(pl.program_id(1) == 0)
    def _():
        x = x_ref[...]
        sh = mod_ref[0, 0:1, :]
        sc = mod_ref[0, 1:2, :]
        h_ref[...] = (_rms(x, g_ref[...]) * (1.0 + sc) + sh).astype(BF16)

    res = (jnp.dot(h_ref[...], w_ref[...], preferred_element_type=F32) + add_ref[...]).astype(o_ref.dtype)
    if heads_out:
        for k in range(tn // LANE):
            o_ref[0, k] = res[:, k * LANE:(k + 1) * LANE]
    else:
        o_ref[...] = res


def _project(x2d, mod3, g, w, add, seq, heads_out):
    t, d = x2d.shape
    n = w.shape[1]
    bsz = t // seq
    tiles_per_seq = seq // PROJ_TM
    tn = PROJ_TN_HEADS if heads_out else PROJ_TN_REST
    assert n % tn == 0
    grid = (t // PROJ_TM, n // tn)
    if heads_out:
        hpt = tn // LANE
        out_shape = jax.ShapeDtypeStruct((bsz, n // LANE, seq, LANE), BF16)
        out_spec = pl.BlockSpec((1, hpt, PROJ_TM, LANE),
                                lambda i, j: (i // tiles_per_seq, j, i % tiles_per_seq, 0))
    else:
        out_shape = jax.ShapeDtypeStruct((t, n), BF16)
        out_spec = pl.BlockSpec((PROJ_TM, tn), lambda i, j: (i, j))
    return pl.pallas_call(
        functools.partial(_proj_kernel, heads_out=heads_out, tn=tn),
        grid=grid,
        in_specs=[pl.BlockSpec((PROJ_TM, d), lambda i, j: (i, 0)),
                  pl.BlockSpec((1, 2, d), lambda i, j: (i // tiles_per_seq, 0, 0)),
                  pl.BlockSpec((1, d), lambda i, j: (0, 0)),
                  pl.BlockSpec((d, tn), lambda i, j: (0, j)),
                  pl.BlockSpec((1, tn), lambda i, j: (0, j))],
        out_specs=out_spec,
        out_shape=out_shape,
        scratch_shapes=[pltpu.VMEM((PROJ_TM, d), BF16)],
        compiler_params=_cparams("parallel", "arbitrary"),
        name="proj_heads" if heads_out else "proj_rest",
    )(x2d, mod3, g, w, add)


def _compress_kernel(kch_ref, vch_ref, pk_ref, pv_ref, w1k_ref, w2k_ref, w1v_ref, w2v_ref,
                     kc_ref, vc_ref):
    def one(ch_ref, pos_ref, w1_ref, w2_ref, o_ref):
        a = ch_ref[0, 0].astype(F32)
        a1 = (a + pos_ref[0:1, :]).astype(BF16)
        a2 = (a + pos_ref[1:2, :]).astype(BF16)
        p1 = jnp.dot(a1, w1_ref[0], preferred_element_type=F32)
        p2 = jnp.dot(a2, w1_ref[1], preferred_element_type=F32)
        n_chunk = p2.shape[0]
        hid = _gelu(p1 + pltpu.roll(p2, n_chunk - 1, 0))
        o_ref[0, 0] = jnp.dot(hid.astype(BF16), w2_ref[...],
                              preferred_element_type=F32).astype(BF16)

    one(kch_ref, pk_ref, w1k_ref, w2k_ref, kc_ref)
    one(vch_ref, pv_ref, w1v_ref, w2v_ref, vc_ref)


def _compress(kch, vch, pk, pv, w1k, w2k, w1v, w2v):
    bsz, g, n_chunk, width = kch.shape
    assert n_chunk == LANE, "compressed keys are laid out on one 128-row tile"
    spec_in = pl.BlockSpec((1, 1, n_chunk, width), lambda b, gg: (b, gg, 0, 0))
    spec_out = pl.BlockSpec((1, 1, n_chunk, LANE), lambda b, gg: (b, gg, 0, 0))
    full = lambda a: pl.BlockSpec(a.shape, lambda b, gg: (0,) * a.ndim)
    out = jax.ShapeDtypeStruct((bsz, g, n_chunk, LANE), BF16)
    return pl.pallas_call(
        _compress_kernel,
        grid=(bsz, g),
        in_specs=[spec_in, spec_in, full(pk), full(pv), full(w1k), full(w2k), full(w1v), full(w2v)],
        out_specs=[spec_out, spec_out],
        out_shape=[out, out],
        compiler_params=_cparams("parallel", "parallel"),
        name="compress",
    )(kch, vch, pk, pv, w1k, w2k, w1v, w2v)


def _attn_kernel(q_ref, kc_ref, vc_ref, ks_ref, vs_ref, kw_ref, vw_ref, gate_ref,
                 bias_c_ref, bias_t_ref, bias_w_ref, overlap_ref, expand_ref,
                 o_ref, mask_ref):
    qi = pl.program_id(2)
    rows = HEADS_PER_GROUP * TQ
    seq = mask_ref.shape[1]
    q4 = q_ref[0].reshape(rows, LANE)

    def per_head(x):
        return jnp.broadcast_to(x[None], (HEADS_PER_GROUP,) + x.shape).reshape(rows, x.shape[-1])

    lc = lax.dot_general(q4, kc_ref[0, 0], NT_DIMS, preferred_element_type=F32)
    lc = lc + bias_c_ref[...].reshape(rows, LANE)
    valid = lc > 0.5 * NEG_INF
    mc = jnp.max(lc, axis=-1, keepdims=True)
    ec = jnp.where(valid, jnp.exp2(lc - mc), 0.0)
    sc = jnp.sum(ec, axis=-1, keepdims=True)
    pc = ec / jnp.where(sc > 0.0, sc, 1.0)
    o_cmp = jnp.dot(pc.astype(BF16), vc_ref[0, 0], preferred_element_type=F32)

    psum = jnp.sum(pc.reshape(HEADS_PER_GROUP, TQ, LANE), axis=0)
    imp = jnp.dot(psum, overlap_ref[...], preferred_element_type=F32,
                  precision=lax.Precision.HIGHEST)
    t_pos = qi * TQ + lax.broadcasted_iota(jnp.int32, (TQ, LANE), 0)
    j_blk = lax.broadcasted_iota(jnp.int32, (TQ, LANE), 1)
    cur = t_pos // SLC_BLOCK
    forced = (j_blk == 0) | (j_blk == cur) | (j_blk == cur - 1)
    score = jnp.where(j_blk <= cur, imp + jnp.where(forced, FORCE_BONUS, 0.0), NEG_INF)
    n_slc = seq // SLC_BLOCK
    sc_t = score.T[:n_slc]
    j_row = lax.broadcasted_iota(jnp.int32, (n_slc, TQ), 0)
    rank = jnp.zeros((n_slc, TQ), F32)
    for jp in range(n_slc):
        row = sc_t[jp:jp + 1, :]
        ge = jnp.where(row >= sc_t, 1.0, 0.0)
        gt = jnp.where(row > sc_t, 1.0, 0.0)
        rank = rank + jnp.where(j_row > jp, ge, gt)
    sel_t = jnp.where(rank < float(min(SLC_TOPN, n_slc)), 1.0, 0.0)
    sel_t = jnp.concatenate([sel_t, jnp.zeros((LANE - n_slc, TQ), F32)], axis=0)
    sel = ((sel_t.T - 1.0) * (-NEG_INF)).astype(BF16)
    mask_ref[...] = jnp.dot(sel, expand_ref[...], preferred_element_type=F32)

    tiles_per_chunk = SLC_CHUNK // TQ

    def slc_step(c, carry):
        m, acc = carry
        k0 = pl.multiple_of(c * SLC_CHUNK, SLC_CHUNK)
        kt = ks_ref[0, 0, pl.ds(k0, SLC_CHUNK), :]
        vt = vs_ref[0, 0, pl.ds(k0, SLC_CHUNK), :]
        s = lax.dot_general(q4, kt, NT_DIMS, preferred_element_type=F32)
        bias = jnp.concatenate(
            [bias_t_ref[jnp.clip(qi - (c * tiles_per_chunk + k), 0, 2)]
             for k in range(tiles_per_chunk)], axis=-1)
        s = s + bias.reshape(rows, SLC_CHUNK) + per_head(mask_ref[:, pl.ds(k0, SLC_CHUNK)])
        m_new = jnp.maximum(m, jnp.max(s, axis=-1, keepdims=True))
        p = jnp.exp2(s - m_new)
        acc = jnp.exp2(m - m_new) * acc + jnp.dot(p.astype(BF16), vt, preferred_element_type=F32)
        return m_new, acc

    n_chunks = (qi + tiles_per_chunk) // tiles_per_chunk
    init = (jnp.full((rows, 1), -jnp.inf, F32), jnp.zeros((rows, LANE), F32))
    _, acc_s = lax.fori_loop(0, n_chunks, slc_step, init)
    o_slc = acc_s / pltpu.roll(acc_s, LANE - D_V, 1)

    w0 = pl.multiple_of(qi * TQ, TQ)
    kwin = kw_ref[0, 0, pl.ds(w0, WIN_SPAN), :]
    vwin = vw_ref[0, 0, pl.ds(w0, WIN_SPAN), :]
    sw = lax.dot_general(q4, kwin, NT_DIMS, preferred_element_type=F32)
    sw = sw + bias_w_ref[...].reshape(rows, WIN_SPAN)
    kj = lax.broadcasted_iota(jnp.int32, (1, WIN_SPAN), 1)
    sw = sw + jnp.where(kj >= WINDOW - qi * TQ, 0.0, NEG_INF)
    mw = jnp.max(sw, axis=-1, keepdims=True)
    pw = jnp.exp2(sw - mw)
    o_win = jnp.dot(pw.astype(BF16), vwin, preferred_element_type=F32)
    o_win = o_win / pltpu.roll(o_win, LANE - D_V, 1)

    gates = _sigmoid(gate_ref[...].astype(F32))
    outs = []
    for r in range(HEADS_PER_GROUP):
        sl = slice(r * TQ, (r + 1) * TQ)
        g_c = gates[:, r:r + 1]
        g_s = gates[:, HEADS_PER_GROUP + r:HEADS_PER_GROUP + r + 1]
        g_w = gates[:, 2 * HEADS_PER_GROUP + r:2 * HEADS_PER_GROUP + r + 1]
        o = g_c * o_cmp[sl] + g_s * o_slc[sl] + g_w * o_win[sl]
        outs.append(o[:, :D_V])
    o_ref[0] = jnp.concatenate(outs, axis=-1).astype(o_ref.dtype)


def _attention(qkv, kwp, vwp, kc, vc, rest, bias_c, bias_t, bias_w, overlap, expand):
    bsz, _, seq, _ = qkv.shape
    nq = seq // TQ
    g_cols0 = REST_GATE0 // LANE
    head = lambda base: pl.BlockSpec((1, 1, seq, LANE), lambda b, g, i: (b, base + g, 0, 0))
    in_specs = [
        pl.BlockSpec((1, HEADS_PER_GROUP, TQ, LANE), lambda b, g, i: (b, g, i, 0)),
        pl.BlockSpec((1, 1, LANE, LANE), lambda b, g, i: (b, g, 0, 0)),
        pl.BlockSpec((1, 1, LANE, LANE), lambda b, g, i: (b, g, 0, 0)),
        head(QKV_KS0), head(QKV_VS0),
        pl.BlockSpec((1, 1, seq + WINDOW, LANE), lambda b, g, i: (b, g, 0, 0)),
        pl.BlockSpec((1, 1, seq + WINDOW, LANE), lambda b, g, i: (b, g, 0, 0)),
        pl.BlockSpec((TQ, LANE), lambda b, g, i: (b * nq + i, g_cols0 + g)),
        pl.BlockSpec((HEADS_PER_GROUP, TQ, LANE), lambda b, g, i: (g, i, 0)),
        pl.BlockSpec((3, HEADS_PER_GROUP, TQ, TQ), lambda b, g, i: (0, g, 0, 0)),
        pl.BlockSpec((HEADS_PER_GROUP, TQ, WIN_SPAN), lambda b, g, i: (g, 0, 0)),
        pl.BlockSpec((LANE, LANE), lambda b, g, i: (0, 0)),
        pl.BlockSpec((LANE, seq), lambda b, g, i: (0, 0)),
    ]
    return pl.pallas_call(
        _attn_kernel,
        grid=(bsz, N_GROUPS, nq),
        in_specs=in_specs,
        out_specs=pl.BlockSpec((1, TQ, HEADS_PER_GROUP * D_V), lambda b, g, i: (b, i, g)),
        out_shape=jax.ShapeDtypeStruct((bsz, seq, N_HEADS * D_V), BF16),
        scratch_shapes=[pltpu.VMEM((TQ, seq), F32)],
        compiler_params=_cparams("parallel", "parallel", "arbitrary"),
        name="attention",
    )(qkv, kc, vc, qkv, qkv, kwp, vwp, rest, bias_c, bias_t, bias_w, overlap, expand)


def _tail_kernel(attn_ref, ga_ref, gg_ref, ha_ref, hg_ref, ma_ref, mb_ref, x_ref, mod_ref,
                 wa_ref, wc_ref, wo_ref, cw_ref, cvec_ref, gvec_ref,
                 x1_ref, h2t_ref, u_ref, ush_ref, *, tiles_per_seq):
    first = (pl.program_id(0) % tiles_per_seq) == 0
    y_a = jnp.dot(attn_ref[...], wa_ref[...], preferred_element_type=F32)

    halo = ha_ref[...].astype(F32) * _sigmoid(hg_ref[...].astype(F32))
    u_ref[0:HALO, :] = jnp.where(first, 0.0, halo)
    u_ref[HALO:, :] = ga_ref[...].astype(F32) * _sigmoid(gg_ref[...].astype(F32))

    conv_b, ln_g, ln_b = cvec_ref[0:1, :], cvec_ref[1:2, :], cvec_ref[2:3, :]
    n_sh = HALO + TAIL_TM - SUBLANE
    for r in range(1, SUBLANE):
        ush_ref[r - 1, 0:n_sh, :] = u_ref[r:r + n_sh, :]
    blk = 32
    pieces = []
    for rb in range(TAIL_TM // blk):
        acc = jnp.zeros((blk, D_MODEL), F32)
        for j in range(CONV_WIDTH):
            start = HALO - (CONV_WIDTH - 1) + rb * blk + j
            r, a = start % SUBLANE, start - start % SUBLANE
            win = u_ref[a:a + blk, :] if r == 0 else ush_ref[r - 1, a:a + blk, :]
            acc = acc + cw_ref[j:j + 1, :] * win
        pieces.append(acc)
    y = jnp.concatenate(pieces, axis=0) + conv_b
    yc = y - jnp.mean(y, axis=-1, keepdims=True)
    yn = yc * lax.rsqrt(jnp.mean(yc * yc, axis=-1, keepdims=True) + NORM_EPS) * ln_g + ln_b
    act = yn * _sigmoid(yn)
    y_b = jnp.dot(act.astype(BF16), wc_ref[...], preferred_element_type=F32)

    merged = (_sigmoid(ma_ref[...].astype(F32)) * y_a + _sigmoid(mb_ref[...].astype(F32)) * y_b)
    out = jnp.dot(merged.astype(BF16), wo_ref[...], preferred_element_type=F32)

    gt1, sh2, sc2 = mod_ref[0, 0:1, :], mod_ref[0, 1:2, :], mod_ref[0, 2:3, :]
    x1 = x_ref[...] + gt1 * _rms(out, gvec_ref[0:1, :])
    x1_ref[...] = x1
    h2 = _rms(x1, gvec_ref[1:2, :]) * (1.0 + sc2) + sh2
    h2t_ref[...] = h2.T.astype(BF16)


def _mixer_tail(attn2d, rest, x2d, mod3, wa, wc, wo, cw, cvec, gvec, seq):
    t, d = x2d.shape
    tiles_per_seq = seq // TAIL_TM
    hpt = TAIL_TM // HALO
    row = lambda cb: pl.BlockSpec((TAIL_TM, d), lambda i: (i, cb))
    halo = lambda cb: pl.BlockSpec((HALO, d), lambda i: (jnp.maximum(i * hpt - 1, 0), cb))
    full = lambda a: pl.BlockSpec(a.shape, lambda i: (0,) * a.ndim)
    return pl.pallas_call(
        functools.partial(_tail_kernel, tiles_per_seq=tiles_per_seq),
        grid=(t // TAIL_TM,),
        in_specs=[row(0), row(REST_GLU_A), row(REST_GLU_G), halo(REST_GLU_A), halo(REST_GLU_G),
                  row(REST_MRG_A), row(REST_MRG_B), row(0),
                  pl.BlockSpec((1, 3, d), lambda i: (i // tiles_per_seq, 0, 0)),
                  full(wa), full(wc), full(wo), full(cw), full(cvec), full(gvec)],
        out_specs=[row(0), pl.BlockSpec((d, TAIL_TM), lambda i: (0, i))],
        out_shape=[jax.ShapeDtypeStruct((t, d), F32), jax.ShapeDtypeStruct((d, t), BF16)],
        scratch_shapes=[pltpu.VMEM((HALO + TAIL_TM, d), F32),
                        pltpu.VMEM((SUBLANE - 1, HALO + TAIL_TM, d), F32)],
        compiler_params=_cparams("parallel"),
        name="mixer_tail",
    )(attn2d, rest, rest, rest, rest, rest, rest, x2d, mod3, wa, wc, wo, cw, cvec, gvec)


def _cand_pairs():
    return [(i, j) for i in range(PEER_TOPK) for j in range(PEER_TOPK)
            if (i + 1) * (j + 1) <= PEER_TOPK]


def _top16_rows(s):
    n = s.shape[0]
    kio = lax.broadcasted_iota(jnp.int32, s.shape, 0).astype(F32)
    vals, idxs = [], []
    for _ in range(PEER_TOPK):
        m = jnp.max(s, axis=0, keepdims=True)
        idx = jnp.min(jnp.where(s == m, kio, float(n)), axis=0, keepdims=True)
        vals.append(m)
        idxs.append(idx)
        s = jnp.where(kio == idx, -jnp.inf, s)
    return vals, idxs


def _sort16_network():
    def merge(lo, hi, r):
        step = r * 2
        if step < hi - lo:
            yield from merge(lo, hi, step)
            yield from merge(lo + r, hi, step)
            yield from [(i, i + r) for i in range(lo + r, hi - r, step)]
        else:
            yield (lo, lo + r)

    def sort(lo, hi):
        if hi - lo >= 1:
            mid = lo + (hi - lo) // 2
            yield from sort(lo, mid)
            yield from sort(mid + 1, hi)
            yield from merge(lo, hi, 1)

    return list(sort(0, PEER_TOPK - 1))


def _top16_values(s):
    n_grp = s.shape[0] // SUBLANE
    assert n_grp == PEER_TOPK
    lst = [s[SUBLANE * v:SUBLANE * (v + 1), :] for v in range(n_grp)]
    for a, b in _sort16_network():
        lst[a], lst[b] = jnp.maximum(lst[a], lst[b]), jnp.minimum(lst[a], lst[b])
    for shift in (4, 2, 1):
        other = [pltpu.roll(x, shift, 0) for x in lst]
        lst = [jnp.maximum(lst[j], other[n_grp - 1 - j]) for j in range(n_grp)]
        dist = n_grp // 2
        while dist >= 1:
            for j in range(n_grp):
                if (j // dist) % 2 == 0:
                    lst[j], lst[j + dist] = (jnp.maximum(lst[j], lst[j + dist]),
                                             jnp.minimum(lst[j], lst[j + dist]))
            dist //= 2
    return [x[0:1, :] for x in lst]


def _route_kernel(h2t_ref, wqt_ref, sk_ref, f_ref, cnt_ref, g_ref, rb_ref):
    q2t = jnp.dot(wqt_ref[...], h2t_ref[...], preferred_element_type=F32)
    tl = q2t.shape[1]
    kio = lax.broadcasted_iota(jnp.int32, (PEER_NKEYS, tl), 0).astype(F32)
    pairs = _cand_pairs()
    n_pad = -len(pairs) % 8
    flat_ids = np.array([i * PEER_TOPK + j for i, j in pairs] + [PEER_TOPK ** 2] * n_pad, np.float32)
    group_start = [min(k for k, (i, _) in enumerate(pairs) if i == ii) for ii in range(PEER_TOPK)]
    group_len = [sum(1 for (i, _) in pairs if i == ii) for ii in range(PEER_TOPK)]
    n_rows = len(pairs) + n_pad
    flat_col = lax.broadcasted_iota(jnp.int32, (n_rows, tl), 0)
    flat = jnp.zeros((n_rows, tl), F32)
    for k in range(n_rows):
        flat = jnp.where(flat_col == k, float(flat_ids[k]), flat)

    def head_scores(h):
        out = []
        for c in range(2):
            hc = 2 * h + c
            qt = q2t[hc * LANE:(hc + 1) * LANE, :].astype(BF16)
            out.append(jnp.dot(sk_ref[hc], qt, preferred_element_type=F32))
        return out

    def write_tables(h, scores, v1, v2, is_rank1, is_rank2):
        cand0 = jnp.concatenate([v1[i] + v2[j] for i, j in pairs]
                                + [jnp.full((n_pad, tl), -jnp.inf, F32)], axis=0)
        cand = cand0
        for _ in range(PEER_TOPK):
            m = jnp.max(cand, axis=0, keepdims=True)
            fid = jnp.min(jnp.where(cand == m, flat, float(PEER_TOPK ** 2 + 1)), axis=0, keepdims=True)
            cand = jnp.where(flat == fid, -jnp.inf, cand)
        picked = jnp.where((cand == -jnp.inf) & (flat < float(PEER_TOPK ** 2)), 1.0, 0.0)
        top = v1[0] + v2[0]
        z = jnp.sum(picked * jnp.exp(cand0 - top), axis=0, keepdims=True)
        counts = [jnp.sum(picked[group_start[i]:group_start[i] + group_len[i]], axis=0, keepdims=True)
                  for i in range(PEER_TOPK)]
        cnt = jnp.zeros((PEER_NKEYS, tl), F32)
        rank_b = jnp.full((PEER_NKEYS, tl), float(PEER_NKEYS), F32)
        for i in range(PEER_TOPK):
            cnt = jnp.where(is_rank1(i), counts[i], cnt)
            rank_b = jnp.where(is_rank2(i), float(i), rank_b)
        outs = ((f_ref, jnp.exp(scores[0] - v1[0]) * (0.5 / z)),
                (g_ref, jnp.exp(scores[1] - v2[0])), (cnt_ref, cnt), (rb_ref, rank_b))
        for ref, val in outs:
            for ch in range(tl // LANE):
                ref[h, ch] = val[:, ch * LANE:(ch + 1) * LANE].astype(ref.dtype)

    tied = jnp.zeros((1, tl), F32)
    for h in range(PEER_HEADS):
        scores = head_scores(h)
        tops = [_top16_values(sc) for sc in scores]
        for sc, v in zip(scores, tops):
            n_ge = jnp.sum(jnp.where(sc >= v[PEER_TOPK - 1], 1.0, 0.0), axis=0, keepdims=True)
            tied = jnp.maximum(tied, jnp.where(n_ge == float(PEER_TOPK), 0.0, 1.0))
            for i in range(PEER_TOPK - 1):
                tied = jnp.maximum(tied, jnp.where(v[i] > v[i + 1], 0.0, 1.0))
        write_tables(h, scores, tops[0], tops[1],
                     lambda i, sc=scores[0], v=tops[0]: sc == v[i],
                     lambda i, sc=scores[1], v=tops[1]: sc == v[i])

    @pl.when(jnp.max(tied) > 0.0)
    def _():
        for h in range(PEER_HEADS):
            scores = head_scores(h)
            (v1, i1), (v2, i2) = [_top16_rows(sc) for sc in scores]
            write_tables(h, scores, v1, v2, lambda i, i1=i1: kio == i1[i], lambda i, i2=i2: kio == i2[i])


def _route(h2t, wqt, sk):
    d, t = h2t.shape
    spec = pl.BlockSpec((PEER_HEADS, ROUTE_TL // LANE, PEER_NKEYS, LANE), lambda i: (0, i, 0, 0))
    out = jax.ShapeDtypeStruct((PEER_HEADS, t // LANE, PEER_NKEYS, LANE), F32)
    out16 = jax.ShapeDtypeStruct((PEER_HEADS, t // LANE, PEER_NKEYS, LANE), BF16)
    return pl.pallas_call(
        _route_kernel,
        grid=(t // ROUTE_TL,),
        in_specs=[pl.BlockSpec((d, ROUTE_TL), lambda i: (0, i)),
                  pl.BlockSpec(wqt.shape, lambda i: (0, 0)),
                  pl.BlockSpec(sk.shape, lambda i: (0, 0, 0))],
        out_specs=[spec, spec, spec, spec],
        out_shape=[out, out, out16, out16],
        compiler_params=_cparams("parallel"),
        name="peer_route",
    )(h2t, wqt, sk)


def _expert_kernel(h2t_ref, u_ref, vt_ref, f_ref, cnt_ref, g_ref, rb_ref, x1_ref, mod_ref, gp_ref,
                   o_ref, acc_ref, act_ref, coef_ref):
    e = pl.program_id(1)
    n_chunks = EXP_TL // LANE
    a_per_step = EXP_TE // PEER_NKEYS
    a_group = 2

    @pl.when(e == 0)
    def _():
        acc_ref[...] = jnp.zeros_like(acc_ref)

    act = jnp.dot(u_ref[...], h2t_ref[...], preferred_element_type=F32)
    for c in range(n_chunks):
        act_ref[c] = act[:, c * LANE:(c + 1) * LANE]

    def lane_chunk(c, carry):
        for ag in range(a_per_step // a_group):
            coefs = [jnp.zeros((PEER_NKEYS, LANE), BF16) for _ in range(a_group)]

            def row_tile(ref, h, a):
                row = jnp.broadcast_to(ref[h, c, pl.ds(a, 1), :], (BF16_ROWS, LANE)).astype(BF16)
                return jnp.broadcast_to(row[None], (PEER_NKEYS // BF16_ROWS, BF16_ROWS, LANE)
                                        ).reshape(PEER_NKEYS, LANE)

            for h in range(PEER_HEADS):
                rank_b = rb_ref[h, c]
                g_b = g_ref[h, c]
                for k in range(a_group):
                    a = e * a_per_step + ag * a_group + k
                    picked = jnp.maximum(jnp.minimum(row_tile(cnt_ref, h, a) - rank_b, g_b), 0.0)
                    coefs[k] = coefs[k] + row_tile(f_ref, h, a) * picked
            for k in range(a_group):
                rows = pl.ds((ag * a_group + k) * PEER_NKEYS, PEER_NKEYS)
                coef_ref[c, rows, :] = _gelu_x2(act_ref[c, rows, :].astype(BF16)) * coefs[k]
        return carry

    lax.fori_loop(0, n_chunks, lane_chunk, 0)
    coef = jnp.concatenate([coef_ref[c] for c in range(n_chunks)], axis=1)
    acc_ref[...] += jnp.dot(vt_ref[0], coef, preferred_element_type=F32)

    @pl.when(e == pl.num_programs(1) - 1)
    def _():
        y = acc_ref[...].T
        o_ref[...] = x1_ref[...] + mod_ref[0] * _rms(y, gp_ref[...])


def _experts(h2t, u, vt, f, cnt, g, rb, x1, gt2, gpost, seq):
    d, t = h2t.shape
    n_exp = u.shape[0]
    tiles_per_seq = seq // EXP_TL
    n_chunks = EXP_TL // LANE
    route = pl.BlockSpec((PEER_HEADS, n_chunks, PEER_NKEYS, LANE), lambda i, e: (0, i, 0, 0))
    return pl.pallas_call(
        _expert_kernel,
        grid=(t // EXP_TL, n_exp // EXP_TE),
        in_specs=[pl.BlockSpec((d, EXP_TL), lambda i, e: (0, i)),
                  pl.BlockSpec((EXP_TE, d), lambda i, e: (e, 0)),
                  pl.BlockSpec((1, d, EXP_TE), lambda i, e: (e, 0, 0)),
                  route, route, route, route,
                  pl.BlockSpec((EXP_TL, d), lambda i, e: (i, 0)),
                  pl.BlockSpec((1, 1, d), lambda i, e: (i // tiles_per_seq, 0, 0)),
                  pl.BlockSpec((1, d), lambda i, e: (0, 0))],
        out_specs=pl.BlockSpec((EXP_TL, d), lambda i, e: (i, 0)),
        out_shape=jax.ShapeDtypeStruct((t, d), F32),
        scratch_shapes=[pltpu.VMEM((d, EXP_TL), F32), pltpu.VMEM((n_chunks, EXP_TE, LANE), F32),
                        pltpu.VMEM((n_chunks, EXP_TE, LANE), BF16)],
        compiler_params=_cparams("parallel", "arbitrary"),
        name="peer_experts",
    )(h2t, u, vt, f, cnt, g, rb, x1, gt2, gpost)


def _pad_heads(w, n, width):
    d = w.shape[0]
    return jnp.pad(w.reshape(d, n, width), ((0, 0), (0, 0), (0, LANE - width))).reshape(d, n * LANE)


def _split_w_in(w_in):
    q_cols = N_HEADS * D_QK
    k_cols = N_GROUPS * D_QK
    v_cols = N_GROUPS * D_V
    sizes = (q_cols, k_cols, k_cols, k_cols, v_cols, v_cols, v_cols, 3 * N_HEADS, 2 * D_MODEL, 2 * D_MODEL)
    offs = np.cumsum((0,) + sizes)
    parts = [w_in[:, offs[k]:offs[k + 1]] for k in range(len(sizes))]
    wq, wkc, wks, wkw, wvc, wvs, wvw, wgate, wglu, wmerge = parts
    w_qkv = jnp.concatenate(
        [_pad_heads(wq * (ATTN_SCALE * LOG2E), N_HEADS, D_QK)]
        + [_pad_heads(w, N_GROUPS, D_QK) for w in (wkc, wks, wkw)]
        + [_pad_heads(w, N_GROUPS, D_V) for w in (wvc, wvs, wvw)], axis=1)
    d = w_in.shape[0]
    wg = wgate.reshape(d, 3, N_GROUPS, HEADS_PER_GROUP).transpose(0, 2, 1, 3)
    wg = wg.reshape(d, N_GROUPS, 3 * HEADS_PER_GROUP)
    wg = jnp.pad(wg, ((0, 0), (0, 0), (0, LANE - 3 * HEADS_PER_GROUP))).reshape(d, N_GROUPS * LANE)
    w_rest = jnp.concatenate([wglu, wmerge, wg], axis=1)
    return w_qkv.astype(BF16), w_rest.astype(BF16)


def _cmp_weights(w1, w2, pos, dh):
    hidden = w1.shape[1]
    w1p = jnp.pad(w1.reshape(CMP_BLOCK, dh, hidden), ((0, 0), (0, LANE - dh), (0, 0)))
    w1p = w1p.reshape(2, CMP_STRIDE * LANE, hidden).astype(BF16)
    w2p = jnp.pad(w2, ((0, 0), (0, LANE - dh))).astype(BF16)
    posp = jnp.pad(pos, ((0, 0), (0, LANE - dh))).reshape(2, CMP_STRIDE * LANE)
    return w1p, w2p, posp


def kernel(x, c, w_ada, b_ada, g_pre_mix, g_post_mix, g_pre_ffn, g_post_ffn, rel_table, w_in,
           cmp_w1k, cmp_w2k, cmp_pos_k, cmp_w1v, cmp_w2v, cmp_pos_v, w_attn_out,
           conv_w, conv_b, conv_ln_g, conv_ln_b, w_conv_out, w_out,
           peer_wq, peer_subkeys, peer_u, peer_v):
    bsz, seq, d = x.shape
    t = bsz * seq
    assert d == D_MODEL and w_ada.shape[0] == 1, "single-layer block with D_MODEL channels"
    assert seq % PROJ_TM == 0 and seq // CMP_STRIDE == LANE and seq % EXP_TL == 0
    x2d = x.reshape(t, d)

    mod = _ada(c, w_ada[0], b_ada[0]).reshape(bsz, 6, d)
    mod_in = mod[:, 0:2]
    mod_tail = mod[:, 2:5]
    mod_out = mod[:, 5:6]

    bkt_c, bkt_t, bkt_w = _static_buckets(seq)
    bias_c = _bias_table(rel_table, bkt_c)
    bias_t = _bias_table(rel_table, bkt_t).reshape(N_HEADS, 3, TQ, TQ).transpose(1, 0, 2, 3)
    bias_w = _bias_table(rel_table, bkt_w)

    w_qkv, w_rest = _split_w_in(w_in[0])
    add_qkv = np.zeros((QKV_HEADS, LANE), np.float32)
    add_qkv[QKV_VS0:QKV_VS0 + N_GROUPS, D_V:] = 1.0
    add_qkv[QKV_VW0:QKV_VW0 + N_GROUPS, D_V:] = 1.0
    qkv = _project(x2d, mod_in, g_pre_mix, w_qkv, jnp.asarray(add_qkv.reshape(1, -1)), seq,
                   heads_out=True)
    rest = _project(x2d, mod_in, g_pre_mix, w_rest, jnp.zeros((1, REST_COLS), F32), seq,
                    heads_out=False)

    n_chunk = seq // CMP_STRIDE
    kch = qkv[:, QKV_KC0:QKV_KC0 + N_GROUPS].reshape(bsz, N_GROUPS, n_chunk, CMP_STRIDE * LANE)
    vch = qkv[:, QKV_VC0:QKV_VC0 + N_GROUPS].reshape(bsz, N_GROUPS, n_chunk, CMP_STRIDE * LANE)
    w1k, w2k, pk = _cmp_weights(cmp_w1k[0], cmp_w2k[0], cmp_pos_k[0], D_QK)
    w1v, w2v, pv = _cmp_weights(cmp_w1v[0], cmp_w2v[0], cmp_pos_v[0], D_V)
    kc, vc = _compress(kch, vch, pk, pv, w1k, w2k, w1v, w2v)

    pad = ((0, 0), (0, 0), (WINDOW, 0), (0, 0))
    kwp = jnp.pad(qkv[:, QKV_KW0:QKV_KW0 + N_GROUPS], pad)
    vwp = jnp.pad(qkv[:, QKV_VW0:QKV_VW0 + N_GROUPS], pad)

    n_idx = np.arange(LANE)[:, None] * CMP_STRIDE
    j_idx = np.arange(LANE)[None, :] * SLC_BLOCK
    overlap = ((n_idx <= j_idx + SLC_BLOCK - 1) & (n_idx + CMP_BLOCK - 1 >= j_idx)
               & (np.arange(LANE)[None, :] < seq // SLC_BLOCK)
               & (np.arange(LANE)[:, None] < n_chunk - 1)).astype(np.float32)
    expand = (np.arange(LANE)[:, None] == np.arange(seq)[None, :] // SLC_BLOCK)
    attn = _attention(qkv, kwp, vwp, kc, vc, rest, bias_c, bias_t, bias_w,
                      jnp.asarray(overlap), jnp.asarray(expand, dtype=BF16))

    cvec = jnp.stack([conv_b[0], conv_ln_g[0], conv_ln_b[0]])
    gvec = jnp.stack([g_post_mix[0], g_pre_ffn[0]])
    x1, h2t = _mixer_tail(attn.reshape(t, N_HEADS * D_V), rest, x2d, mod_tail,
                          w_attn_out[0].astype(BF16), w_conv_out[0].astype(BF16),
                          w_out[0].astype(BF16), conv_w[0, :, 0, :], cvec, gvec, seq)

    wqt = peer_wq[0].T.astype(BF16)
    sk = peer_subkeys[0].reshape(2 * PEER_HEADS, PEER_NKEYS, PEER_DQ // 2).astype(BF16)
    f, cnt, g, rb = _route(h2t, wqt, sk)
    n_exp = peer_v.shape[1]
    vt = peer_v[0].reshape(n_exp // EXP_TE, EXP_TE, d).transpose(0, 2, 1).astype(BF16)
    out = _experts(h2t, peer_u[0].astype(BF16), vt, f, cnt, g, rb, x1, mod_out, g_post_ffn, seq)
    return out.reshape(bsz, seq, d)
```

```python
import functools
import math

import numpy as np
import jax
import jax.numpy as jnp
from jax import lax
from jax.experimental import pallas as pl
from jax.experimental.pallas import tpu as pltpu

F32 = jnp.float32
BF16 = jnp.bfloat16

D_MODEL = 1024
N_HEADS = 16
N_GROUPS = 4
HEADS_PER_GROUP = N_HEADS // N_GROUPS
D_QK = 96
D_V = 64
CMP_BLOCK = 32
CMP_STRIDE = 16
CMP_HIDDEN = 256
SLC_BLOCK = 64
SLC_TOPN = 16
WINDOW = 512
ATTN_SCALE = D_QK ** -0.5
FORCE_BONUS = 1e6
N_BUCKETS = 32
MAX_DISTANCE = 128
CONV_WIDTH = 31
PEER_HEADS = 8
PEER_NKEYS = 128
PEER_DQ = 256
PEER_TOPK = 16
NORM_EPS = 1e-6
NEG_INF = -1e30
LOG2E = math.log2(math.e)

LANE = 128
SUBLANE = 8
BF16_ROWS = 16
VMEM_LIMIT = 56 * 1024 * 1024

TQ = 256
SLC_CHUNK = 256
WIN_SPAN = TQ + WINDOW
PROJ_TM = 2048
PROJ_TN_HEADS = 1024
PROJ_TN_REST = 768
TAIL_TM = 256
HALO = 32
ROUTE_TL = 256
EXP_TL = 512
EXP_TE = 2048

QKV_Q0, QKV_KC0, QKV_KS0, QKV_KW0, QKV_VC0, QKV_VS0, QKV_VW0 = 0, 16, 20, 24, 28, 32, 36
QKV_HEADS = 40
REST_GLU_A, REST_GLU_G, REST_MRG_A, REST_MRG_B = 0, 1, 2, 3
REST_GATE0 = 4 * D_MODEL
REST_COLS = 4 * D_MODEL + N_GROUPS * LANE

NT_DIMS = (((1,), (1,)), ((), ()))


def _cparams(*sem):
    return pltpu.CompilerParams(dimension_semantics=sem, vmem_limit_bytes=VMEM_LIMIT)


def _gelu(x):
    return 0.5 * x * (1.0 + jnp.tanh(math.sqrt(2.0 / math.pi) * (x + 0.044715 * (x * x * x))))


def _gelu_x2(x):
    k1 = math.sqrt(2.0 / math.pi)
    return x * (1.0 + jnp.tanh(x * (k1 + (k1 * 0.044715) * (x * x))))


def _sigmoid(x):
    return 1.0 / (1.0 + jnp.exp(-x))


def _rms(x, g):
    return x * lax.rsqrt(jnp.mean(x * x, axis=-1, keepdims=True) + NORM_EPS) * g


def _ada_kernel(c_ref, w_ref, b_ref, o_ref):
    c = c_ref[...]
    c_act = c * _sigmoid(c)
    o_ref[...] = jnp.dot(c_act, w_ref[...], preferred_element_type=F32,
                         precision=lax.Precision.HIGHEST) + b_ref[...]


def _ada(c, w, b):
    bsz, d = c.shape
    n = w.shape[1]
    tn = 1024
    return pl.pallas_call(
        _ada_kernel,
        grid=(n // tn,),
        in_specs=[pl.BlockSpec((bsz, d), lambda j: (0, 0)),
                  pl.BlockSpec((d, tn), lambda j: (0, j)),
                  pl.BlockSpec((1, tn), lambda j: (0, j))],
        out_specs=pl.BlockSpec((bsz, tn), lambda j: (0, j)),
        out_shape=jax.ShapeDtypeStruct((bsz, n), F32),
        compiler_params=_cparams("parallel"),
        name="ada",
    )(c, w, b.reshape(1, n))


def _t5_bucket_np(dist):
    n = np.maximum(dist, 0)
    max_exact = N_BUCKETS // 2
    nf = np.maximum(n, 1).astype(np.float64)
    large = max_exact + (np.log(nf / max_exact) / math.log(MAX_DISTANCE / max_exact)
                         * (N_BUCKETS - max_exact)).astype(np.int64)
    return np.where(n < max_exact, n, np.minimum(large, N_BUCKETS - 1)).astype(np.int32)


def _bias_kernel(tab_ref, bkt_ref, o_ref):
    h = pl.program_id(0)
    bkt = bkt_ref[...]
    acc = jnp.where(bkt < 0, NEG_INF, 0.0).astype(F32)
    for i in range(N_BUCKETS):
        acc = jnp.where(bkt == i, tab_ref[i, h] * LOG2E, acc)
    o_ref[0] = acc


def _bias_table(rel_table, bucket):
    rows, cols = bucket.shape
    return pl.pallas_call(
        _bias_kernel,
        grid=(N_HEADS,),
        in_specs=[pl.BlockSpec(memory_space=pltpu.SMEM),
                  pl.BlockSpec((rows, cols), lambda h: (0, 0))],
        out_specs=pl.BlockSpec((1, rows, cols), lambda h: (h, 0, 0)),
        out_shape=jax.ShapeDtypeStruct((N_HEADS, rows, cols), F32),
        compiler_params=_cparams("parallel"),
        name="bias_table",
    )(rel_table, jnp.asarray(bucket))


def _static_buckets(seq):
    t = np.arange(seq)[:, None]
    n = np.arange(LANE)[None, :]
    dist_c = t - (n * CMP_STRIDE + CMP_BLOCK - 1)
    bkt_c = np.where(dist_c >= 0, _t5_bucket_np(dist_c), -1).astype(np.int32)
    qi = np.arange(TQ)[:, None]
    kj = np.arange(TQ)[None, :]
    tiles = [np.where(d * TQ + qi - kj >= 0, _t5_bucket_np(d * TQ + qi - kj), -1) for d in range(3)]
    bkt_t = np.concatenate(tiles, axis=0).astype(np.int32)
    kw = np.arange(WIN_SPAN)[None, :]
    dist_w = WINDOW + qi - kw
    band = (dist_w >= 0) & (dist_w < WINDOW)
    bkt_w = np.where(band, _t5_bucket_np(dist_w), -1).astype(np.int32)
    return bkt_c, bkt_t, bkt_w


def _proj_kernel(x_ref, mod_ref, g_ref, w_ref, add_ref, o_ref, h_ref, *, heads_out, tn):
    @pl.when(pl.program_id(1) == 0)
    def _():
        x = x_ref[...]
        sh = mod_ref[0, 0:1, :]
        sc = mod_ref[0, 1:2, :]
        h_ref[...] = (_rms(x, g_ref[...]) * (1.0 + sc) + sh).astype(BF16)

    res = (jnp.dot(h_ref[...], w_ref[...], preferred_element_type=F32) + add_ref[...]).astype(o_ref.dtype)
    if heads_out:
        for k in range(tn // LANE):
            o_ref[0, k] = res[:, k * LANE:(k + 1) * LANE]
    else:
        o_ref[...] = res


def _project(x2d, mod3, g, w, add, seq, heads_out):
    t, d = x2d.shape
    n = w.shape[1]
    bsz = t // seq
    tiles_per_seq = seq // PROJ_TM
    tn = PROJ_TN_HEADS if heads_out else PROJ_TN_REST
    assert n % tn == 0
    grid = (t // PROJ_TM, n // tn)
    if heads_out:
        hpt = tn // LANE
        out_shape = jax.ShapeDtypeStruct((bsz, n // LANE, seq, LANE), BF16)
        out_spec = pl.BlockSpec((1, hpt, PROJ_TM, LANE),
                                lambda i, j: (i // tiles_per_seq, j, i % tiles_per_seq, 0))
    else:
        out_shape = jax.ShapeDtypeStruct((t, n), BF16)
        out_spec = pl.BlockSpec((PROJ_TM, tn), lambda i, j: (i, j))
    return pl.pallas_call(
        functools.partial(_proj_kernel, heads_out=heads_out, tn=tn),
        grid=grid,
        in_specs=[pl.BlockSpec((PROJ_TM, d), lambda i, j: (i, 0)),
                  pl.BlockSpec((1, 2, d), lambda i, j: (i // tiles_per_seq, 0, 0)),
                  pl.BlockSpec((1, d), lambda i, j: (0, 0)),
                  pl.BlockSpec((d, tn), lambda i, j: (0, j)),
                  pl.BlockSpec((1, tn), lambda i, j: (0, j))],
        out_specs=out_spec,
        out_shape=out_shape,
        scratch_shapes=[pltpu.VMEM((PROJ_TM, d), BF16)],
        compiler_params=_cparams("parallel", "arbitrary"),
        name="proj_heads" if heads_out else "proj_rest",
    )(x2d, mod3, g, w, add)


def _compress_kernel(kch_ref, vch_ref, pk_ref, pv_ref, w1k_ref, w2k_ref, w1v_ref, w2v_ref,
                     kc_ref, vc_ref):
    def one(ch_ref, pos_ref, w1_ref, w2_ref, o_ref):
        a = ch_ref[0, 0].astype(F32)
        a1 = (a + pos_ref[0:1, :]).astype(BF16)
        a2 = (a + pos_ref[1:2, :]).astype(BF16)
        p1 = jnp.dot(a1, w1_ref[0], preferred_element_type=F32)
        p2 = jnp.dot(a2, w1_ref[1], preferred_element_type=F32)
        n_chunk = p2.shape[0]
        hid = _gelu(p1 + pltpu.roll(p2, n_chunk - 1, 0))
        o_ref[0, 0] = jnp.dot(hid.astype(BF16), w2_ref[...],
                              preferred_element_type=F32).astype(BF16)

    one(kch_ref, pk_ref, w1k_ref, w2k_ref, kc_ref)
    one(vch_ref, pv_ref, w1v_ref, w2v_ref, vc_ref)


def _compress(kch, vch, pk, pv, w1k, w2k, w1v, w2v):
    bsz, g, n_chunk, width = kch.shape
    assert n_chunk == LANE, "compressed keys are laid out on one 128-row tile"
    spec_in = pl.BlockSpec((1, 1, n_chunk, width), lambda b, gg: (b, gg, 0, 0))
    spec_out = pl.BlockSpec((1, 1, n_chunk, LANE), lambda b, gg: (b, gg, 0, 0))
    full = lambda a: pl.BlockSpec(a.shape, lambda b, gg: (0,) * a.ndim)
    out = jax.ShapeDtypeStruct((bsz, g, n_chunk, LANE), BF16)
    return pl.pallas_call(
        _compress_kernel,
        grid=(bsz, g),
        in_specs=[spec_in, spec_in, full(pk), full(pv), full(w1k), full(w2k), full(w1v), full(w2v)],
        out_specs=[spec_out, spec_out],
        out_shape=[out, out],
        compiler_params=_cparams("parallel", "parallel"),
        name="compress",
    )(kch, vch, pk, pv, w1k, w2k, w1v, w2v)


def _attn_kernel(q_ref, kc_ref, vc_ref, ks_ref, vs_ref, kw_ref, vw_ref, gate_ref,
                 bias_c_ref, bias_t_ref, bias_w_ref, overlap_ref,
                 o_ref):
    qi = pl.program_id(2)
    rows = HEADS_PER_GROUP * TQ
    seq = ks_ref.shape[2]
    q4 = q_ref[0].reshape(rows, LANE)

    lc = lax.dot_general(q4, kc_ref[0, 0], NT_DIMS, preferred_element_type=F32)
    lc = lc + bias_c_ref[...].reshape(rows, LANE)
    valid = lc > 0.5 * NEG_INF
    mc = jnp.max(lc, axis=-1, keepdims=True)
    ec = jnp.where(valid, jnp.exp2(lc - mc), 0.0)
    sc = jnp.sum(ec, axis=-1, keepdims=True)
    pc = ec / jnp.where(sc > 0.0, sc, 1.0)
    o_cmp = jnp.dot(pc.astype(BF16), vc_ref[0, 0], preferred_element_type=F32)

    psum = jnp.sum(pc.reshape(HEADS_PER_GROUP, TQ, LANE), axis=0)
    imp = jnp.dot(psum, overlap_ref[...], preferred_element_type=F32,
                  precision=lax.Precision.HIGHEST)
    t_pos = qi * TQ + lax.broadcasted_iota(jnp.int32, (TQ, LANE), 0)
    j_blk = lax.broadcasted_iota(jnp.int32, (TQ, LANE), 1)
    cur = t_pos // SLC_BLOCK
    forced = (j_blk == 0) | (j_blk == cur) | (j_blk == cur - 1)
    score = jnp.where(j_blk <= cur, imp + jnp.where(forced, FORCE_BONUS, 0.0), NEG_INF)
    n_slc = seq // SLC_BLOCK
    sc_t = score.T[:n_slc]
    j_row = lax.broadcasted_iota(jnp.int32, (n_slc, TQ), 0)
    rank = jnp.zeros((n_slc, TQ), F32)
    for jp in range(n_slc):
        row = sc_t[jp:jp + 1, :]
        ge = jnp.where(row >= sc_t, 1.0, 0.0)
        gt = jnp.where(row > sc_t, 1.0, 0.0)
        rank = rank + jnp.where(j_row > jp, ge, gt)
    sel_t = jnp.where(rank < float(min(SLC_TOPN, n_slc)), 1.0, 0.0)
    assert D_QK + n_slc == LANE
    sel_add = jnp.concatenate([jnp.zeros((D_QK, TQ), F32), (sel_t - 1.0) * (-NEG_INF)], axis=0)
    q_sel = (q_ref[0] + sel_add.T.astype(BF16)[None]).reshape(rows, LANE)

    tiles_per_chunk = SLC_CHUNK // TQ

    def slc_step(c, carry):
        m, acc = carry
        k0 = pl.multiple_of(c * SLC_CHUNK, SLC_CHUNK)
        kt = ks_ref[0, 0, pl.ds(k0, SLC_CHUNK), :]
        vt = vs_ref[0, 0, pl.ds(k0, SLC_CHUNK), :]
        s = lax.dot_general(q_sel, kt, NT_DIMS, preferred_element_type=F32)
        bias = jnp.concatenate(
            [bias_t_ref[jnp.clip(qi - (c * tiles_per_chunk + k), 0, 2)]
             for k in range(tiles_per_chunk)], axis=-1)
        s = s + bias.reshape(rows, SLC_CHUNK)
        m_new = jnp.maximum(m, jnp.max(s, axis=-1, keepdims=True))
        p = jnp.exp2(s - m_new)
        acc = jnp.exp2(m - m_new) * acc + jnp.dot(p.astype(BF16), vt, preferred_element_type=F32)
        return m_new, acc

    n_chunks = (qi + tiles_per_chunk) // tiles_per_chunk
    init = (jnp.full((rows, 1), -jnp.inf, F32), jnp.zeros((rows, LANE), F32))
    _, acc_s = lax.fori_loop(0, n_chunks, slc_step, init)
    o_slc = acc_s / pltpu.roll(acc_s, LANE - D_V, 1)

    w0 = pl.multiple_of(qi * TQ, TQ)
    kwin = kw_ref[0, 0, pl.ds(w0, WIN_SPAN), :]
    vwin = vw_ref[0, 0, pl.ds(w0, WIN_SPAN), :]
    sw = lax.dot_general(q4, kwin, NT_DIMS, preferred_element_type=F32)
    sw = sw + bias_w_ref[...].reshape(rows, WIN_SPAN)
    kj = lax.broadcasted_iota(jnp.int32, (1, WIN_SPAN), 1)
    sw = sw + jnp.where(kj >= WINDOW - qi * TQ, 0.0, NEG_INF)
    mw = jnp.max(sw, axis=-1, keepdims=True)
    pw = jnp.exp2(sw - mw)
    o_win = jnp.dot(pw.astype(BF16), vwin, preferred_element_type=F32)
    o_win = o_win / pltpu.roll(o_win, LANE - D_V, 1)

    gates = _sigmoid(gate_ref[...].astype(F32))
    outs = []
    for r in range(HEADS_PER_GROUP):
        sl = slice(r * TQ, (r + 1) * TQ)
        g_c = gates[:, r:r + 1]
        g_s = gates[:, HEADS_PER_GROUP + r:HEADS_PER_GROUP + r + 1]
        g_w = gates[:, 2 * HEADS_PER_GROUP + r:2 * HEADS_PER_GROUP + r + 1]
        o = g_c * o_cmp[sl] + g_s * o_slc[sl] + g_w * o_win[sl]
        outs.append(o[:, :D_V])
    o_ref[0] = jnp.concatenate(outs, axis=-1).astype(o_ref.dtype)


def _attention(qkv, ks, kwp, vwp, kc, vc, rest, bias_c, bias_t, bias_w, overlap):
    bsz, _, seq, _ = qkv.shape
    nq = seq // TQ
    g_cols0 = REST_GATE0 // LANE
    head = lambda base: pl.BlockSpec((1, 1, seq, LANE), lambda b, g, i: (b, base + g, 0, 0))
    in_specs = [
        pl.BlockSpec((1, HEADS_PER_GROUP, TQ, LANE), lambda b, g, i: (b, g, i, 0)),
        pl.BlockSpec((1, 1, LANE, LANE), lambda b, g, i: (b, g, 0, 0)),
        pl.BlockSpec((1, 1, LANE, LANE), lambda b, g, i: (b, g, 0, 0)),
        pl.BlockSpec((1, 1, seq, LANE), lambda b, g, i: (b, g, 0, 0)),
        head(QKV_VS0),
        pl.BlockSpec((1, 1, seq + WINDOW, LANE), lambda b, g, i: (b, g, 0, 0)),
        pl.BlockSpec((1, 1, seq + WINDOW, LANE), lambda b, g, i: (b, g, 0, 0)),
        pl.BlockSpec((TQ, LANE), lambda b, g, i: (b * nq + i, g_cols0 + g)),
        pl.BlockSpec((HEADS_PER_GROUP, TQ, LANE), lambda b, g, i: (g, i, 0)),
        pl.BlockSpec((3, HEADS_PER_GROUP, TQ, TQ), lambda b, g, i: (0, g, 0, 0)),
        pl.BlockSpec((HEADS_PER_GROUP, TQ, WIN_SPAN), lambda b, g, i: (g, 0, 0)),
        pl.BlockSpec((LANE, LANE), lambda b, g, i: (0, 0)),
    ]
    return pl.pallas_call(
        _attn_kernel,
        grid=(bsz, N_GROUPS, nq),
        in_specs=in_specs,
        out_specs=pl.BlockSpec((1, TQ, HEADS_PER_GROUP * D_V), lambda b, g, i: (b, i, g)),
        out_shape=jax.ShapeDtypeStruct((bsz, seq, N_HEADS * D_V), BF16),
        compiler_params=_cparams("parallel", "parallel", "arbitrary"),
        name="attention",
    )(qkv, kc, vc, ks, qkv, kwp, vwp, rest, bias_c, bias_t, bias_w, overlap)


def _tail_kernel(attn_ref, ga_ref, gg_ref, ha_ref, hg_ref, ma_ref, mb_ref, x_ref, mod_ref,
                 wa_ref, wc_ref, wo_ref, cw_ref, cvec_ref, gvec_ref,
                 x1_ref, h2t_ref, u_ref, ush_ref, *, tiles_per_seq):
    first = (pl.program_id(0) % tiles_per_seq) == 0
    y_a = jnp.dot(attn_ref[...], wa_ref[...], preferred_element_type=F32)

    halo = ha_ref[...].astype(F32) * _sigmoid(hg_ref[...].astype(F32))
    u_ref[0:HALO, :] = jnp.where(first, 0.0, halo)
    u_ref[HALO:, :] = ga_ref[...].astype(F32) * _sigmoid(gg_ref[...].astype(F32))

    conv_b, ln_g, ln_b = cvec_ref[0:1, :], cvec_ref[1:2, :], cvec_ref[2:3, :]
    n_sh = HALO + TAIL_TM - SUBLANE
    for r in range(1, SUBLANE):
        ush_ref[r - 1, 0:n_sh, :] = u_ref[r:r + n_sh, :]
    blk = 32
    pieces = []
    for rb in range(TAIL_TM // blk):
        acc = jnp.zeros((blk, D_MODEL), F32)
        for j in range(CONV_WIDTH):
            start = HALO - (CONV_WIDTH - 1) + rb * blk + j
            r, a = start % SUBLANE, start - start % SUBLANE
            win = u_ref[a:a + blk, :] if r == 0 else ush_ref[r - 1, a:a + blk, :]
            acc = acc + cw_ref[j:j + 1, :] * win
        pieces.append(acc)
    y = jnp.concatenate(pieces, axis=0) + conv_b
    yc = y - jnp.mean(y, axis=-1, keepdims=True)
    yn = yc * lax.rsqrt(jnp.mean(yc * yc, axis=-1, keepdims=True) + NORM_EPS) * ln_g + ln_b
    act = yn * _sigmoid(yn)
    y_b = jnp.dot(act.astype(BF16), wc_ref[...], preferred_element_type=F32)

    merged = (_sigmoid(ma_ref[...].astype(F32)) * y_a + _sigmoid(mb_ref[...].astype(F32)) * y_b)
    out = jnp.dot(merged.astype(BF16), wo_ref[...], preferred_element_type=F32)

    gt1, sh2, sc2 = mod_ref[0, 0:1, :], mod_ref[0, 1:2, :], mod_ref[0, 2:3, :]
    x1 = x_ref[...] + gt1 * _rms(out, gvec_ref[0:1, :])
    x1_ref[...] = x1
    h2 = _rms(x1, gvec_ref[1:2, :]) * (1.0 + sc2) + sh2
    h2t_ref[...] = h2.T.astype(BF16)


def _mixer_tail(attn2d, rest, x2d, mod3, wa, wc, wo, cw, cvec, gvec, seq):
    t, d = x2d.shape
    tiles_per_seq = seq // TAIL_TM
    hpt = TAIL_TM // HALO
    row = lambda cb: pl.BlockSpec((TAIL_TM, d), lambda i: (i, cb))
    halo = lambda cb: pl.BlockSpec((HALO, d), lambda i: (jnp.maximum(i * hpt - 1, 0), cb))
    full = lambda a: pl.BlockSpec(a.shape, lambda i: (0,) * a.ndim)
    return pl.pallas_call(
        functools.partial(_tail_kernel, tiles_per_seq=tiles_per_seq),
        grid=(t // TAIL_TM,),
        in_specs=[row(0), row(REST_GLU_A), row(REST_GLU_G), halo(REST_GLU_A), halo(REST_GLU_G),
                  row(REST_MRG_A), row(REST_MRG_B), row(0),
                  pl.BlockSpec((1, 3, d), lambda i: (i // tiles_per_seq, 0, 0)),
                  full(wa), full(wc), full(wo), full(cw), full(cvec), full(gvec)],
        out_specs=[row(0), pl.BlockSpec((d, TAIL_TM), lambda i: (0, i))],
        out_shape=[jax.ShapeDtypeStruct((t, d), F32), jax.ShapeDtypeStruct((d, t), BF16)],
        scratch_shapes=[pltpu.VMEM((HALO + TAIL_TM, d), F32),
                        pltpu.VMEM((SUBLANE - 1, HALO + TAIL_TM, d), F32)],
        compiler_params=_cparams("parallel"),
        name="mixer_tail",
    )(attn2d, rest, rest, rest, rest, rest, rest, x2d, mod3, wa, wc, wo, cw, cvec, gvec)


def _cand_pairs():
    return [(i, j) for i in range(PEER_TOPK) for j in range(PEER_TOPK)
            if (i + 1) * (j + 1) <= PEER_TOPK]


def _top16_rows(s):
    n = s.shape[0]
    kio = lax.broadcasted_iota(jnp.int32, s.shape, 0).astype(F32)
    vals, idxs = [], []
    for _ in range(PEER_TOPK):
        m = jnp.max(s, axis=0, keepdims=True)
        idx = jnp.min(jnp.where(s == m, kio, float(n)), axis=0, keepdims=True)
        vals.append(m)
        idxs.append(idx)
        s = jnp.where(kio == idx, -jnp.inf, s)
    return vals, idxs


def _sort16_network():
    def merge(lo, hi, r):
        step = r * 2
        if step < hi - lo:
            yield from merge(lo, hi, step)
            yield from merge(lo + r, hi, step)
            yield from [(i, i + r) for i in range(lo + r, hi - r, step)]
        else:
            yield (lo, lo + r)

    def sort(lo, hi):
        if hi - lo >= 1:
            mid = lo + (hi - lo) // 2
            yield from sort(lo, mid)
            yield from sort(mid + 1, hi)
            yield from merge(lo, hi, 1)

    return list(sort(0, PEER_TOPK - 1))


def _top16_values(s):
    n_grp = s.shape[0] // SUBLANE
    assert n_grp == PEER_TOPK
    lst = [s[SUBLANE * v:SUBLANE * (v + 1), :] for v in range(n_grp)]
    for a, b in _sort16_network():
        lst[a], lst[b] = jnp.maximum(lst[a], lst[b]), jnp.minimum(lst[a], lst[b])
    for shift in (4, 2, 1):
        other = [pltpu.roll(x, shift, 0) for x in lst]
        lst = [jnp.maximum(lst[j], other[n_grp - 1 - j]) for j in range(n_grp)]
        dist = n_grp // 2
        while dist >= 1:
            for j in range(n_grp):
                if (j // dist) % 2 == 0:
                    lst[j], lst[j + dist] = (jnp.maximum(lst[j], lst[j + dist]),
                                             jnp.minimum(lst[j], lst[j + dist]))
            dist //= 2
    return [x[0:1, :] for x in lst]


def _route_kernel(h2t_ref, wqt_ref, sk_ref, f_ref, cnt_ref, g_ref, rb_ref):
    q2t = jnp.dot(wqt_ref[...], h2t_ref[...], preferred_element_type=F32)
    tl = q2t.shape[1]
    kio = lax.broadcasted_iota(jnp.int32, (PEER_NKEYS, tl), 0).astype(F32)
    pairs = _cand_pairs()
    n_pad = -len(pairs) % 8
    flat_ids = np.array([i * PEER_TOPK + j for i, j in pairs] + [PEER_TOPK ** 2] * n_pad, np.float32)
    group_start = [min(k for k, (i, _) in enumerate(pairs) if i == ii) for ii in range(PEER_TOPK)]
    group_len = [sum(1 for (i, _) in pairs if i == ii) for ii in range(PEER_TOPK)]
    n_rows = len(pairs) + n_pad
    flat_col = lax.broadcasted_iota(jnp.int32, (n_rows, tl), 0)
    flat = jnp.zeros((n_rows, tl), F32)
    for k in range(n_rows):
        flat = jnp.where(flat_col == k, float(flat_ids[k]), flat)

    def head_scores(h):
        out = []
        for c in range(2):
            hc = 2 * h + c
            qt = q2t[hc * LANE:(hc + 1) * LANE, :].astype(BF16)
            out.append(jnp.dot(sk_ref[hc], qt, preferred_element_type=F32))
        return out

    def write_tables(h, scores, v1, v2, is_rank1, is_rank2):
        cand0 = jnp.concatenate([v1[i] + v2[j] for i, j in pairs]
                                + [jnp.full((n_pad, tl), -jnp.inf, F32)], axis=0)
        cand = cand0
        for _ in range(PEER_TOPK):
            m = jnp.max(cand, axis=0, keepdims=True)
            fid = jnp.min(jnp.where(cand == m, flat, float(PEER_TOPK ** 2 + 1)), axis=0, keepdims=True)
            cand = jnp.where(flat == fid, -jnp.inf, cand)
        picked = jnp.where((cand == -jnp.inf) & (flat < float(PEER_TOPK ** 2)), 1.0, 0.0)
        top = v1[0] + v2[0]
        z = jnp.sum(picked * jnp.exp(cand0 - top), axis=0, keepdims=True)
        counts = [jnp.sum(picked[group_start[i]:group_start[i] + group_len[i]], axis=0, keepdims=True)
                  for i in range(PEER_TOPK)]
        cnt = jnp.zeros((PEER_NKEYS, tl), F32)
        rank_b = jnp.full((PEER_NKEYS, tl), float(PEER_NKEYS), F32)
        for i in range(PEER_TOPK):
            cnt = jnp.where(is_rank1(i), counts[i], cnt)
            rank_b = jnp.where(is_rank2(i), float(i), rank_b)
        outs = ((f_ref, jnp.exp(scores[0] - v1[0]) * (0.5 / z)),
                (g_ref, jnp.exp(scores[1] - v2[0])), (cnt_ref, cnt), (rb_ref, rank_b))
        for ref, val in outs:
            for ch in range(tl // LANE):
                ref[h, ch] = val[:, ch * LANE:(ch + 1) * LANE].astype(ref.dtype)

    tied = jnp.zeros((1, tl), F32)
    for h in range(PEER_HEADS):
        scores = head_scores(h)
        tops = [_top16_values(sc) for sc in scores]
        for sc, v in zip(scores, tops):
            n_ge = jnp.sum(jnp.where(sc >= v[PEER_TOPK - 1], 1.0, 0.0), axis=0, keepdims=True)
            tied = jnp.maximum(tied, jnp.where(n_ge == float(PEER_TOPK), 0.0, 1.0))
            for i in range(PEER_TOPK - 1):
                tied = jnp.maximum(tied, jnp.where(v[i] > v[i + 1], 0.0, 1.0))
        write_tables(h, scores, tops[0], tops[1],
                     lambda i, sc=scores[0], v=tops[0]: sc == v[i],
                     lambda i, sc=scores[1], v=tops[1]: sc == v[i])

    @pl.when(jnp.max(tied) > 0.0)
    def _():
        for h in range(PEER_HEADS):
            scores = head_scores(h)
            (v1, i1), (v2, i2) = [_top16_rows(sc) for sc in scores]
            write_tables(h, scores, v1, v2, lambda i, i1=i1: kio == i1[i], lambda i, i2=i2: kio == i2[i])


def _route(h2t, wqt, sk):
    d, t = h2t.shape
    spec = pl.BlockSpec((PEER_HEADS, ROUTE_TL // LANE, PEER_NKEYS, LANE), lambda i: (0, i, 0, 0))
    out = jax.ShapeDtypeStruct((PEER_HEADS, t // LANE, PEER_NKEYS, LANE), F32)
    out16 = jax.ShapeDtypeStruct((PEER_HEADS, t // LANE, PEER_NKEYS, LANE), BF16)
    return pl.pallas_call(
        _route_kernel,
        grid=(t // ROUTE_TL,),
        in_specs=[pl.BlockSpec((d, ROUTE_TL), lambda i: (0, i)),
                  pl.BlockSpec(wqt.shape, lambda i: (0, 0)),
                  pl.BlockSpec(sk.shape, lambda i: (0, 0, 0))],
        out_specs=[spec, spec, spec, spec],
        out_shape=[out, out, out16, out16],
        compiler_params=_cparams("parallel"),
        name="peer_route",
    )(h2t, wqt, sk)


def _expert_kernel(h2t_ref, u_ref, vt_ref, f_ref, cnt_ref, g_ref, rb_ref, x1_ref, mod_ref, gp_ref,
                   o_ref, acc_ref, act_ref, coef_ref):
    e = pl.program_id(1)
    n_chunks = EXP_TL // LANE
    a_per_step = EXP_TE // PEER_NKEYS
    a_group = 2

    @pl.when(e == 0)
    def _():
        acc_ref[...] = jnp.zeros_like(acc_ref)

    act = jnp.dot(u_ref[...], h2t_ref[...], preferred_element_type=F32)
    for c in range(n_chunks):
        act_ref[c] = act[:, c * LANE:(c + 1) * LANE]

    def lane_chunk(c, carry):
        for ag in range(a_per_step // a_group):
            coefs = [jnp.zeros((PEER_NKEYS, LANE), BF16) for _ in range(a_group)]

            def row_tile(ref, h, a):
                row = jnp.broadcast_to(ref[h, c, pl.ds(a, 1), :], (BF16_ROWS, LANE)).astype(BF16)
                return jnp.broadcast_to(row[None], (PEER_NKEYS // BF16_ROWS, BF16_ROWS, LANE)
                                        ).reshape(PEER_NKEYS, LANE)

            for h in range(PEER_HEADS):
                rank_b = rb_ref[h, c]
                g_b = g_ref[h, c]
                for k in range(a_group):
                    a = e * a_per_step + ag * a_group + k
                    picked = jnp.maximum(jnp.minimum(row_tile(cnt_ref, h, a) - rank_b, g_b), 0.0)
                    coefs[k] = coefs[k] + row_tile(f_ref, h, a) * picked
            for k in range(a_group):
                rows = pl.ds((ag * a_group + k) * PEER_NKEYS, PEER_NKEYS)
                coef_ref[c, rows, :] = _gelu_x2(act_ref[c, rows, :].astype(BF16)) * coefs[k]
        return carry

    lax.fori_loop(0, n_chunks, lane_chunk, 0)
    coef = jnp.concatenate([coef_ref[c] for c in range(n_chunks)], axis=1)
    acc_ref[...] += jnp.dot(vt_ref[0], coef, preferred_element_type=F32)

    @pl.when(e == pl.num_programs(1) - 1)
    def _():
        y = acc_ref[...].T
        o_ref[...] = x1_ref[...] + mod_ref[0] * _rms(y, gp_ref[...])


def _experts(h2t, u, vt, f, cnt, g, rb, x1, gt2, gpost, seq):
    d, t = h2t.shape
    n_exp = u.shape[0]
    tiles_per_seq = seq // EXP_TL
    n_chunks = EXP_TL // LANE
    route = pl.BlockSpec((PEER_HEADS, n_chunks, PEER_NKEYS, LANE), lambda i, e: (0, i, 0, 0))
    return pl.pallas_call(
        _expert_kernel,
        grid=(t // EXP_TL, n_exp // EXP_TE),
        in_specs=[pl.BlockSpec((d, EXP_TL), lambda i, e: (0, i)),
                  pl.BlockSpec((EXP_TE, d), lambda i, e: (e, 0)),
                  pl.BlockSpec((1, d, EXP_TE), lambda i, e: (e, 0, 0)),
                  route, route, route, route,
                  pl.BlockSpec((EXP_TL, d), lambda i, e: (i, 0)),
                  pl.BlockSpec((1, 1, d), lambda i, e: (i // tiles_per_seq, 0, 0)),
                  pl.BlockSpec((1, d), lambda i, e: (0, 0))],
        out_specs=pl.BlockSpec((EXP_TL, d), lambda i, e: (i, 0)),
        out_shape=jax.ShapeDtypeStruct((t, d), F32),
        scratch_shapes=[pltpu.VMEM((d, EXP_TL), F32), pltpu.VMEM((n_chunks, EXP_TE, LANE), F32),
                        pltpu.VMEM((n_chunks, EXP_TE, LANE), BF16)],
        compiler_params=_cparams("parallel", "arbitrary"),
        name="peer_experts",
    )(h2t, u, vt, f, cnt, g, rb, x1, gt2, gpost)


def _pad_heads(w, n, width):
    d = w.shape[0]
    return jnp.pad(w.reshape(d, n, width), ((0, 0), (0, 0), (0, LANE - width))).reshape(d, n * LANE)


def _split_w_in(w_in):
    q_cols = N_HEADS * D_QK
    k_cols = N_GROUPS * D_QK
    v_cols = N_GROUPS * D_V
    sizes = (q_cols, k_cols, k_cols, k_cols, v_cols, v_cols, v_cols, 3 * N_HEADS, 2 * D_MODEL, 2 * D_MODEL)
    offs = np.cumsum((0,) + sizes)
    parts = [w_in[:, offs[k]:offs[k + 1]] for k in range(len(sizes))]
    wq, wkc, wks, wkw, wvc, wvs, wvw, wgate, wglu, wmerge = parts
    w_qkv = jnp.concatenate(
        [_pad_heads(wq * (ATTN_SCALE * LOG2E), N_HEADS, D_QK)]
        + [_pad_heads(w, N_GROUPS, D_QK) for w in (wkc, wks, wkw)]
        + [_pad_heads(w, N_GROUPS, D_V) for w in (wvc, wvs, wvw)], axis=1)
    d = w_in.shape[0]
    wg = wgate.reshape(d, 3, N_GROUPS, HEADS_PER_GROUP).transpose(0, 2, 1, 3)
    wg = wg.reshape(d, N_GROUPS, 3 * HEADS_PER_GROUP)
    wg = jnp.pad(wg, ((0, 0), (0, 0), (0, LANE - 3 * HEADS_PER_GROUP))).reshape(d, N_GROUPS * LANE)
    w_rest = jnp.concatenate([wglu, wmerge, wg], axis=1)
    return w_qkv.astype(BF16), w_rest.astype(BF16)


def _cmp_weights(w1, w2, pos, dh):
    hidden = w1.shape[1]
    w1p = jnp.pad(w1.reshape(CMP_BLOCK, dh, hidden), ((0, 0), (0, LANE - dh), (0, 0)))
    w1p = w1p.reshape(2, CMP_STRIDE * LANE, hidden).astype(BF16)
    w2p = jnp.pad(w2, ((0, 0), (0, LANE - dh))).astype(BF16)
    posp = jnp.pad(pos, ((0, 0), (0, LANE - dh))).reshape(2, CMP_STRIDE * LANE)
    return w1p, w2p, posp


def kernel(x, c, w_ada, b_ada, g_pre_mix, g_post_mix, g_pre_ffn, g_post_ffn, rel_table, w_in,
           cmp_w1k, cmp_w2k, cmp_pos_k, cmp_w1v, cmp_w2v, cmp_pos_v, w_attn_out,
           conv_w, conv_b, conv_ln_g, conv_ln_b, w_conv_out, w_out,
           peer_wq, peer_subkeys, peer_u, peer_v):
    bsz, seq, d = x.shape
    t = bsz * seq
    assert d == D_MODEL and w_ada.shape[0] == 1, "single-layer block with D_MODEL channels"
    assert seq % PROJ_TM == 0 and seq // CMP_STRIDE == LANE and seq % EXP_TL == 0
    x2d = x.reshape(t, d)

    mod = _ada(c, w_ada[0], b_ada[0]).reshape(bsz, 6, d)
    mod_in = mod[:, 0:2]
    mod_tail = mod[:, 2:5]
    mod_out = mod[:, 5:6]

    bkt_c, bkt_t, bkt_w = _static_buckets(seq)
    bias_c = _bias_table(rel_table, bkt_c)
    bias_t = _bias_table(rel_table, bkt_t).reshape(N_HEADS, 3, TQ, TQ).transpose(1, 0, 2, 3)
    bias_w = _bias_table(rel_table, bkt_w)

    w_qkv, w_rest = _split_w_in(w_in[0])
    add_qkv = np.zeros((QKV_HEADS, LANE), np.float32)
    add_qkv[QKV_VS0:QKV_VS0 + N_GROUPS, D_V:] = 1.0
    add_qkv[QKV_VW0:QKV_VW0 + N_GROUPS, D_V:] = 1.0
    qkv = _project(x2d, mod_in, g_pre_mix, w_qkv, jnp.asarray(add_qkv.reshape(1, -1)), seq,
                   heads_out=True)
    rest = _project(x2d, mod_in, g_pre_mix, w_rest, jnp.zeros((1, REST_COLS), F32), seq,
                    heads_out=False)

    n_chunk = seq // CMP_STRIDE
    kch = qkv[:, QKV_KC0:QKV_KC0 + N_GROUPS].reshape(bsz, N_GROUPS, n_chunk, CMP_STRIDE * LANE)
    vch = qkv[:, QKV_VC0:QKV_VC0 + N_GROUPS].reshape(bsz, N_GROUPS, n_chunk, CMP_STRIDE * LANE)
    w1k, w2k, pk = _cmp_weights(cmp_w1k[0], cmp_w2k[0], cmp_pos_k[0], D_QK)
    w1v, w2v, pv = _cmp_weights(cmp_w1v[0], cmp_w2v[0], cmp_pos_v[0], D_V)
    kc, vc = _compress(kch, vch, pk, pv, w1k, w2k, w1v, w2v)

    pad = ((0, 0), (0, 0), (WINDOW, 0), (0, 0))
    kwp = jnp.pad(qkv[:, QKV_KW0:QKV_KW0 + N_GROUPS], pad)
    vwp = jnp.pad(qkv[:, QKV_VW0:QKV_VW0 + N_GROUPS], pad)

    n_idx = np.arange(LANE)[:, None] * CMP_STRIDE
    j_idx = np.arange(LANE)[None, :] * SLC_BLOCK
    overlap = ((n_idx <= j_idx + SLC_BLOCK - 1) & (n_idx + CMP_BLOCK - 1 >= j_idx)
               & (np.arange(LANE)[None, :] < seq // SLC_BLOCK)
               & (np.arange(LANE)[:, None] < n_chunk - 1)).astype(np.float32)
    blk_onehot = (np.arange(LANE)[None, :] == D_QK + np.arange(seq)[:, None] // SLC_BLOCK)
    ks = qkv[:, QKV_KS0:QKV_KS0 + N_GROUPS] + jnp.asarray(blk_onehot, dtype=BF16)
    attn = _attention(qkv, ks, kwp, vwp, kc, vc, rest, bias_c, bias_t, bias_w, jnp.asarray(overlap))

    cvec = jnp.stack([conv_b[0], conv_ln_g[0], conv_ln_b[0]])
    gvec = jnp.stack([g_post_mix[0], g_pre_ffn[0]])
    x1, h2t = _mixer_tail(attn.reshape(t, N_HEADS * D_V), rest, x2d, mod_tail,
                          w_attn_out[0].astype(BF16), w_conv_out[0].astype(BF16),
                          w_out[0].astype(BF16), conv_w[0, :, 0, :], cvec, gvec, seq)

    wqt = peer_wq[0].T.astype(BF16)
    sk = peer_subkeys[0].reshape(2 * PEER_HEADS, PEER_NKEYS, PEER_DQ // 2).astype(BF16)
    f, cnt, g, rb = _route(h2t, wqt, sk)
    n_exp = peer_v.shape[1]
    vt = peer_v[0].reshape(n_exp // EXP_TE, EXP_TE, d).transpose(0, 2, 1).astype(BF16)
    out = _experts(h2t, peer_u[0].astype(BF16), vt, f, cnt, g, rb, x1, mod_out, g_post_ffn, seq)
    return out.reshape(bsz, seq, d)
```

```python
import functools
import math

import numpy as np
import jax
import jax.numpy as jnp
from jax import lax
from jax.experimental import pallas as pl
from jax.experimental.pallas import tpu as pltpu

F32 = jnp.float32
BF16 = jnp.bfloat16

D_MODEL = 1024
N_HEADS = 16
N_GROUPS = 4
HEADS_PER_GROUP = N_HEADS // N_GROUPS
D_QK = 96
D_V = 64
CMP_BLOCK = 32
CMP_STRIDE = 16
CMP_HIDDEN = 256
SLC_BLOCK = 64
SLC_TOPN = 16
WINDOW = 512
ATTN_SCALE = D_QK ** -0.5
FORCE_BONUS = 1e6
N_BUCKETS = 32
MAX_DISTANCE = 128
CONV_WIDTH = 31
PEER_HEADS = 8
PEER_NKEYS = 128
PEER_DQ = 256
PEER_TOPK = 16
NORM_EPS = 1e-6
NEG_INF = -1e30
LOG2E = math.log2(math.e)

LANE = 128
SUBLANE = 8
BF16_ROWS = 16
VMEM_LIMIT = 56 * 1024 * 1024

TQ = 256
SLC_CHUNK = 256
WIN_SPAN = TQ + WINDOW
PROJ_TM = 2048
PROJ_TN_HEADS = 1024
PROJ_TN_REST = 768
TAIL_TM = 256
HALO = 32
ROUTE_TL = 256
EXP_TL = 512
EXP_TE = 2048

QKV_Q0, QKV_KC0, QKV_KS0, QKV_KW0, QKV_VC0, QKV_VS0, QKV_VW0 = 0, 16, 20, 24, 28, 32, 36
QKV_HEADS = 40
REST_GLU_A, REST_GLU_G, REST_MRG_A, REST_MRG_B = 0, 1, 2, 3
REST_GATE0 = 4 * D_MODEL
REST_COLS = 4 * D_MODEL + N_GROUPS * LANE

NT_DIMS = (((1,), (1,)), ((), ()))


def _cparams(*sem):
    return pltpu.CompilerParams(dimension_semantics=sem, vmem_limit_bytes=VMEM_LIMIT)


def _gelu(x):
    return 0.5 * x * (1.0 + jnp.tanh(math.sqrt(2.0 / math.pi) * (x + 0.044715 * (x * x * x))))


def _gelu_x2(x):
    k1 = math.sqrt(2.0 / math.pi)
    return x * (1.0 + jnp.tanh(x * (k1 + (k1 * 0.044715) * (x * x))))


def _sigmoid(x):
    return 1.0 / (1.0 + jnp.exp(-x))


def _rms(x, g):
    return x * lax.rsqrt(jnp.mean(x * x, axis=-1, keepdims=True) + NORM_EPS) * g


def _ada_kernel(c_ref, w_ref, b_ref, o_ref):
    c = c_ref[...]
    c_act = c * _sigmoid(c)
    o_ref[...] = jnp.dot(c_act, w_ref[...], preferred_element_type=F32,
                         precision=lax.Precision.HIGHEST) + b_ref[...]


def _ada(c, w, b):
    bsz, d = c.shape
    n = w.shape[1]
    tn = 1024
    return pl.pallas_call(
        _ada_kernel,
        grid=(n // tn,),
        in_specs=[pl.BlockSpec((bsz, d), lambda j: (0, 0)),
                  pl.BlockSpec((d, tn), lambda j: (0, j)),
                  pl.BlockSpec((1, tn), lambda j: (0, j))],
        out_specs=pl.BlockSpec((bsz, tn), lambda j: (0, j)),
        out_shape=jax.ShapeDtypeStruct((bsz, n), F32),
        compiler_params=_cparams("parallel"),
        name="ada",
    )(c, w, b.reshape(1, n))


def _t5_bucket_np(dist):
    n = np.maximum(dist, 0)
    max_exact = N_BUCKETS // 2
    nf = np.maximum(n, 1).astype(np.float64)
    large = max_exact + (np.log(nf / max_exact) / math.log(MAX_DISTANCE / max_exact)
                         * (N_BUCKETS - max_exact)).astype(np.int64)
    return np.where(n < max_exact, n, np.minimum(large, N_BUCKETS - 1)).astype(np.int32)


def _bias_kernel(tab_ref, bkt_ref, o_ref):
    h = pl.program_id(0)
    bkt = bkt_ref[...]
    acc = jnp.where(bkt < 0, NEG_INF, 0.0).astype(F32)
    for i in range(N_BUCKETS):
        acc = jnp.where(bkt == i, tab_ref[i, h] * LOG2E, acc)
    o_ref[0] = acc


def _bias_table(rel_table, bucket):
    rows, cols = bucket.shape
    return pl.pallas_call(
        _bias_kernel,
        grid=(N_HEADS,),
        in_specs=[pl.BlockSpec(memory_space=pltpu.SMEM),
                  pl.BlockSpec((rows, cols), lambda h: (0, 0))],
        out_specs=pl.BlockSpec((1, rows, cols), lambda h: (h, 0, 0)),
        out_shape=jax.ShapeDtypeStruct((N_HEADS, rows, cols), F32),
        compiler_params=_cparams("parallel"),
        name="bias_table",
    )(rel_table, jnp.asarray(bucket))


def _static_buckets(seq):
    t = np.arange(seq)[:, None]
    n = np.arange(LANE)[None, :]
    dist_c = t - (n * CMP_STRIDE + CMP_BLOCK - 1)
    bkt_c = np.where(dist_c >= 0, _t5_bucket_np(dist_c), -1).astype(np.int32)
    qi = np.arange(TQ)[:, None]
    kj = np.arange(TQ)[None, :]
    tiles = [np.where(d * TQ + qi - kj >= 0, _t5_bucket_np(d * TQ + qi - kj), -1) for d in range(3)]
    bkt_t = np.concatenate(tiles, axis=0).astype(np.int32)
    kw = np.arange(WIN_SPAN)[None, :]
    dist_w = WINDOW + qi - kw
    band = (dist_w >= 0) & (dist_w < WINDOW)
    bkt_w = np.where(band, _t5_bucket_np(dist_w), -1).astype(np.int32)
    return bkt_c, bkt_t, bkt_w


def _proj_kernel(x_ref, mod_ref, g_ref, w_ref, add_ref, o_ref, h_ref, *, heads_out, tn):
    @pl.when(pl.program_id(1) == 0)
    def _():
        x = x_ref[...]
        sh = mod_ref[0, 0:1, :]
        sc = mod_ref[0, 1:2, :]
        h_ref[...] = (_rms(x, g_ref[...]) * (1.0 + sc) + sh).astype(BF16)

    res = (jnp.dot(h_ref[...], w_ref[...], preferred_element_type=F32) + add_ref[...]).astype(o_ref.dtype)
    if heads_out:
        for k in range(tn // LANE):
            o_ref[0, k] = res[:, k * LANE:(k + 1) * LANE]
    else:
        o_ref[...] = res


def _project(x2d, mod3, g, w, add, seq, heads_out):
    t, d = x2d.shape
    n = w.shape[1]
    bsz = t // seq
    tiles_per_seq = seq // PROJ_TM
    tn = PROJ_TN_HEADS if heads_out else PROJ_TN_REST
    assert n % tn == 0
    grid = (t // PROJ_TM, n // tn)
    if heads_out:
        hpt = tn // LANE
        out_shape = jax.ShapeDtypeStruct((bsz, n // LANE, seq, LANE), BF16)
        out_spec = pl.BlockSpec((1, hpt, PROJ_TM, LANE),
                                lambda i, j: (i // tiles_per_seq, j, i % tiles_per_seq, 0))
    else:
        out_shape = jax.ShapeDtypeStruct((t, n), BF16)
        out_spec = pl.BlockSpec((PROJ_TM, tn), lambda i, j: (i, j))
    return pl.pallas_call(
        functools.partial(_proj_kernel, heads_out=heads_out, tn=tn),
        grid=grid,
        in_specs=[pl.BlockSpec((PROJ_TM, d), lambda i, j: (i, 0)),
                  pl.BlockSpec((1, 2, d), lambda i, j: (i // tiles_per_seq, 0, 0)),
                  pl.BlockSpec((1, d), lambda i, j: (0, 0)),
                  pl.BlockSpec((d, tn), lambda i, j: (0, j)),
                  pl.BlockSpec((1, tn), lambda i, j: (0, j))],
        out_specs=out_spec,
        out_shape=out_shape,
        scratch_shapes=[pltpu.VMEM((PROJ_TM, d), BF16)],
        compiler_params=_cparams("parallel", "arbitrary"),
        name="proj_heads" if heads_out else "proj_rest",
    )(x2d, mod3, g, w, add)


def _compress_kernel(kch_ref, vch_ref, pk_ref, pv_ref, w1k_ref, w2k_ref, w1v_ref, w2v_ref,
                     kc_ref, vc_ref):
    def one(ch_ref, pos_ref, w1_ref, w2_ref, o_ref):
        a = ch_ref[0, 0].astype(F32)
        a1 = (a + pos_ref[0:1, :]).astype(BF16)
        a2 = (a + pos_ref[1:2, :]).astype(BF16)
        p1 = jnp.dot(a1, w1_ref[0], preferred_element_type=F32)
        p2 = jnp.dot(a2, w1_ref[1], preferred_element_type=F32)
        n_chunk = p2.shape[0]
        hid = _gelu(p1 + pltpu.roll(p2, n_chunk - 1, 0))
        o_ref[0, 0] = jnp.dot(hid.astype(BF16), w2_ref[...],
                              preferred_element_type=F32).astype(BF16)

    one(kch_ref, pk_ref, w1k_ref, w2k_ref, kc_ref)
    one(vch_ref, pv_ref, w1v_ref, w2v_ref, vc_ref)


def _compress(kch, vch, pk, pv, w1k, w2k, w1v, w2v):
    bsz, g, n_chunk, width = kch.shape
    assert n_chunk == LANE, "compressed keys are laid out on one 128-row tile"
    spec_in = pl.BlockSpec((1, 1, n_chunk, width), lambda b, gg: (b, gg, 0, 0))
    spec_out = pl.BlockSpec((1, 1, n_chunk, LANE), lambda b, gg: (b, gg, 0, 0))
    full = lambda a: pl.BlockSpec(a.shape, lambda b, gg: (0,) * a.ndim)
    out = jax.ShapeDtypeStruct((bsz, g, n_chunk, LANE), BF16)
    return pl.pallas_call(
        _compress_kernel,
        grid=(bsz, g),
        in_specs=[spec_in, spec_in, full(pk), full(pv), full(w1k), full(w2k), full(w1v), full(w2v)],
        out_specs=[spec_out, spec_out],
        out_shape=[out, out],
        compiler_params=_cparams("parallel", "parallel"),
        name="compress",
    )(kch, vch, pk, pv, w1k, w2k, w1v, w2v)


def _attn_kernel(q_ref, kc_ref, vc_ref, ks_ref, vs_ref, kw_ref, vw_ref, gate_ref,
                 bias_c_ref, bias_t_ref, bias_w_ref, overlap_ref,
                 o_ref, s_ref, p_ref, a_ref, acc_ref):
    qi = pl.program_id(2)
    rows = HEADS_PER_GROUP * TQ
    seq = ks_ref.shape[2]
    q4 = q_ref[0].reshape(rows, LANE)

    lc = lax.dot_general(q4, kc_ref[0, 0], NT_DIMS, preferred_element_type=F32)
    lc = lc + bias_c_ref[...].reshape(rows, LANE)
    valid = lc > 0.5 * NEG_INF
    mc = jnp.max(lc, axis=-1, keepdims=True)
    ec = jnp.where(valid, jnp.exp2(lc - mc), 0.0)
    sc = jnp.sum(ec, axis=-1, keepdims=True)
    pc = ec / jnp.where(sc > 0.0, sc, 1.0)
    o_cmp = jnp.dot(pc.astype(BF16), vc_ref[0, 0], preferred_element_type=F32)

    psum = jnp.sum(pc.reshape(HEADS_PER_GROUP, TQ, LANE), axis=0)
    imp = jnp.dot(psum, overlap_ref[...], preferred_element_type=F32,
                  precision=lax.Precision.HIGHEST)
    t_pos = qi * TQ + lax.broadcasted_iota(jnp.int32, (TQ, LANE), 0)
    j_blk = lax.broadcasted_iota(jnp.int32, (TQ, LANE), 1)
    cur = t_pos // SLC_BLOCK
    forced = (j_blk == 0) | (j_blk == cur) | (j_blk == cur - 1)
    score = jnp.where(j_blk <= cur, imp + jnp.where(forced, FORCE_BONUS, 0.0), NEG_INF)
    n_slc = seq // SLC_BLOCK
    sc_t = score.T[:n_slc]
    j_row = lax.broadcasted_iota(jnp.int32, (n_slc, TQ), 0)
    rank = jnp.zeros((n_slc, TQ), F32)
    for jp in range(n_slc):
        row = sc_t[jp:jp + 1, :]
        ge = jnp.where(row >= sc_t, 1.0, 0.0)
        gt = jnp.where(row > sc_t, 1.0, 0.0)
        rank = rank + jnp.where(j_row > jp, ge, gt)
    sel_t = jnp.where(rank < float(min(SLC_TOPN, n_slc)), 1.0, 0.0)
    assert D_QK + n_slc == LANE
    sel_add = jnp.concatenate([jnp.zeros((D_QK, TQ), F32), (sel_t - 1.0) * (-NEG_INF)], axis=0)
    q_sel = (q_ref[0] + sel_add.T.astype(BF16)[None]).reshape(rows, LANE)

    tiles_per_chunk = SLC_CHUNK // TQ

    def slc_step(c, carry):
        k0 = pl.multiple_of(c * SLC_CHUNK, SLC_CHUNK)
        kt = ks_ref[0, 0, pl.ds(k0, SLC_CHUNK), :]
        vt = vs_ref[0, 0, pl.ds(k0, SLC_CHUNK), :]
        s_ref[...] = lax.dot_general(q_sel, kt, NT_DIMS, preferred_element_type=F32)
        out = []
        for r in range(HEADS_PER_GROUP):
            m = carry[r]
            rsl = slice(r * TQ, (r + 1) * TQ)
            bias = jnp.concatenate(
                [bias_t_ref[jnp.clip(qi - (c * tiles_per_chunk + k), 0, 2), r]
                 for k in range(tiles_per_chunk)], axis=-1)
            s = s_ref[rsl, :] + bias
            m_new = jnp.maximum(m, jnp.max(s, axis=-1, keepdims=True))
            p_ref[rsl, :] = jnp.exp2(s - m_new).astype(BF16)
            a_ref[rsl, :] = jnp.broadcast_to(jnp.exp2(m - m_new), (TQ, LANE))
            out.append(m_new)
        acc_ref[...] = a_ref[...] * acc_ref[...] + jnp.dot(p_ref[...], vt, preferred_element_type=F32)
        return tuple(out)

    n_chunks = (qi + tiles_per_chunk) // tiles_per_chunk
    acc_ref[...] = jnp.zeros_like(acc_ref)
    init = tuple(jnp.full((TQ, 1), -jnp.inf, F32) for _ in range(HEADS_PER_GROUP))
    lax.fori_loop(0, n_chunks, slc_step, init)
    acc_s = acc_ref[...]
    o_slc = acc_s / pltpu.roll(acc_s, LANE - D_V, 1)

    w0 = pl.multiple_of(qi * TQ, TQ)
    kwin = kw_ref[0, 0, pl.ds(w0, WIN_SPAN), :]
    vwin = vw_ref[0, 0, pl.ds(w0, WIN_SPAN), :]
    sw = lax.dot_general(q4, kwin, NT_DIMS, preferred_element_type=F32)
    sw = sw + bias_w_ref[...].reshape(rows, WIN_SPAN)
    kj = lax.broadcasted_iota(jnp.int32, (1, WIN_SPAN), 1)
    sw = sw + jnp.where(kj >= WINDOW - qi * TQ, 0.0, NEG_INF)
    mw = jnp.max(sw, axis=-1, keepdims=True)
    pw = jnp.exp2(sw - mw)
    o_win = jnp.dot(pw.astype(BF16), vwin, preferred_element_type=F32)
    o_win = o_win / pltpu.roll(o_win, LANE - D_V, 1)

    gates = _sigmoid(gate_ref[...].astype(F32))
    outs = []
    for r in range(HEADS_PER_GROUP):
        sl = slice(r * TQ, (r + 1) * TQ)
        g_c = gates[:, r:r + 1]
        g_s = gates[:, HEADS_PER_GROUP + r:HEADS_PER_GROUP + r + 1]
        g_w = gates[:, 2 * HEADS_PER_GROUP + r:2 * HEADS_PER_GROUP + r + 1]
        o = g_c * o_cmp[sl] + g_s * o_slc[sl] + g_w * o_win[sl]
        outs.append(o[:, :D_V])
    o_ref[0] = jnp.concatenate(outs, axis=-1).astype(o_ref.dtype)


def _attention(qkv, ks, kwp, vwp, kc, vc, rest, bias_c, bias_t, bias_w, overlap):
    bsz, _, seq, _ = qkv.shape
    nq = seq // TQ
    g_cols0 = REST_GATE0 // LANE
    head = lambda base: pl.BlockSpec((1, 1, seq, LANE), lambda b, g, i: (b, base + g, 0, 0))
    in_specs = [
        pl.BlockSpec((1, HEADS_PER_GROUP, TQ, LANE), lambda b, g, i: (b, g, i, 0)),
        pl.BlockSpec((1, 1, LANE, LANE), lambda b, g, i: (b, g, 0, 0)),
        pl.BlockSpec((1, 1, LANE, LANE), lambda b, g, i: (b, g, 0, 0)),
        pl.BlockSpec((1, 1, seq, LANE), lambda b, g, i: (b, g, 0, 0)),
        head(QKV_VS0),
        pl.BlockSpec((1, 1, seq + WINDOW, LANE), lambda b, g, i: (b, g, 0, 0)),
        pl.BlockSpec((1, 1, seq + WINDOW, LANE), lambda b, g, i: (b, g, 0, 0)),
        pl.BlockSpec((TQ, LANE), lambda b, g, i: (b * nq + i, g_cols0 + g)),
        pl.BlockSpec((HEADS_PER_GROUP, TQ, LANE), lambda b, g, i: (g, i, 0)),
        pl.BlockSpec((3, HEADS_PER_GROUP, TQ, TQ), lambda b, g, i: (0, g, 0, 0)),
        pl.BlockSpec((HEADS_PER_GROUP, TQ, WIN_SPAN), lambda b, g, i: (g, 0, 0)),
        pl.BlockSpec((LANE, LANE), lambda b, g, i: (0, 0)),
    ]
    return pl.pallas_call(
        _attn_kernel,
        grid=(bsz, N_GROUPS, nq),
        in_specs=in_specs,
        out_specs=pl.BlockSpec((1, TQ, HEADS_PER_GROUP * D_V), lambda b, g, i: (b, i, g)),
        out_shape=jax.ShapeDtypeStruct((bsz, seq, N_HEADS * D_V), BF16),
        scratch_shapes=[pltpu.VMEM((HEADS_PER_GROUP * TQ, SLC_CHUNK), F32),
                        pltpu.VMEM((HEADS_PER_GROUP * TQ, SLC_CHUNK), BF16),
                        pltpu.VMEM((HEADS_PER_GROUP * TQ, LANE), F32),
                        pltpu.VMEM((HEADS_PER_GROUP * TQ, LANE), F32)],
        compiler_params=_cparams("parallel", "parallel", "arbitrary"),
        name="attention",
    )(qkv, kc, vc, ks, qkv, kwp, vwp, rest, bias_c, bias_t, bias_w, overlap)


def _tail_kernel(attn_ref, ga_ref, gg_ref, ha_ref, hg_ref, ma_ref, mb_ref, x_ref, mod_ref,
                 wa_ref, wc_ref, wo_ref, cw_ref, cvec_ref, gvec_ref,
                 x1_ref, h2t_ref, u_ref, ush_ref, *, tiles_per_seq):
    first = (pl.program_id(0) % tiles_per_seq) == 0
    y_a = jnp.dot(attn_ref[...], wa_ref[...], preferred_element_type=F32)

    halo = ha_ref[...].astype(F32) * _sigmoid(hg_ref[...].astype(F32))
    u_ref[0:HALO, :] = jnp.where(first, 0.0, halo)
    u_ref[HALO:, :] = ga_ref[...].astype(F32) * _sigmoid(gg_ref[...].astype(F32))

    conv_b, ln_g, ln_b = cvec_ref[0:1, :], cvec_ref[1:2, :], cvec_ref[2:3, :]
    n_sh = HALO + TAIL_TM - SUBLANE
    for r in range(1, SUBLANE):
        ush_ref[r - 1, 0:n_sh, :] = u_ref[r:r + n_sh, :]
    blk = 32
    pieces = []
    for rb in range(TAIL_TM // blk):
        acc = jnp.zeros((blk, D_MODEL), F32)
        for j in range(CONV_WIDTH):
            start = HALO - (CONV_WIDTH - 1) + rb * blk + j
            r, a = start % SUBLANE, start - start % SUBLANE
            win = u_ref[a:a + blk, :] if r == 0 else ush_ref[r - 1, a:a + blk, :]
            acc = acc + cw_ref[j:j + 1, :] * win
        pieces.append(acc)
    y = jnp.concatenate(pieces, axis=0) + conv_b
    yc = y - jnp.mean(y, axis=-1, keepdims=True)
    yn = yc * lax.rsqrt(jnp.mean(yc * yc, axis=-1, keepdims=True) + NORM_EPS) * ln_g + ln_b
    act = yn * _sigmoid(yn)
    y_b = jnp.dot(act.astype(BF16), wc_ref[...], preferred_element_type=F32)

    merged = (_sigmoid(ma_ref[...].astype(F32)) * y_a + _sigmoid(mb_ref[...].astype(F32)) * y_b)
    out = jnp.dot(merged.astype(BF16), wo_ref[...], preferred_element_type=F32)

    gt1, sh2, sc2 = mod_ref[0, 0:1, :], mod_ref[0, 1:2, :], mod_ref[0, 2:3, :]
    x1 = x_ref[...] + gt1 * _rms(out, gvec_ref[0:1, :])
    x1_ref[...] = x1
    h2 = _rms(x1, gvec_ref[1:2, :]) * (1.0 + sc2) + sh2
    h2t_ref[...] = h2.T.astype(BF16)


def _mixer_tail(attn2d, rest, x2d, mod3, wa, wc, wo, cw, cvec, gvec, seq):
    t, d = x2d.shape
    tiles_per_seq = seq // TAIL_TM
    hpt = TAIL_TM // HALO
    row = lambda cb: pl.BlockSpec((TAIL_TM, d), lambda i: (i, cb))
    halo = lambda cb: pl.BlockSpec((HALO, d), lambda i: (jnp.maximum(i * hpt - 1, 0), cb))
    full = lambda a: pl.BlockSpec(a.shape, lambda i: (0,) * a.ndim)
    return pl.pallas_call(
        functools.partial(_tail_kernel, tiles_per_seq=tiles_per_seq),
        grid=(t // TAIL_TM,),
        in_specs=[row(0), row(REST_GLU_A), row(REST_GLU_G), halo(REST_GLU_A), halo(REST_GLU_G),
                  row(REST_MRG_A), row(REST_MRG_B), row(0),
                  pl.BlockSpec((1, 3, d), lambda i: (i // tiles_per_seq, 0, 0)),
                  full(wa), full(wc), full(wo), full(cw), full(cvec), full(gvec)],
        out_specs=[row(0), pl.BlockSpec((d, TAIL_TM), lambda i: (0, i))],
        out_shape=[jax.ShapeDtypeStruct((t, d), F32), jax.ShapeDtypeStruct((d, t), BF16)],
        scratch_shapes=[pltpu.VMEM((HALO + TAIL_TM, d), F32),
                        pltpu.VMEM((SUBLANE - 1, HALO + TAIL_TM, d), F32)],
        compiler_params=_cparams("parallel"),
        name="mixer_tail",
    )(attn2d, rest, rest, rest, rest, rest, rest, x2d, mod3, wa, wc, wo, cw, cvec, gvec)


def _cand_pairs():
    return [(i, j) for i in range(PEER_TOPK) for j in range(PEER_TOPK)
            if (i + 1) * (j + 1) <= PEER_TOPK]


def _top16_rows(s):
    n = s.shape[0]
    kio = lax.broadcasted_iota(jnp.int32, s.shape, 0).astype(F32)
    vals, idxs = [], []
    for _ in range(PEER_TOPK):
        m = jnp.max(s, axis=0, keepdims=True)
        idx = jnp.min(jnp.where(s == m, kio, float(n)), axis=0, keepdims=True)
        vals.append(m)
        idxs.append(idx)
        s = jnp.where(kio == idx, -jnp.inf, s)
    return vals, idxs


def _sort16_network():
    def merge(lo, hi, r):
        step = r * 2
        if step < hi - lo:
            yield from merge(lo, hi, step)
            yield from merge(lo + r, hi, step)
            yield from [(i, i + r) for i in range(lo + r, hi - r, step)]
        else:
            yield (lo, lo + r)

    def sort(lo, hi):
        if hi - lo >= 1:
            mid = lo + (hi - lo) // 2
            yield from sort(lo, mid)
            yield from sort(mid + 1, hi)
            yield from merge(lo, hi, 1)

    return list(sort(0, PEER_TOPK - 1))


def _top16_values(s):
    n_grp = s.shape[0] // SUBLANE
    assert n_grp == PEER_TOPK
    lst = [s[SUBLANE * v:SUBLANE * (v + 1), :] for v in range(n_grp)]
    for a, b in _sort16_network():
        lst[a], lst[b] = jnp.maximum(lst[a], lst[b]), jnp.minimum(lst[a], lst[b])
    for shift in (4, 2, 1):
        other = [pltpu.roll(x, shift, 0) for x in lst]
        lst = [jnp.maximum(lst[j], other[n_grp - 1 - j]) for j in range(n_grp)]
        dist = n_grp // 2
        while dist >= 1:
            for j in range(n_grp):
                if (j // dist) % 2 == 0:
                    lst[j], lst[j + dist] = (jnp.maximum(lst[j], lst[j + dist]),
                                             jnp.minimum(lst[j], lst[j + dist]))
            dist //= 2
    return [x[0:1, :] for x in lst]


def _route_kernel(h2t_ref, wqt_ref, sk_ref, f_ref, cnt_ref, g_ref, rb_ref):
    q2t = jnp.dot(wqt_ref[...], h2t_ref[...], preferred_element_type=F32)
    tl = q2t.shape[1]
    kio = lax.broadcasted_iota(jnp.int32, (PEER_NKEYS, tl), 0).astype(F32)
    pairs = _cand_pairs()
    n_pad = -len(pairs) % 8
    flat_ids = np.array([i * PEER_TOPK + j for i, j in pairs] + [PEER_TOPK ** 2] * n_pad, np.float32)
    group_start = [min(k for k, (i, _) in enumerate(pairs) if i == ii) for ii in range(PEER_TOPK)]
    group_len = [sum(1 for (i, _) in pairs if i == ii) for ii in range(PEER_TOPK)]
    n_rows = len(pairs) + n_pad
    flat_col = lax.broadcasted_iota(jnp.int32, (n_rows, tl), 0)
    flat = jnp.zeros((n_rows, tl), F32)
    for k in range(n_rows):
        flat = jnp.where(flat_col == k, float(flat_ids[k]), flat)

    def head_scores(h):
        out = []
        for c in range(2):
            hc = 2 * h + c
            qt = q2t[hc * LANE:(hc + 1) * LANE, :].astype(BF16)
            out.append(jnp.dot(sk_ref[hc], qt, preferred_element_type=F32))
        return out

    def write_tables(h, scores, v1, v2, is_rank1, is_rank2):
        cand0 = jnp.concatenate([v1[i] + v2[j] for i, j in pairs]
                                + [jnp.full((n_pad, tl), -jnp.inf, F32)], axis=0)
        cand = cand0
        for _ in range(PEER_TOPK):
            m = jnp.max(cand, axis=0, keepdims=True)
            fid = jnp.min(jnp.where(cand == m, flat, float(PEER_TOPK ** 2 + 1)), axis=0, keepdims=True)
            cand = jnp.where(flat == fid, -jnp.inf, cand)
        picked = jnp.where((cand == -jnp.inf) & (flat < float(PEER_TOPK ** 2)), 1.0, 0.0)
        top = v1[0] + v2[0]
        z = jnp.sum(picked * jnp.exp(cand0 - top), axis=0, keepdims=True)
        counts = [jnp.sum(picked[group_start[i]:group_start[i] + group_len[i]], axis=0, keepdims=True)
                  for i in range(PEER_TOPK)]
        cnt = jnp.zeros((PEER_NKEYS, tl), F32)
        rank_b = jnp.full((PEER_NKEYS, tl), float(PEER_NKEYS), F32)
        for i in range(PEER_TOPK):
            cnt = jnp.where(is_rank1(i), counts[i], cnt)
            rank_b = jnp.where(is_rank2(i), float(i), rank_b)
        outs = ((f_ref, jnp.exp(scores[0] - v1[0]) * (0.5 / z)),
                (g_ref, jnp.exp(scores[1] - v2[0])), (cnt_ref, cnt), (rb_ref, rank_b))
        for ref, val in outs:
            for ch in range(tl // LANE):
                ref[h, ch] = val[:, ch * LANE:(ch + 1) * LANE].astype(ref.dtype)

    tied = jnp.zeros((1, tl), F32)
    for h in range(PEER_HEADS):
        scores = head_scores(h)
        tops = [_top16_values(sc) for sc in scores]
        for sc, v in zip(scores, tops):
            n_ge = jnp.sum(jnp.where(sc >= v[PEER_TOPK - 1], 1.0, 0.0), axis=0, keepdims=True)
            tied = jnp.maximum(tied, jnp.where(n_ge == float(PEER_TOPK), 0.0, 1.0))
            for i in range(PEER_TOPK - 1):
                tied = jnp.maximum(tied, jnp.where(v[i] > v[i + 1], 0.0, 1.0))
        write_tables(h, scores, tops[0], tops[1],
                     lambda i, sc=scores[0], v=tops[0]: sc == v[i],
                     lambda i, sc=scores[1], v=tops[1]: sc == v[i])

    @pl.when(jnp.max(tied) > 0.0)
    def _():
        for h in range(PEER_HEADS):
            scores = head_scores(h)
            (v1, i1), (v2, i2) = [_top16_rows(sc) for sc in scores]
            write_tables(h, scores, v1, v2, lambda i, i1=i1: kio == i1[i], lambda i, i2=i2: kio == i2[i])


def _route(h2t, wqt, sk):
    d, t = h2t.shape
    spec = pl.BlockSpec((PEER_HEADS, ROUTE_TL // LANE, PEER_NKEYS, LANE), lambda i: (0, i, 0, 0))
    out = jax.ShapeDtypeStruct((PEER_HEADS, t // LANE, PEER_NKEYS, LANE), F32)
    out16 = jax.ShapeDtypeStruct((PEER_HEADS, t // LANE, PEER_NKEYS, LANE), BF16)
    return pl.pallas_call(
        _route_kernel,
        grid=(t // ROUTE_TL,),
        in_specs=[pl.BlockSpec((d, ROUTE_TL), lambda i: (0, i)),
                  pl.BlockSpec(wqt.shape, lambda i: (0, 0)),
                  pl.BlockSpec(sk.shape, lambda i: (0, 0, 0))],
        out_specs=[spec, spec, spec, spec],
        out_shape=[out, out, out16, out16],
        compiler_params=_cparams("parallel"),
        name="peer_route",
    )(h2t, wqt, sk)


def _expert_kernel(h2t_ref, u_ref, vt_ref, f_ref, cnt_ref, g_ref, rb_ref, x1_ref, mod_ref, gp_ref,
                   o_ref, acc_ref, act_ref, coef_ref):
    e = pl.program_id(1)
    n_chunks = EXP_TL // LANE
    a_per_step = EXP_TE // PEER_NKEYS
    a_group = 2

    @pl.when(e == 0)
    def _():
        acc_ref[...] = jnp.zeros_like(acc_ref)

    act = jnp.dot(u_ref[...], h2t_ref[...], preferred_element_type=F32)
    for c in range(n_chunks):
        act_ref[c] = act[:, c * LANE:(c + 1) * LANE]

    def lane_chunk(c, carry):
        for ag in range(a_per_step // a_group):
            coefs = [jnp.zeros((PEER_NKEYS, LANE), BF16) for _ in range(a_group)]

            def row_tile(ref, h, a):
                row = jnp.broadcast_to(ref[h, c, pl.ds(a, 1), :], (BF16_ROWS, LANE)).astype(BF16)
                return jnp.broadcast_to(row[None], (PEER_NKEYS // BF16_ROWS, BF16_ROWS, LANE)
                                        ).reshape(PEER_NKEYS, LANE)

            for h in range(PEER_HEADS):
                rank_b = rb_ref[h, c]
                g_b = g_ref[h, c]
                for k in range(a_group):
                    a = e * a_per_step + ag * a_group + k
                    picked = jnp.maximum(jnp.minimum(row_tile(cnt_ref, h, a) - rank_b, g_b), 0.0)
                    coefs[k] = coefs[k] + row_tile(f_ref, h, a) * picked
            for k in range(a_group):
                rows = pl.ds((ag * a_group + k) * PEER_NKEYS, PEER_NKEYS)
                coef_ref[c, rows, :] = _gelu_x2(act_ref[c, rows, :].astype(BF16)) * coefs[k]
        return carry

    lax.fori_loop(0, n_chunks, lane_chunk, 0)
    coef = jnp.concatenate([coef_ref[c] for c in range(n_chunks)], axis=1)
    acc_ref[...] += jnp.dot(vt_ref[0], coef, preferred_element_type=F32)

    @pl.when(e == pl.num_programs(1) - 1)
    def _():
        y = acc_ref[...].T
        o_ref[...] = x1_ref[...] + mod_ref[0] * _rms(y, gp_ref[...])


def _experts(h2t, u, vt, f, cnt, g, rb, x1, gt2, gpost, seq):
    d, t = h2t.shape
    n_exp = u.shape[0]
    tiles_per_seq = seq // EXP_TL
    n_chunks = EXP_TL // LANE
    route = pl.BlockSpec((PEER_HEADS, n_chunks, PEER_NKEYS, LANE), lambda i, e: (0, i, 0, 0))
    return pl.pallas_call(
        _expert_kernel,
        grid=(t // EXP_TL, n_exp // EXP_TE),
        in_specs=[pl.BlockSpec((d, EXP_TL), lambda i, e: (0, i)),
                  pl.BlockSpec((EXP_TE, d), lambda i, e: (e, 0)),
                  pl.BlockSpec((1, d, EXP_TE), lambda i, e: (e, 0, 0)),
                  route, route, route, route,
                  pl.BlockSpec((EXP_TL, d), lambda i, e: (i, 0)),
                  pl.BlockSpec((1, 1, d), lambda i, e: (i // tiles_per_seq, 0, 0)),
                  pl.BlockSpec((1, d), lambda i, e: (0, 0))],
        out_specs=pl.BlockSpec((EXP_TL, d), lambda i, e: (i, 0)),
        out_shape=jax.ShapeDtypeStruct((t, d), F32),
        scratch_shapes=[pltpu.VMEM((d, EXP_TL), F32), pltpu.VMEM((n_chunks, EXP_TE, LANE), F32),
                        pltpu.VMEM((n_chunks, EXP_TE, LANE), BF16)],
        compiler_params=_cparams("parallel", "arbitrary"),
        name="peer_experts",
    )(h2t, u, vt, f, cnt, g, rb, x1, gt2, gpost)


def _pad_heads(w, n, width):
    d = w.shape[0]
    return jnp.pad(w.reshape(d, n, width), ((0, 0), (0, 0), (0, LANE - width))).reshape(d, n * LANE)


def _split_w_in(w_in):
    q_cols = N_HEADS * D_QK
    k_cols = N_GROUPS * D_QK
    v_cols = N_GROUPS * D_V
    sizes = (q_cols, k_cols, k_cols, k_cols, v_cols, v_cols, v_cols, 3 * N_HEADS, 2 * D_MODEL, 2 * D_MODEL)
    offs = np.cumsum((0,) + sizes)
    parts = [w_in[:, offs[k]:offs[k + 1]] for k in range(len(sizes))]
    wq, wkc, wks, wkw, wvc, wvs, wvw, wgate, wglu, wmerge = parts
    w_qkv = jnp.concatenate(
        [_pad_heads(wq * (ATTN_SCALE * LOG2E), N_HEADS, D_QK)]
        + [_pad_heads(w, N_GROUPS, D_QK) for w in (wkc, wks, wkw)]
        + [_pad_heads(w, N_GROUPS, D_V) for w in (wvc, wvs, wvw)], axis=1)
    d = w_in.shape[0]
    wg = wgate.reshape(d, 3, N_GROUPS, HEADS_PER_GROUP).transpose(0, 2, 1, 3)
    wg = wg.reshape(d, N_GROUPS, 3 * HEADS_PER_GROUP)
    wg = jnp.pad(wg, ((0, 0), (0, 0), (0, LANE - 3 * HEADS_PER_GROUP))).reshape(d, N_GROUPS * LANE)
    w_rest = jnp.concatenate([wglu, wmerge, wg], axis=1)
    return w_qkv.astype(BF16), w_rest.astype(BF16)


def _cmp_weights(w1, w2, pos, dh):
    hidden = w1.shape[1]
    w1p = jnp.pad(w1.reshape(CMP_BLOCK, dh, hidden), ((0, 0), (0, LANE - dh), (0, 0)))
    w1p = w1p.reshape(2, CMP_STRIDE * LANE, hidden).astype(BF16)
    w2p = jnp.pad(w2, ((0, 0), (0, LANE - dh))).astype(BF16)
    posp = jnp.pad(pos, ((0, 0), (0, LANE - dh))).reshape(2, CMP_STRIDE * LANE)
    return w1p, w2p, posp


def kernel(x, c, w_ada, b_ada, g_pre_mix, g_post_mix, g_pre_ffn, g_post_ffn, rel_table, w_in,
           cmp_w1k, cmp_w2k, cmp_pos_k, cmp_w1v, cmp_w2v, cmp_pos_v, w_attn_out,
           conv_w, conv_b, conv_ln_g, conv_ln_b, w_conv_out, w_out,
           peer_wq, peer_subkeys, peer_u, peer_v):
    bsz, seq, d = x.shape
    t = bsz * seq
    assert d == D_MODEL and w_ada.shape[0] == 1, "single-layer block with D_MODEL channels"
    assert seq % PROJ_TM == 0 and seq // CMP_STRIDE == LANE and seq % EXP_TL == 0
    x2d = x.reshape(t, d)

    mod = _ada(c, w_ada[0], b_ada[0]).reshape(bsz, 6, d)
    mod_in = mod[:, 0:2]
    mod_tail = mod[:, 2:5]
    mod_out = mod[:, 5:6]

    bkt_c, bkt_t, bkt_w = _static_buckets(seq)
    bias_c = _bias_table(rel_table, bkt_c)
    bias_t = _bias_table(rel_table, bkt_t).reshape(N_HEADS, 3, TQ, TQ).transpose(1, 0, 2, 3)
    bias_w = _bias_table(rel_table, bkt_w)

    w_qkv, w_rest = _split_w_in(w_in[0])
    add_qkv = np.zeros((QKV_HEADS, LANE), np.float32)
    add_qkv[QKV_VS0:QKV_VS0 + N_GROUPS, D_V:] = 1.0
    add_qkv[QKV_VW0:QKV_VW0 + N_GROUPS, D_V:] = 1.0
    qkv = _project(x2d, mod_in, g_pre_mix, w_qkv, jnp.asarray(add_qkv.reshape(1, -1)), seq,
                   heads_out=True)
    rest = _project(x2d, mod_in, g_pre_mix, w_rest, jnp.zeros((1, REST_COLS), F32), seq,
                    heads_out=False)

    n_chunk = seq // CMP_STRIDE
    kch = qkv[:, QKV_KC0:QKV_KC0 + N_GROUPS].reshape(bsz, N_GROUPS, n_chunk, CMP_STRIDE * LANE)
    vch = qkv[:, QKV_VC0:QKV_VC0 + N_GROUPS].reshape(bsz, N_GROUPS, n_chunk, CMP_STRIDE * LANE)
    w1k, w2k, pk = _cmp_weights(cmp_w1k[0], cmp_w2k[0], cmp_pos_k[0], D_QK)
    w1v, w2v, pv = _cmp_weights(cmp_w1v[0], cmp_w2v[0], cmp_pos_v[0], D_V)
    kc, vc = _compress(kch, vch, pk, pv, w1k, w2k, w1v, w2v)

    pad = ((0, 0), (0, 0), (WINDOW, 0), (0, 0))
    kwp = jnp.pad(qkv[:, QKV_KW0:QKV_KW0 + N_GROUPS], pad)
    vwp = jnp.pad(qkv[:, QKV_VW0:QKV_VW0 + N_GROUPS], pad)

    n_idx = np.arange(LANE)[:, None] * CMP_STRIDE
    j_idx = np.arange(LANE)[None, :] * SLC_BLOCK
    overlap = ((n_idx <= j_idx + SLC_BLOCK - 1) & (n_idx + CMP_BLOCK - 1 >= j_idx)
               & (np.arange(LANE)[None, :] < seq // SLC_BLOCK)
               & (np.arange(LANE)[:, None] < n_chunk - 1)).astype(np.float32)
    blk_onehot = (np.arange(LANE)[None, :] == D_QK + np.arange(seq)[:, None] // SLC_BLOCK)
    ks = qkv[:, QKV_KS0:QKV_KS0 + N_GROUPS] + jnp.asarray(blk_onehot, dtype=BF16)
    attn = _attention(qkv, ks, kwp, vwp, kc, vc, rest, bias_c, bias_t, bias_w, jnp.asarray(overlap))

    cvec = jnp.stack([conv_b[0], conv_ln_g[0], conv_ln_b[0]])
    gvec = jnp.stack([g_post_mix[0], g_pre_ffn[0]])
    x1, h2t = _mixer_tail(attn.reshape(t, N_HEADS * D_V), rest, x2d, mod_tail,
                          w_attn_out[0].astype(BF16), w_conv_out[0].astype(BF16),
                          w_out[0].astype(BF16), conv_w[0, :, 0, :], cvec, gvec, seq)

    wqt = peer_wq[0].T.astype(BF16)
    sk = peer_subkeys[0].reshape(2 * PEER_HEADS, PEER_NKEYS, PEER_DQ // 2).astype(BF16)
    f, cnt, g, rb = _route(h2t, wqt, sk)
    n_exp = peer_v.shape[1]
    vt = peer_v[0].reshape(n_exp // EXP_TE, EXP_TE, d).transpose(0, 2, 1).astype(BF16)
    out = _experts(h2t, peer_u[0].astype(BF16), vt, f, cnt, g, rb, x1, mod_out, g_post_ffn, seq)
    return out.reshape(bsz, seq, d)
```

```python
import functools
import math

import numpy as np
import jax
import jax.numpy as jnp
from jax import lax
from jax.experimental import pallas as pl
from jax.experimental.pallas import tpu as pltpu

F32 = jnp.float32
BF16 = jnp.bfloat16

D_MODEL = 1024
N_HEADS = 16
N_GROUPS = 4
HEADS_PER_GROUP = N_HEADS // N_GROUPS
D_QK = 96
D_V = 64
CMP_BLOCK = 32
CMP_STRIDE = 16
CMP_HIDDEN = 256
SLC_BLOCK = 64
SLC_TOPN = 16
WINDOW = 512
ATTN_SCALE = D_QK ** -0.5
FORCE_BONUS = 1e6
N_BUCKETS = 32
MAX_DISTANCE = 128
CONV_WIDTH = 31
PEER_HEADS = 8
PEER_NKEYS = 128
PEER_DQ = 256
PEER_TOPK = 16
NORM_EPS = 1e-6
NEG_INF = -1e30
LOG2E = math.log2(math.e)

LANE = 128
SUBLANE = 8
BF16_ROWS = 16
VMEM_LIMIT = 56 * 1024 * 1024

TQ = 256
SLC_CHUNK = 512
WIN_SPAN = TQ + WINDOW
N_BIAS_TILES = 4
PROJ_TM = 2048
PROJ_TN_HEADS = 1024
PROJ_TN_REST = 768
TAIL_TM = 512
HALO = 32
ROUTE_TL = 256
EXP_TL = 512
EXP_TE = 2048

QKV_Q0, QKV_KC0, QKV_KS0, QKV_KW0, QKV_VC0, QKV_VS0, QKV_VW0 = 0, 16, 20, 24, 28, 32, 36
QKV_HEADS = 40
REST_GLU_A, REST_GLU_G, REST_MRG_A, REST_MRG_B = 0, 1, 2, 3
REST_GATE0 = 4 * D_MODEL
REST_COLS = 4 * D_MODEL + N_GROUPS * LANE

NT_DIMS = (((1,), (1,)), ((), ()))


def _cparams(*sem):
    return pltpu.CompilerParams(dimension_semantics=sem, vmem_limit_bytes=VMEM_LIMIT)


def _gelu(x):
    return 0.5 * x * (1.0 + jnp.tanh(math.sqrt(2.0 / math.pi) * (x + 0.044715 * (x * x * x))))


def _gelu_x2(x):
    k1 = math.sqrt(2.0 / math.pi)
    return x * (1.0 + jnp.tanh(x * (k1 + (k1 * 0.044715) * (x * x))))


def _sigmoid(x):
    return 1.0 / (1.0 + jnp.exp(-x))


def _rms(x, g):
    return x * lax.rsqrt(jnp.mean(x * x, axis=-1, keepdims=True) + NORM_EPS) * g


def _ada_kernel(c_ref, w_ref, b_ref, o_ref):
    c = c_ref[...]
    c_act = c * _sigmoid(c)
    o_ref[...] = jnp.dot(c_act, w_ref[...], preferred_element_type=F32,
                         precision=lax.Precision.HIGHEST) + b_ref[...]


def _ada(c, w, b):
    bsz, d = c.shape
    n = w.shape[1]
    tn = 1024
    return pl.pallas_call(
        _ada_kernel,
        grid=(n // tn,),
        in_specs=[pl.BlockSpec((bsz, d), lambda j: (0, 0)),
                  pl.BlockSpec((d, tn), lambda j: (0, j)),
                  pl.BlockSpec((1, tn), lambda j: (0, j))],
        out_specs=pl.BlockSpec((bsz, tn), lambda j: (0, j)),
        out_shape=jax.ShapeDtypeStruct((bsz, n), F32),
        compiler_params=_cparams("parallel"),
        name="ada",
    )(c, w, b.reshape(1, n))


def _t5_bucket_np(dist):
    n = np.maximum(dist, 0)
    max_exact = N_BUCKETS // 2
    nf = np.maximum(n, 1).astype(np.float64)
    large = max_exact + (np.log(nf / max_exact) / math.log(MAX_DISTANCE / max_exact)
                         * (N_BUCKETS - max_exact)).astype(np.int64)
    return np.where(n < max_exact, n, np.minimum(large, N_BUCKETS - 1)).astype(np.int32)


def _bias_kernel(tab_ref, bkt_ref, o_ref):
    h = pl.program_id(0)
    bkt = bkt_ref[...]
    acc = jnp.where(bkt < 0, NEG_INF, 0.0).astype(F32)
    for i in range(N_BUCKETS):
        acc = jnp.where(bkt == i, tab_ref[i, h] * LOG2E, acc)
    o_ref[0] = acc


def _bias_table(rel_table, bucket):
    rows, cols = bucket.shape
    return pl.pallas_call(
        _bias_kernel,
        grid=(N_HEADS,),
        in_specs=[pl.BlockSpec(memory_space=pltpu.SMEM),
                  pl.BlockSpec((rows, cols), lambda h: (0, 0))],
        out_specs=pl.BlockSpec((1, rows, cols), lambda h: (h, 0, 0)),
        out_shape=jax.ShapeDtypeStruct((N_HEADS, rows, cols), F32),
        compiler_params=_cparams("parallel"),
        name="bias_table",
    )(rel_table, jnp.asarray(bucket))


def _static_buckets(seq):
    t = np.arange(seq)[:, None]
    n = np.arange(LANE)[None, :]
    dist_c = t - (n * CMP_STRIDE + CMP_BLOCK - 1)
    bkt_c = np.where(dist_c >= 0, _t5_bucket_np(dist_c), -1).astype(np.int32)
    qi = np.arange(TQ)[:, None]
    kj = np.arange(TQ)[None, :]
    tiles = [np.where(d * TQ + qi - kj >= 0, _t5_bucket_np(d * TQ + qi - kj), -1) for d in range(3)]
    tiles.append(np.full((TQ, TQ), -1))
    bkt_t = np.concatenate(tiles, axis=0).astype(np.int32)
    kw = np.arange(WIN_SPAN)[None, :]
    dist_w = WINDOW + qi - kw
    band = (dist_w >= 0) & (dist_w < WINDOW)
    bkt_w = np.where(band, _t5_bucket_np(dist_w), -1).astype(np.int32)
    return bkt_c, bkt_t, bkt_w


def _proj_kernel(x_ref, mod_ref, g_ref, w_ref, add_ref, o_ref, h_ref, *, heads_out, tn):
    @pl.when(pl.program_id(1) == 0)
    def _():
        x = x_ref[...]
        sh = mod_ref[0, 0:1, :]
        sc = mod_ref[0, 1:2, :]
        h_ref[...] = (_rms(x, g_ref[...]) * (1.0 + sc) + sh).astype(BF16)

    res = (jnp.dot(h_ref[...], w_ref[...], preferred_element_type=F32) + add_ref[...]).astype(o_ref.dtype)
    if heads_out:
        for k in range(tn // LANE):
            o_ref[0, k] = res[:, k * LANE:(k + 1) * LANE]
    else:
        o_ref[...] = res


def _project(x2d, mod3, g, w, add, seq, heads_out):
    t, d = x2d.shape
    n = w.shape[1]
    bsz = t // seq
    tiles_per_seq = seq // PROJ_TM
    tn = PROJ_TN_HEADS if heads_out else PROJ_TN_REST
    assert n % tn == 0
    grid = (t // PROJ_TM, n // tn)
    if heads_out:
        hpt = tn // LANE
        out_shape = jax.ShapeDtypeStruct((bsz, n // LANE, seq, LANE), BF16)
        out_spec = pl.BlockSpec((1, hpt, PROJ_TM, LANE),
                                lambda i, j: (i // tiles_per_seq, j, i % tiles_per_seq, 0))
    else:
        out_shape = jax.ShapeDtypeStruct((t, n), BF16)
        out_spec = pl.BlockSpec((PROJ_TM, tn), lambda i, j: (i, j))
    return pl.pallas_call(
        functools.partial(_proj_kernel, heads_out=heads_out, tn=tn),
        grid=grid,
        in_specs=[pl.BlockSpec((PROJ_TM, d), lambda i, j: (i, 0)),
                  pl.BlockSpec((1, 2, d), lambda i, j: (i // tiles_per_seq, 0, 0)),
                  pl.BlockSpec((1, d), lambda i, j: (0, 0)),
                  pl.BlockSpec((d, tn), lambda i, j: (0, j)),
                  pl.BlockSpec((1, tn), lambda i, j: (0, j))],
        out_specs=out_spec,
        out_shape=out_shape,
        scratch_shapes=[pltpu.VMEM((PROJ_TM, d), BF16)],
        compiler_params=_cparams("parallel", "arbitrary"),
        name="proj_heads" if heads_out else "proj_rest",
    )(x2d, mod3, g, w, add)


def _compress_kernel(kch_ref, vch_ref, pk_ref, pv_ref, w1k_ref, w2k_ref, w1v_ref, w2v_ref,
                     kc_ref, vc_ref):
    def one(ch_ref, pos_ref, w1_ref, w2_ref, o_ref):
        a = ch_ref[0, 0].astype(F32)
        a1 = (a + pos_ref[0:1, :]).astype(BF16)
        a2 = (a + pos_ref[1:2, :]).astype(BF16)
        p1 = jnp.dot(a1, w1_ref[0], preferred_element_type=F32)
        p2 = jnp.dot(a2, w1_ref[1], preferred_element_type=F32)
        n_chunk = p2.shape[0]
        hid = _gelu(p1 + pltpu.roll(p2, n_chunk - 1, 0))
        o_ref[0, 0] = jnp.dot(hid.astype(BF16), w2_ref[...],
                              preferred_element_type=F32).astype(BF16)

    one(kch_ref, pk_ref, w1k_ref, w2k_ref, kc_ref)
    one(vch_ref, pv_ref, w1v_ref, w2v_ref, vc_ref)


def _compress(kch, vch, pk, pv, w1k, w2k, w1v, w2v):
    bsz, g, n_chunk, width = kch.shape
    assert n_chunk == LANE, "compressed keys are laid out on one 128-row tile"
    spec_in = pl.BlockSpec((1, 1, n_chunk, width), lambda b, gg: (b, gg, 0, 0))
    spec_out = pl.BlockSpec((1, 1, n_chunk, LANE), lambda b, gg: (b, gg, 0, 0))
    full = lambda a: pl.BlockSpec(a.shape, lambda b, gg: (0,) * a.ndim)
    out = jax.ShapeDtypeStruct((bsz, g, n_chunk, LANE), BF16)
    return pl.pallas_call(
        _compress_kernel,
        grid=(bsz, g),
        in_specs=[spec_in, spec_in, full(pk), full(pv), full(w1k), full(w2k), full(w1v), full(w2v)],
        out_specs=[spec_out, spec_out],
        out_shape=[out, out],
        compiler_params=_cparams("parallel", "parallel"),
        name="compress",
    )(kch, vch, pk, pv, w1k, w2k, w1v, w2v)


def _attn_kernel(q_ref, kc_ref, vc_ref, ks_ref, vs_ref, kw_ref, vw_ref, gate_ref,
                 bias_c_ref, bias_t_ref, bias_w_ref, overlap_ref,
                 o_ref, s_ref, p_ref, a_ref, acc_ref):
    qi = pl.program_id(2)
    rows = HEADS_PER_GROUP * TQ
    seq = ks_ref.shape[2]
    q4 = q_ref[0].reshape(rows, LANE)

    lc = lax.dot_general(q4, kc_ref[0, 0], NT_DIMS, preferred_element_type=F32)
    lc = lc + bias_c_ref[...].reshape(rows, LANE)
    valid = lc > 0.5 * NEG_INF
    mc = jnp.max(lc, axis=-1, keepdims=True)
    ec = jnp.where(valid, jnp.exp2(lc - mc), 0.0)
    sc = jnp.sum(ec, axis=-1, keepdims=True)
    pc = ec / jnp.where(sc > 0.0, sc, 1.0)
    o_cmp = jnp.dot(pc.astype(BF16), vc_ref[0, 0], preferred_element_type=F32)

    psum = jnp.sum(pc.reshape(HEADS_PER_GROUP, TQ, LANE), axis=0)
    imp = jnp.dot(psum, overlap_ref[...], preferred_element_type=F32,
                  precision=lax.Precision.HIGHEST)
    t_pos = qi * TQ + lax.broadcasted_iota(jnp.int32, (TQ, LANE), 0)
    j_blk = lax.broadcasted_iota(jnp.int32, (TQ, LANE), 1)
    cur = t_pos // SLC_BLOCK
    forced = (j_blk == 0) | (j_blk == cur) | (j_blk == cur - 1)
    score = jnp.where(j_blk <= cur, imp + jnp.where(forced, FORCE_BONUS, 0.0), NEG_INF)
    n_slc = seq // SLC_BLOCK
    sc_t = score.T[:n_slc]
    j_row = lax.broadcasted_iota(jnp.int32, (n_slc, TQ), 0)
    rank = jnp.zeros((n_slc, TQ), F32)
    for jp in range(n_slc):
        row = sc_t[jp:jp + 1, :]
        ge = jnp.where(row >= sc_t, 1.0, 0.0)
        gt = jnp.where(row > sc_t, 1.0, 0.0)
        rank = rank + jnp.where(j_row > jp, ge, gt)
    sel_t = jnp.where(rank < float(min(SLC_TOPN, n_slc)), 1.0, 0.0)
    assert D_QK + n_slc == LANE
    sel_add = jnp.concatenate([jnp.zeros((D_QK, TQ), F32), (sel_t - 1.0) * (-NEG_INF)], axis=0)
    q_sel = (q_ref[0] + sel_add.T.astype(BF16)[None]).reshape(rows, LANE)

    tiles_per_chunk = SLC_CHUNK // TQ

    def slc_step(c, carry):
        k0 = pl.multiple_of(c * SLC_CHUNK, SLC_CHUNK)
        kt = ks_ref[0, 0, pl.ds(k0, SLC_CHUNK), :]
        vt = vs_ref[0, 0, pl.ds(k0, SLC_CHUNK), :]
        s_ref[...] = lax.dot_general(q_sel, kt, NT_DIMS, preferred_element_type=F32)
        behind = [qi - (c * tiles_per_chunk + k) for k in range(tiles_per_chunk)]
        tile_ids = [jnp.where(d < 0, N_BIAS_TILES - 1, jnp.minimum(d, N_BIAS_TILES - 2)) for d in behind]
        out = []
        for r in range(HEADS_PER_GROUP):
            m = carry[r]
            rsl = slice(r * TQ, (r + 1) * TQ)
            bias = jnp.concatenate([bias_t_ref[tile_ids[k], r] for k in range(tiles_per_chunk)],
                                   axis=-1)
            s = s_ref[rsl, :] + bias
            m_new = jnp.maximum(m, jnp.max(s, axis=-1, keepdims=True))
            p_ref[rsl, :] = jnp.exp2(s - m_new).astype(BF16)
            a_ref[rsl, :] = jnp.broadcast_to(jnp.exp2(m - m_new), (TQ, LANE))
            out.append(m_new)
        acc_ref[...] = a_ref[...] * acc_ref[...] + jnp.dot(p_ref[...], vt, preferred_element_type=F32)
        return tuple(out)

    n_chunks = (qi + tiles_per_chunk) // tiles_per_chunk
    acc_ref[...] = jnp.zeros_like(acc_ref)
    init = tuple(jnp.full((TQ, 1), -jnp.inf, F32) for _ in range(HEADS_PER_GROUP))
    lax.fori_loop(0, n_chunks, slc_step, init)
    acc_s = acc_ref[...]
    o_slc = acc_s / pltpu.roll(acc_s, LANE - D_V, 1)

    w0 = pl.multiple_of(qi * TQ, TQ)
    kwin = kw_ref[0, 0, pl.ds(w0, WIN_SPAN), :]
    vwin = vw_ref[0, 0, pl.ds(w0, WIN_SPAN), :]
    sw = lax.dot_general(q4, kwin, NT_DIMS, preferred_element_type=F32)
    sw = sw + bias_w_ref[...].reshape(rows, WIN_SPAN)
    kj = lax.broadcasted_iota(jnp.int32, (1, WIN_SPAN), 1)
    sw = sw + jnp.where(kj >= WINDOW - qi * TQ, 0.0, NEG_INF)
    mw = jnp.max(sw, axis=-1, keepdims=True)
    pw = jnp.exp2(sw - mw)
    o_win = jnp.dot(pw.astype(BF16), vwin, preferred_element_type=F32)
    o_win = o_win / pltpu.roll(o_win, LANE - D_V, 1)

    gates = _sigmoid(gate_ref[...].astype(F32))
    outs = []
    for r in range(HEADS_PER_GROUP):
        sl = slice(r * TQ, (r + 1) * TQ)
        g_c = gates[:, r:r + 1]
        g_s = gates[:, HEADS_PER_GROUP + r:HEADS_PER_GROUP + r + 1]
        g_w = gates[:, 2 * HEADS_PER_GROUP + r:2 * HEADS_PER_GROUP + r + 1]
        o = g_c * o_cmp[sl] + g_s * o_slc[sl] + g_w * o_win[sl]
        outs.append(o[:, :D_V])
    o_ref[0] = jnp.concatenate(outs, axis=-1).astype(o_ref.dtype)


def _attention(qkv, ks, kwp, vwp, kc, vc, rest, bias_c, bias_t, bias_w, overlap):
    bsz, _, seq, _ = qkv.shape
    nq = seq // TQ
    g_cols0 = REST_GATE0 // LANE
    head = lambda base: pl.BlockSpec((1, 1, seq, LANE), lambda b, g, i: (b, base + g, 0, 0))
    in_specs = [
        pl.BlockSpec((1, HEADS_PER_GROUP, TQ, LANE), lambda b, g, i: (b, g, i, 0)),
        pl.BlockSpec((1, 1, LANE, LANE), lambda b, g, i: (b, g, 0, 0)),
        pl.BlockSpec((1, 1, LANE, LANE), lambda b, g, i: (b, g, 0, 0)),
        pl.BlockSpec((1, 1, seq, LANE), lambda b, g, i: (b, g, 0, 0)),
        head(QKV_VS0),
        pl.BlockSpec((1, 1, seq + WINDOW, LANE), lambda b, g, i: (b, g, 0, 0)),
        pl.BlockSpec((1, 1, seq + WINDOW, LANE), lambda b, g, i: (b, g, 0, 0)),
        pl.BlockSpec((TQ, LANE), lambda b, g, i: (b * nq + i, g_cols0 + g)),
        pl.BlockSpec((HEADS_PER_GROUP, TQ, LANE), lambda b, g, i: (g, i, 0)),
        pl.BlockSpec((N_BIAS_TILES, HEADS_PER_GROUP, TQ, TQ), lambda b, g, i: (0, g, 0, 0)),
        pl.BlockSpec((HEADS_PER_GROUP, TQ, WIN_SPAN), lambda b, g, i: (g, 0, 0)),
        pl.BlockSpec((LANE, LANE), lambda b, g, i: (0, 0)),
    ]
    return pl.pallas_call(
        _attn_kernel,
        grid=(bsz, N_GROUPS, nq),
        in_specs=in_specs,
        out_specs=pl.BlockSpec((1, TQ, HEADS_PER_GROUP * D_V), lambda b, g, i: (b, i, g)),
        out_shape=jax.ShapeDtypeStruct((bsz, seq, N_HEADS * D_V), BF16),
        scratch_shapes=[pltpu.VMEM((HEADS_PER_GROUP * TQ, SLC_CHUNK), F32),
                        pltpu.VMEM((HEADS_PER_GROUP * TQ, SLC_CHUNK), BF16),
                        pltpu.VMEM((HEADS_PER_GROUP * TQ, LANE), F32),
                        pltpu.VMEM((HEADS_PER_GROUP * TQ, LANE), F32)],
        compiler_params=_cparams("parallel", "parallel", "arbitrary"),
        name="attention",
    )(qkv, kc, vc, ks, qkv, kwp, vwp, rest, bias_c, bias_t, bias_w, overlap)


def _tail_kernel(attn_ref, ga_ref, gg_ref, ha_ref, hg_ref, ma_ref, mb_ref, x_ref, mod_ref,
                 wa_ref, wc_ref, wo_ref, cw_ref, cvec_ref, gvec_ref,
                 x1_ref, h2t_ref, u_ref, ush_ref, *, tiles_per_seq):
    first = (pl.program_id(0) % tiles_per_seq) == 0
    y_a = jnp.dot(attn_ref[...], wa_ref[...], preferred_element_type=F32)

    halo = ha_ref[...].astype(F32) * _sigmoid(hg_ref[...].astype(F32))
    u_ref[0:HALO, :] = jnp.where(first, 0.0, halo)
    u_ref[HALO:, :] = ga_ref[...].astype(F32) * _sigmoid(gg_ref[...].astype(F32))

    conv_b, ln_g, ln_b = cvec_ref[0:1, :], cvec_ref[1:2, :], cvec_ref[2:3, :]
    n_sh = HALO + TAIL_TM - SUBLANE
    for r in range(1, SUBLANE):
        ush_ref[r - 1, 0:n_sh, :] = u_ref[r:r + n_sh, :]
    blk = 32
    pieces = []
    for rb in range(TAIL_TM // blk):
        acc = jnp.zeros((blk, D_MODEL), F32)
        for j in range(CONV_WIDTH):
            start = HALO - (CONV_WIDTH - 1) + rb * blk + j
            r, a = start % SUBLANE, start - start % SUBLANE
            win = u_ref[a:a + blk, :] if r == 0 else ush_ref[r - 1, a:a + blk, :]
            acc = acc + cw_ref[j:j + 1, :] * win
        pieces.append(acc)
    y = jnp.concatenate(pieces, axis=0) + conv_b
    yc = y - jnp.mean(y, axis=-1, keepdims=True)
    yn = yc * lax.rsqrt(jnp.mean(yc * yc, axis=-1, keepdims=True) + NORM_EPS) * ln_g + ln_b
    act = yn * _sigmoid(yn)
    y_b = jnp.dot(act.astype(BF16), wc_ref[...], preferred_element_type=F32)

    merged = (_sigmoid(ma_ref[...].astype(F32)) * y_a + _sigmoid(mb_ref[...].astype(F32)) * y_b)
    out = jnp.dot(merged.astype(BF16), wo_ref[...], preferred_element_type=F32)

    gt1, sh2, sc2 = mod_ref[0, 0:1, :], mod_ref[0, 1:2, :], mod_ref[0, 2:3, :]
    x1 = x_ref[...] + gt1 * _rms(out, gvec_ref[0:1, :])
    x1_ref[...] = x1
    h2 = _rms(x1, gvec_ref[1:2, :]) * (1.0 + sc2) + sh2
    h2t_ref[...] = h2.T.astype(BF16)


def _mixer_tail(attn2d, rest, x2d, mod3, wa, wc, wo, cw, cvec, gvec, seq):
    t, d = x2d.shape
    tiles_per_seq = seq // TAIL_TM
    hpt = TAIL_TM // HALO
    row = lambda cb: pl.BlockSpec((TAIL_TM, d), lambda i: (i, cb))
    halo = lambda cb: pl.BlockSpec((HALO, d), lambda i: (jnp.maximum(i * hpt - 1, 0), cb))
    full = lambda a: pl.BlockSpec(a.shape, lambda i: (0,) * a.ndim)
    return pl.pallas_call(
        functools.partial(_tail_kernel, tiles_per_seq=tiles_per_seq),
        grid=(t // TAIL_TM,),
        in_specs=[row(0), row(REST_GLU_A), row(REST_GLU_G), halo(REST_GLU_A), halo(REST_GLU_G),
                  row(REST_MRG_A), row(REST_MRG_B), row(0),
                  pl.BlockSpec((1, 3, d), lambda i: (i // tiles_per_seq, 0, 0)),
                  full(wa), full(wc), full(wo), full(cw), full(cvec), full(gvec)],
        out_specs=[row(0), pl.BlockSpec((d, TAIL_TM), lambda i: (0, i))],
        out_shape=[jax.ShapeDtypeStruct((t, d), F32), jax.ShapeDtypeStruct((d, t), BF16)],
        scratch_shapes=[pltpu.VMEM((HALO + TAIL_TM, d), F32),
                        pltpu.VMEM((SUBLANE - 1, HALO + TAIL_TM, d), F32)],
        compiler_params=_cparams("parallel"),
        name="mixer_tail",
    )(attn2d, rest, rest, rest, rest, rest, rest, x2d, mod3, wa, wc, wo, cw, cvec, gvec)


def _cand_pairs():
    return [(i, j) for i in range(PEER_TOPK) for j in range(PEER_TOPK)
            if (i + 1) * (j + 1) <= PEER_TOPK]


def _top16_rows(s):
    n = s.shape[0]
    kio = lax.broadcasted_iota(jnp.int32, s.shape, 0).astype(F32)
    vals, idxs = [], []
    for _ in range(PEER_TOPK):
        m = jnp.max(s, axis=0, keepdims=True)
        idx = jnp.min(jnp.where(s == m, kio, float(n)), axis=0, keepdims=True)
        vals.append(m)
        idxs.append(idx)
        s = jnp.where(kio == idx, -jnp.inf, s)
    return vals, idxs


def _sort16_network():
    def merge(lo, hi, r):
        step = r * 2
        if step < hi - lo:
            yield from merge(lo, hi, step)
            yield from merge(lo + r, hi, step)
            yield from [(i, i + r) for i in range(lo + r, hi - r, step)]
        else:
            yield (lo, lo + r)

    def sort(lo, hi):
        if hi - lo >= 1:
            mid = lo + (hi - lo) // 2
            yield from sort(lo, mid)
            yield from sort(mid + 1, hi)
            yield from merge(lo, hi, 1)

    return list(sort(0, PEER_TOPK - 1))


def _top16_values(s):
    n_grp = s.shape[0] // SUBLANE
    assert n_grp == PEER_TOPK
    lst = [s[SUBLANE * v:SUBLANE * (v + 1), :] for v in range(n_grp)]
    for a, b in _sort16_network():
        lst[a], lst[b] = jnp.maximum(lst[a], lst[b]), jnp.minimum(lst[a], lst[b])
    for shift in (4, 2, 1):
        other = [pltpu.roll(x, shift, 0) for x in lst]
        lst = [jnp.maximum(lst[j], other[n_grp - 1 - j]) for j in range(n_grp)]
        dist = n_grp // 2
        while dist >= 1:
            for j in range(n_grp):
                if (j // dist) % 2 == 0:
                    lst[j], lst[j + dist] = (jnp.maximum(lst[j], lst[j + dist]),
                                             jnp.minimum(lst[j], lst[j + dist]))
            dist //= 2
    return [x[0:1, :] for x in lst]


def _route_kernel(h2t_ref, wqt_ref, sk_ref, f_ref, cnt_ref, g_ref, rb_ref):
    q2t = jnp.dot(wqt_ref[...], h2t_ref[...], preferred_element_type=F32)
    tl = q2t.shape[1]
    kio = lax.broadcasted_iota(jnp.int32, (PEER_NKEYS, tl), 0).astype(F32)
    pairs = _cand_pairs()
    n_pad = -len(pairs) % 8
    flat_ids = np.array([i * PEER_TOPK + j for i, j in pairs] + [PEER_TOPK ** 2] * n_pad, np.float32)
    group_start = [min(k for k, (i, _) in enumerate(pairs) if i == ii) for ii in range(PEER_TOPK)]
    group_len = [sum(1 for (i, _) in pairs if i == ii) for ii in range(PEER_TOPK)]
    n_rows = len(pairs) + n_pad
    flat_col = lax.broadcasted_iota(jnp.int32, (n_rows, tl), 0)
    flat = jnp.zeros((n_rows, tl), F32)
    for k in range(n_rows):
        flat = jnp.where(flat_col == k, float(flat_ids[k]), flat)

    def head_scores(h):
        out = []
        for c in range(2):
            hc = 2 * h + c
            qt = q2t[hc * LANE:(hc + 1) * LANE, :].astype(BF16)
            out.append(jnp.dot(sk_ref[hc], qt, preferred_element_type=F32))
        return out

    def write_tables(h, scores, v1, v2, is_rank1, is_rank2):
        cand0 = jnp.concatenate([v1[i] + v2[j] for i, j in pairs]
                                + [jnp.full((n_pad, tl), -jnp.inf, F32)], axis=0)
        cand = cand0
        for _ in range(PEER_TOPK):
            m = jnp.max(cand, axis=0, keepdims=True)
            fid = jnp.min(jnp.where(cand == m, flat, float(PEER_TOPK ** 2 + 1)), axis=0, keepdims=True)
            cand = jnp.where(flat == fid, -jnp.inf, cand)
        picked = jnp.where((cand == -jnp.inf) & (flat < float(PEER_TOPK ** 2)), 1.0, 0.0)
        top = v1[0] + v2[0]
        z = jnp.sum(picked * jnp.exp(cand0 - top), axis=0, keepdims=True)
        counts = [jnp.sum(picked[group_start[i]:group_start[i] + group_len[i]], axis=0, keepdims=True)
                  for i in range(PEER_TOPK)]
        cnt = jnp.zeros((PEER_NKEYS, tl), F32)
        rank_b = jnp.full((PEER_NKEYS, tl), float(PEER_NKEYS), F32)
        for i in range(PEER_TOPK):
            cnt = jnp.where(is_rank1(i), counts[i], cnt)
            rank_b = jnp.where(is_rank2(i), float(i), rank_b)
        outs = ((f_ref, jnp.exp(scores[0] - v1[0]) * (0.5 / z)),
                (g_ref, jnp.exp(scores[1] - v2[0])), (cnt_ref, cnt), (rb_ref, rank_b))
        for ref, val in outs:
            for ch in range(tl // LANE):
                ref[h, ch] = val[:, ch * LANE:(ch + 1) * LANE].astype(ref.dtype)

    tied = jnp.zeros((1, tl), F32)
    for h in range(PEER_HEADS):
        scores = head_scores(h)
        tops = [_top16_values(sc) for sc in scores]
        for sc, v in zip(scores, tops):
            n_ge = jnp.sum(jnp.where(sc >= v[PEER_TOPK - 1], 1.0, 0.0), axis=0, keepdims=True)
            tied = jnp.maximum(tied, jnp.where(n_ge == float(PEER_TOPK), 0.0, 1.0))
            for i in range(PEER_TOPK - 1):
                tied = jnp.maximum(tied, jnp.where(v[i] > v[i + 1], 0.0, 1.0))
        write_tables(h, scores, tops[0], tops[1],
                     lambda i, sc=scores[0], v=tops[0]: sc == v[i],
                     lambda i, sc=scores[1], v=tops[1]: sc == v[i])

    @pl.when(jnp.max(tied) > 0.0)
    def _():
        for h in range(PEER_HEADS):
            scores = head_scores(h)
            (v1, i1), (v2, i2) = [_top16_rows(sc) for sc in scores]
            write_tables(h, scores, v1, v2, lambda i, i1=i1: kio == i1[i], lambda i, i2=i2: kio == i2[i])


def _route(h2t, wqt, sk):
    d, t = h2t.shape
    spec = pl.BlockSpec((PEER_HEADS, ROUTE_TL // LANE, PEER_NKEYS, LANE), lambda i: (0, i, 0, 0))
    out = jax.ShapeDtypeStruct((PEER_HEADS, t // LANE, PEER_NKEYS, LANE), F32)
    out16 = jax.ShapeDtypeStruct((PEER_HEADS, t // LANE, PEER_NKEYS, LANE), BF16)
    return pl.pallas_call(
        _route_kernel,
        grid=(t // ROUTE_TL,),
        in_specs=[pl.BlockSpec((d, ROUTE_TL), lambda i: (0, i)),
                  pl.BlockSpec(wqt.shape, lambda i: (0, 0)),
                  pl.BlockSpec(sk.shape, lambda i: (0, 0, 0))],
        out_specs=[spec, spec, spec, spec],
        out_shape=[out, out, out16, out16],
        compiler_params=_cparams("parallel"),
        name="peer_route",
    )(h2t, wqt, sk)


def _expert_kernel(h2t_ref, u_ref, vt_ref, f_ref, cnt_ref, g_ref, rb_ref, x1_ref, mod_ref, gp_ref,
                   o_ref, acc_ref, act_ref, coef_ref):
    e = pl.program_id(1)
    n_chunks = EXP_TL // LANE
    a_per_step = EXP_TE // PEER_NKEYS
    a_group = 2

    @pl.when(e == 0)
    def _():
        acc_ref[...] = jnp.zeros_like(acc_ref)

    act = jnp.dot(u_ref[...], h2t_ref[...], preferred_element_type=F32)
    for c in range(n_chunks):
        act_ref[c] = act[:, c * LANE:(c + 1) * LANE]

    def lane_chunk(c, carry):
        for ag in range(a_per_step // a_group):
            coefs = [jnp.zeros((PEER_NKEYS, LANE), BF16) for _ in range(a_group)]

            def row_tile(ref, h, a):
                row = jnp.broadcast_to(ref[h, c, pl.ds(a, 1), :], (BF16_ROWS, LANE)).astype(BF16)
                return jnp.broadcast_to(row[None], (PEER_NKEYS // BF16_ROWS, BF16_ROWS, LANE)
                                        ).reshape(PEER_NKEYS, LANE)

            for h in range(PEER_HEADS):
                rank_b = rb_ref[h, c]
                g_b = g_ref[h, c]
                for k in range(a_group):
                    a = e * a_per_step + ag * a_group + k
                    picked = jnp.maximum(jnp.minimum(row_tile(cnt_ref, h, a) - rank_b, g_b), 0.0)
                    coefs[k] = coefs[k] + row_tile(f_ref, h, a) * picked
            for k in range(a_group):
                rows = pl.ds((ag * a_group + k) * PEER_NKEYS, PEER_NKEYS)
                coef_ref[c, rows, :] = _gelu_x2(act_ref[c, rows, :].astype(BF16)) * coefs[k]
        return carry

    lax.fori_loop(0, n_chunks, lane_chunk, 0)
    coef = jnp.concatenate([coef_ref[c] for c in range(n_chunks)], axis=1)
    acc_ref[...] += jnp.dot(vt_ref[0], coef, preferred_element_type=F32)

    @pl.when(e == pl.num_programs(1) - 1)
    def _():
        y = acc_ref[...].T
        o_ref[...] = x1_ref[...] + mod_ref[0] * _rms(y, gp_ref[...])


def _experts(h2t, u, vt, f, cnt, g, rb, x1, gt2, gpost, seq):
    d, t = h2t.shape
    n_exp = u.shape[0]
    tiles_per_seq = seq // EXP_TL
    n_chunks = EXP_TL // LANE
    route = pl.BlockSpec((PEER_HEADS, n_chunks, PEER_NKEYS, LANE), lambda i, e: (0, i, 0, 0))
    return pl.pallas_call(
        _expert_kernel,
        grid=(t // EXP_TL, n_exp // EXP_TE),
        in_specs=[pl.BlockSpec((d, EXP_TL), lambda i, e: (0, i)),
                  pl.BlockSpec((EXP_TE, d), lambda i, e: (e, 0)),
                  pl.BlockSpec((1, d, EXP_TE), lambda i, e: (e, 0, 0)),
                  route, route, route, route,
                  pl.BlockSpec((EXP_TL, d), lambda i, e: (i, 0)),
                  pl.BlockSpec((1, 1, d), lambda i, e: (i // tiles_per_seq, 0, 0)),
                  pl.BlockSpec((1, d), lambda i, e: (0, 0))],
        out_specs=pl.BlockSpec((EXP_TL, d), lambda i, e: (i, 0)),
        out_shape=jax.ShapeDtypeStruct((t, d), F32),
        scratch_shapes=[pltpu.VMEM((d, EXP_TL), F32), pltpu.VMEM((n_chunks, EXP_TE, LANE), F32),
                        pltpu.VMEM((n_chunks, EXP_TE, LANE), BF16)],
        compiler_params=_cparams("parallel", "arbitrary"),
        name="peer_experts",
    )(h2t, u, vt, f, cnt, g, rb, x1, gt2, gpost)


def _pad_heads(w, n, width):
    d = w.shape[0]
    return jnp.pad(w.reshape(d, n, width), ((0, 0), (0, 0), (0, LANE - width))).reshape(d, n * LANE)


def _split_w_in(w_in):
    q_cols = N_HEADS * D_QK
    k_cols = N_GROUPS * D_QK
    v_cols = N_GROUPS * D_V
    sizes = (q_cols, k_cols, k_cols, k_cols, v_cols, v_cols, v_cols, 3 * N_HEADS, 2 * D_MODEL, 2 * D_MODEL)
    offs = np.cumsum((0,) + sizes)
    parts = [w_in[:, offs[k]:offs[k + 1]] for k in range(len(sizes))]
    wq, wkc, wks, wkw, wvc, wvs, wvw, wgate, wglu, wmerge = parts
    w_qkv = jnp.concatenate(
        [_pad_heads(wq * (ATTN_SCALE * LOG2E), N_HEADS, D_QK)]
        + [_pad_heads(w, N_GROUPS, D_QK) for w in (wkc, wks, wkw)]
        + [_pad_heads(w, N_GROUPS, D_V) for w in (wvc, wvs, wvw)], axis=1)
    d = w_in.shape[0]
    wg = wgate.reshape(d, 3, N_GROUPS, HEADS_PER_GROUP).transpose(0, 2, 1, 3)
    wg = wg.reshape(d, N_GROUPS, 3 * HEADS_PER_GROUP)
    wg = jnp.pad(wg, ((0, 0), (0, 0), (0, LANE - 3 * HEADS_PER_GROUP))).reshape(d, N_GROUPS * LANE)
    w_rest = jnp.concatenate([wglu, wmerge, wg], axis=1)
    return w_qkv.astype(BF16), w_rest.astype(BF16)


def _cmp_weights(w1, w2, pos, dh):
    hidden = w1.shape[1]
    w1p = jnp.pad(w1.reshape(CMP_BLOCK, dh, hidden), ((0, 0), (0, LANE - dh), (0, 0)))
    w1p = w1p.reshape(2, CMP_STRIDE * LANE, hidden).astype(BF16)
    w2p = jnp.pad(w2, ((0, 0), (0, LANE - dh))).astype(BF16)
    posp = jnp.pad(pos, ((0, 0), (0, LANE - dh))).reshape(2, CMP_STRIDE * LANE)
    return w1p, w2p, posp


def kernel(x, c, w_ada, b_ada, g_pre_mix, g_post_mix, g_pre_ffn, g_post_ffn, rel_table, w_in,
           cmp_w1k, cmp_w2k, cmp_pos_k, cmp_w1v, cmp_w2v, cmp_pos_v, w_attn_out,
           conv_w, conv_b, conv_ln_g, conv_ln_b, w_conv_out, w_out,
           peer_wq, peer_subkeys, peer_u, peer_v):
    bsz, seq, d = x.shape
    t = bsz * seq
    assert d == D_MODEL and w_ada.shape[0] == 1, "single-layer block with D_MODEL channels"
    assert seq % PROJ_TM == 0 and seq // CMP_STRIDE == LANE and seq % EXP_TL == 0
    x2d = x.reshape(t, d)

    mod = _ada(c, w_ada[0], b_ada[0]).reshape(bsz, 6, d)
    mod_in = mod[:, 0:2]
    mod_tail = mod[:, 2:5]
    mod_out = mod[:, 5:6]

    bkt_c, bkt_t, bkt_w = _static_buckets(seq)
    bias_c = _bias_table(rel_table, bkt_c)
    bias_t = _bias_table(rel_table, bkt_t).reshape(N_HEADS, N_BIAS_TILES, TQ, TQ).transpose(1, 0, 2, 3)
    bias_w = _bias_table(rel_table, bkt_w)

    w_qkv, w_rest = _split_w_in(w_in[0])
    add_qkv = np.zeros((QKV_HEADS, LANE), np.float32)
    add_qkv[QKV_VS0:QKV_VS0 + N_GROUPS, D_V:] = 1.0
    add_qkv[QKV_VW0:QKV_VW0 + N_GROUPS, D_V:] = 1.0
    qkv = _project(x2d, mod_in, g_pre_mix, w_qkv, jnp.asarray(add_qkv.reshape(1, -1)), seq,
                   heads_out=True)
    rest = _project(x2d, mod_in, g_pre_mix, w_rest, jnp.zeros((1, REST_COLS), F32), seq,
                    heads_out=False)

    n_chunk = seq // CMP_STRIDE
    kch = qkv[:, QKV_KC0:QKV_KC0 + N_GROUPS].reshape(bsz, N_GROUPS, n_chunk, CMP_STRIDE * LANE)
    vch = qkv[:, QKV_VC0:QKV_VC0 + N_GROUPS].reshape(bsz, N_GROUPS, n_chunk, CMP_STRIDE * LANE)
    w1k, w2k, pk = _cmp_weights(cmp_w1k[0], cmp_w2k[0], cmp_pos_k[0], D_QK)
    w1v, w2v, pv = _cmp_weights(cmp_w1v[0], cmp_w2v[0], cmp_pos_v[0], D_V)
    kc, vc = _compress(kch, vch, pk, pv, w1k, w2k, w1v, w2v)

    pad = ((0, 0), (0, 0), (WINDOW, 0), (0, 0))
    kwp = jnp.pad(qkv[:, QKV_KW0:QKV_KW0 + N_GROUPS], pad)
    vwp = jnp.pad(qkv[:, QKV_VW0:QKV_VW0 + N_GROUPS], pad)

    n_idx = np.arange(LANE)[:, None] * CMP_STRIDE
    j_idx = np.arange(LANE)[None, :] * SLC_BLOCK
    overlap = ((n_idx <= j_idx + SLC_BLOCK - 1) & (n_idx + CMP_BLOCK - 1 >= j_idx)
               & (np.arange(LANE)[None, :] < seq // SLC_BLOCK)
               & (np.arange(LANE)[:, None] < n_chunk - 1)).astype(np.float32)
    blk_onehot = (np.arange(LANE)[None, :] == D_QK + np.arange(seq)[:, None] // SLC_BLOCK)
    ks = qkv[:, QKV_KS0:QKV_KS0 + N_GROUPS] + jnp.asarray(blk_onehot, dtype=BF16)
    attn = _attention(qkv, ks, kwp, vwp, kc, vc, rest, bias_c, bias_t, bias_w, jnp.asarray(overlap))

    cvec = jnp.stack([conv_b[0], conv_ln_g[0], conv_ln_b[0]])
    gvec = jnp.stack([g_post_mix[0], g_pre_ffn[0]])
    x1, h2t = _mixer_tail(attn.reshape(t, N_HEADS * D_V), rest, x2d, mod_tail,
                          w_attn_out[0].astype(BF16), w_conv_out[0].astype(BF16),
                          w_out[0].astype(BF16), conv_w[0, :, 0, :], cvec, gvec, seq)

    wqt = peer_wq[0].T.astype(BF16)
    sk = peer_subkeys[0].reshape(2 * PEER_HEADS, PEER_NKEYS, PEER_DQ // 2).astype(BF16)
    f, cnt, g, rb = _route(h2t, wqt, sk)
    n_exp = peer_v.shape[1]
    vt = peer_v[0].reshape(n_exp // EXP_TE, EXP_TE, d).transpose(0, 2, 1).astype(BF16)
    out = _experts(h2t, peer_u[0].astype(BF16), vt, f, cnt, g, rb, x1, mod_out, g_post_ffn, seq)
    return out.reshape(bsz, seq, d)
```

```python
import functools
import math

import numpy as np
import jax
import jax.numpy as jnp
from jax import lax
from jax.experimental import pallas as pl
from jax.experimental.pallas import tpu as pltpu

F32 = jnp.float32
BF16 = jnp.bfloat16

D_MODEL = 1024
N_HEADS = 16
N_GROUPS = 4
HEADS_PER_GROUP = N_HEADS // N_GROUPS
D_QK = 96
D_V = 64
CMP_BLOCK = 32
CMP_STRIDE = 16
CMP_HIDDEN = 256
SLC_BLOCK = 64
SLC_TOPN = 16
WINDOW = 512
ATTN_SCALE = D_QK ** -0.5
FORCE_BONUS = 1e6
N_BUCKETS = 32
MAX_DISTANCE = 128
CONV_WIDTH = 31
PEER_HEADS = 8
PEER_NKEYS = 128
PEER_DQ = 256
PEER_TOPK = 16
NORM_EPS = 1e-6
NEG_INF = -1e30
LOG2E = math.log2(math.e)

LANE = 128
SUBLANE = 8
BF16_ROWS = 16
VMEM_LIMIT = 56 * 1024 * 1024

TQ = 256
SLC_CHUNK = 512
WIN_TQ = 128
WIN_SPAN = WIN_TQ + WINDOW
N_BIAS_TILES = 4
PROJ_TM = 2048
PROJ_TN_HEADS = 1024
PROJ_TN_REST = 768
TAIL_TM = 512
HALO = 32
ROUTE_TL = 256
EXP_TL = 512
EXP_TE = 2048

QKV_Q0, QKV_KC0, QKV_KS0, QKV_KW0, QKV_VC0, QKV_VS0, QKV_VW0 = 0, 16, 20, 24, 28, 32, 36
QKV_HEADS = 40
REST_GLU_A, REST_GLU_G, REST_MRG_A, REST_MRG_B = 0, 1, 2, 3
REST_GATE0 = 4 * D_MODEL
REST_COLS = 4 * D_MODEL + N_GROUPS * LANE

NT_DIMS = (((1,), (1,)), ((), ()))


def _cparams(*sem):
    return pltpu.CompilerParams(dimension_semantics=sem, vmem_limit_bytes=VMEM_LIMIT)


def _gelu(x):
    return 0.5 * x * (1.0 + jnp.tanh(math.sqrt(2.0 / math.pi) * (x + 0.044715 * (x * x * x))))


def _gelu_x2(x):
    k1 = math.sqrt(2.0 / math.pi)
    return x * (1.0 + jnp.tanh(x * (k1 + (k1 * 0.044715) * (x * x))))


def _sigmoid(x):
    return 1.0 / (1.0 + jnp.exp(-x))


def _rms(x, g):
    return x * lax.rsqrt(jnp.mean(x * x, axis=-1, keepdims=True) + NORM_EPS) * g


def _ada_kernel(c_ref, w_ref, b_ref, o_ref):
    c = c_ref[...]
    c_act = c * _sigmoid(c)
    o_ref[...] = jnp.dot(c_act, w_ref[...], preferred_element_type=F32,
                         precision=lax.Precision.HIGHEST) + b_ref[...]


def _ada(c, w, b):
    bsz, d = c.shape
    n = w.shape[1]
    tn = 1024
    return pl.pallas_call(
        _ada_kernel,
        grid=(n // tn,),
        in_specs=[pl.BlockSpec((bsz, d), lambda j: (0, 0)),
                  pl.BlockSpec((d, tn), lambda j: (0, j)),
                  pl.BlockSpec((1, tn), lambda j: (0, j))],
        out_specs=pl.BlockSpec((bsz, tn), lambda j: (0, j)),
        out_shape=jax.ShapeDtypeStruct((bsz, n), F32),
        compiler_params=_cparams("parallel"),
        name="ada",
    )(c, w, b.reshape(1, n))


def _t5_bucket_np(dist):
    n = np.maximum(dist, 0)
    max_exact = N_BUCKETS // 2
    nf = np.maximum(n, 1).astype(np.float64)
    large = max_exact + (np.log(nf / max_exact) / math.log(MAX_DISTANCE / max_exact)
                         * (N_BUCKETS - max_exact)).astype(np.int64)
    return np.where(n < max_exact, n, np.minimum(large, N_BUCKETS - 1)).astype(np.int32)


def _bias_kernel(tab_ref, bkt_ref, o_ref):
    h = pl.program_id(0)
    bkt = bkt_ref[...]
    acc = jnp.where(bkt < 0, NEG_INF, 0.0).astype(F32)
    for i in range(N_BUCKETS):
        acc = jnp.where(bkt == i, tab_ref[i, h] * LOG2E, acc)
    o_ref[0] = acc


def _bias_table(rel_table, bucket):
    rows, cols = bucket.shape
    return pl.pallas_call(
        _bias_kernel,
        grid=(N_HEADS,),
        in_specs=[pl.BlockSpec(memory_space=pltpu.SMEM),
                  pl.BlockSpec((rows, cols), lambda h: (0, 0))],
        out_specs=pl.BlockSpec((1, rows, cols), lambda h: (h, 0, 0)),
        out_shape=jax.ShapeDtypeStruct((N_HEADS, rows, cols), F32),
        compiler_params=_cparams("parallel"),
        name="bias_table",
    )(rel_table, jnp.asarray(bucket))


def _static_buckets(seq):
    t = np.arange(seq)[:, None]
    n = np.arange(LANE)[None, :]
    dist_c = t - (n * CMP_STRIDE + CMP_BLOCK - 1)
    bkt_c = np.where(dist_c >= 0, _t5_bucket_np(dist_c), -1).astype(np.int32)
    qi = np.arange(TQ)[:, None]
    kj = np.arange(TQ)[None, :]
    tiles = [np.where(d * TQ + qi - kj >= 0, _t5_bucket_np(d * TQ + qi - kj), -1) for d in range(3)]
    tiles.append(np.full((TQ, TQ), -1))
    bkt_t = np.concatenate(tiles, axis=0).astype(np.int32)
    kw = np.arange(WIN_SPAN)[None, :]
    dist_w = WINDOW + np.arange(WIN_TQ)[:, None] - kw
    band = (dist_w >= 0) & (dist_w < WINDOW)
    bkt_w = np.where(band, _t5_bucket_np(dist_w), -1).astype(np.int32)
    return bkt_c, bkt_t, bkt_w


def _proj_kernel(x_ref, mod_ref, g_ref, w_ref, add_ref, o_ref, h_ref, *, heads_out, tn):
    @pl.when(pl.program_id(1) == 0)
    def _():
        x = x_ref[...]
        sh = mod_ref[0, 0:1, :]
        sc = mod_ref[0, 1:2, :]
        h_ref[...] = (_rms(x, g_ref[...]) * (1.0 + sc) + sh).astype(BF16)

    res = (jnp.dot(h_ref[...], w_ref[...], preferred_element_type=F32) + add_ref[...]).astype(o_ref.dtype)
    if heads_out:
        for k in range(tn // LANE):
            o_ref[0, k] = res[:, k * LANE:(k + 1) * LANE]
    else:
        o_ref[...] = res


def _project(x2d, mod3, g, w, add, seq, heads_out):
    t, d = x2d.shape
    n = w.shape[1]
    bsz = t // seq
    tiles_per_seq = seq // PROJ_TM
    tn = PROJ_TN_HEADS if heads_out else PROJ_TN_REST
    assert n % tn == 0
    grid = (t // PROJ_TM, n // tn)
    if heads_out:
        hpt = tn // LANE
        out_shape = jax.ShapeDtypeStruct((bsz, n // LANE, seq, LANE), BF16)
        out_spec = pl.BlockSpec((1, hpt, PROJ_TM, LANE),
                                lambda i, j: (i // tiles_per_seq, j, i % tiles_per_seq, 0))
    else:
        out_shape = jax.ShapeDtypeStruct((t, n), BF16)
        out_spec = pl.BlockSpec((PROJ_TM, tn), lambda i, j: (i, j))
    return pl.pallas_call(
        functools.partial(_proj_kernel, heads_out=heads_out, tn=tn),
        grid=grid,
        in_specs=[pl.BlockSpec((PROJ_TM, d), lambda i, j: (i, 0)),
                  pl.BlockSpec((1, 2, d), lambda i, j: (i // tiles_per_seq, 0, 0)),
                  pl.BlockSpec((1, d), lambda i, j: (0, 0)),
                  pl.BlockSpec((d, tn), lambda i, j: (0, j)),
                  pl.BlockSpec((1, tn), lambda i, j: (0, j))],
        out_specs=out_spec,
        out_shape=out_shape,
        scratch_shapes=[pltpu.VMEM((PROJ_TM, d), BF16)],
        compiler_params=_cparams("parallel", "arbitrary"),
        name="proj_heads" if heads_out else "proj_rest",
    )(x2d, mod3, g, w, add)


def _compress_kernel(kch_ref, vch_ref, pk_ref, pv_ref, w1k_ref, w2k_ref, w1v_ref, w2v_ref,
                     kc_ref, vc_ref):
    def one(ch_ref, pos_ref, w1_ref, w2_ref, o_ref):
        a = ch_ref[0, 0].astype(F32)
        a1 = (a + pos_ref[0:1, :]).astype(BF16)
        a2 = (a + pos_ref[1:2, :]).astype(BF16)
        p1 = jnp.dot(a1, w1_ref[0], preferred_element_type=F32)
        p2 = jnp.dot(a2, w1_ref[1], preferred_element_type=F32)
        n_chunk = p2.shape[0]
        hid = _gelu(p1 + pltpu.roll(p2, n_chunk - 1, 0))
        o_ref[0, 0] = jnp.dot(hid.astype(BF16), w2_ref[...],
                              preferred_element_type=F32).astype(BF16)

    one(kch_ref, pk_ref, w1k_ref, w2k_ref, kc_ref)
    one(vch_ref, pv_ref, w1v_ref, w2v_ref, vc_ref)


def _compress(kch, vch, pk, pv, w1k, w2k, w1v, w2v):
    bsz, g, n_chunk, width = kch.shape
    assert n_chunk == LANE, "compressed keys are laid out on one 128-row tile"
    spec_in = pl.BlockSpec((1, 1, n_chunk, width), lambda b, gg: (b, gg, 0, 0))
    spec_out = pl.BlockSpec((1, 1, n_chunk, LANE), lambda b, gg: (b, gg, 0, 0))
    full = lambda a: pl.BlockSpec(a.shape, lambda b, gg: (0,) * a.ndim)
    out = jax.ShapeDtypeStruct((bsz, g, n_chunk, LANE), BF16)
    return pl.pallas_call(
        _compress_kernel,
        grid=(bsz, g),
        in_specs=[spec_in, spec_in, full(pk), full(pv), full(w1k), full(w2k), full(w1v), full(w2v)],
        out_specs=[spec_out, spec_out],
        out_shape=[out, out],
        compiler_params=_cparams("parallel", "parallel"),
        name="compress",
    )(kch, vch, pk, pv, w1k, w2k, w1v, w2v)


def _attn_kernel(q_ref, kc_ref, vc_ref, ks_ref, vs_ref, kw_ref, vw_ref, gate_ref,
                 bias_c_ref, bias_t_ref, bias_w_ref, overlap_ref,
                 o_ref, s_ref, p_ref, a_ref, acc_ref):
    qi = pl.program_id(2)
    rows = HEADS_PER_GROUP * TQ
    seq = ks_ref.shape[2]
    q4 = q_ref[0].reshape(rows, LANE)

    lc = lax.dot_general(q4, kc_ref[0, 0], NT_DIMS, preferred_element_type=F32)
    lc = lc + bias_c_ref[...].reshape(rows, LANE)
    valid = lc > 0.5 * NEG_INF
    mc = jnp.max(lc, axis=-1, keepdims=True)
    ec = jnp.where(valid, jnp.exp2(lc - mc), 0.0)
    sc = jnp.sum(ec, axis=-1, keepdims=True)
    pc = ec / jnp.where(sc > 0.0, sc, 1.0)
    o_cmp = jnp.dot(pc.astype(BF16), vc_ref[0, 0], preferred_element_type=F32)

    psum = jnp.sum(pc.reshape(HEADS_PER_GROUP, TQ, LANE), axis=0)
    imp = jnp.dot(psum, overlap_ref[...], preferred_element_type=F32,
                  precision=lax.Precision.HIGHEST)
    t_pos = qi * TQ + lax.broadcasted_iota(jnp.int32, (TQ, LANE), 0)
    j_blk = lax.broadcasted_iota(jnp.int32, (TQ, LANE), 1)
    cur = t_pos // SLC_BLOCK
    forced = (j_blk == 0) | (j_blk == cur) | (j_blk == cur - 1)
    score = jnp.where(j_blk <= cur, imp + jnp.where(forced, FORCE_BONUS, 0.0), NEG_INF)
    n_slc = seq // SLC_BLOCK
    sc_t = score.T[:n_slc]
    j_row = lax.broadcasted_iota(jnp.int32, (n_slc, TQ), 0)
    rank = jnp.zeros((n_slc, TQ), F32)
    for jp in range(n_slc):
        row = sc_t[jp:jp + 1, :]
        ge = jnp.where(row >= sc_t, 1.0, 0.0)
        gt = jnp.where(row > sc_t, 1.0, 0.0)
        rank = rank + jnp.where(j_row > jp, ge, gt)
    sel_t = jnp.where(rank < float(min(SLC_TOPN, n_slc)), 1.0, 0.0)
    assert D_QK + n_slc == LANE
    sel_add = jnp.concatenate([jnp.zeros((D_QK, TQ), F32), (sel_t - 1.0) * (-NEG_INF)], axis=0)
    q_sel = (q_ref[0] + sel_add.T.astype(BF16)[None]).reshape(rows, LANE)

    tiles_per_chunk = SLC_CHUNK // TQ

    def slc_step(c, carry):
        k0 = pl.multiple_of(c * SLC_CHUNK, SLC_CHUNK)
        kt = ks_ref[0, 0, pl.ds(k0, SLC_CHUNK), :]
        vt = vs_ref[0, 0, pl.ds(k0, SLC_CHUNK), :]
        s_ref[...] = lax.dot_general(q_sel, kt, NT_DIMS, preferred_element_type=F32)
        behind = [qi - (c * tiles_per_chunk + k) for k in range(tiles_per_chunk)]
        tile_ids = [jnp.where(d < 0, N_BIAS_TILES - 1, jnp.minimum(d, N_BIAS_TILES - 2)) for d in behind]
        out = []
        for r in range(HEADS_PER_GROUP):
            m = carry[r]
            rsl = slice(r * TQ, (r + 1) * TQ)
            bias = jnp.concatenate([bias_t_ref[tile_ids[k], r] for k in range(tiles_per_chunk)],
                                   axis=-1)
            s = s_ref[rsl, :] + bias
            m_new = jnp.maximum(m, jnp.max(s, axis=-1, keepdims=True))
            p_ref[rsl, :] = jnp.exp2(s - m_new).astype(BF16)
            a_ref[rsl, :] = jnp.broadcast_to(jnp.exp2(m - m_new), (TQ, LANE))
            out.append(m_new)
        acc_ref[...] = a_ref[...] * acc_ref[...] + jnp.dot(p_ref[...], vt, preferred_element_type=F32)
        return tuple(out)

    n_chunks = (qi + tiles_per_chunk) // tiles_per_chunk
    acc_ref[...] = jnp.zeros_like(acc_ref)
    init = tuple(jnp.full((TQ, 1), -jnp.inf, F32) for _ in range(HEADS_PER_GROUP))
    lax.fori_loop(0, n_chunks, slc_step, init)
    acc_s = acc_ref[...]
    o_slc = acc_s / pltpu.roll(acc_s, LANE - D_V, 1)

    kj = lax.broadcasted_iota(jnp.int32, (1, WIN_SPAN), 1)
    slabs = []
    for part in range(TQ // WIN_TQ):
        t0 = qi * TQ + part * WIN_TQ
        w0 = pl.multiple_of(t0, WIN_TQ)
        kwin = kw_ref[0, 0, pl.ds(w0, WIN_SPAN), :]
        vwin = vw_ref[0, 0, pl.ds(w0, WIN_SPAN), :]
        q_part = q_ref[0, :, part * WIN_TQ:(part + 1) * WIN_TQ, :].reshape(HEADS_PER_GROUP * WIN_TQ, LANE)
        sw = lax.dot_general(q_part, kwin, NT_DIMS, preferred_element_type=F32)
        sw = sw + bias_w_ref[...].reshape(HEADS_PER_GROUP * WIN_TQ, WIN_SPAN)
        sw = sw + jnp.where(kj >= WINDOW - t0, 0.0, NEG_INF)
        mw = jnp.max(sw, axis=-1, keepdims=True)
        pw = jnp.exp2(sw - mw)
        ow = jnp.dot(pw.astype(BF16), vwin, preferred_element_type=F32)
        ow = ow / pltpu.roll(ow, LANE - D_V, 1)
        slabs.append(ow.reshape(HEADS_PER_GROUP, WIN_TQ, LANE))
    o_win = jnp.concatenate(slabs, axis=1).reshape(rows, LANE)

    gates = _sigmoid(gate_ref[...].astype(F32))
    outs = []
    for r in range(HEADS_PER_GROUP):
        sl = slice(r * TQ, (r + 1) * TQ)
        g_c = gates[:, r:r + 1]
        g_s = gates[:, HEADS_PER_GROUP + r:HEADS_PER_GROUP + r + 1]
        g_w = gates[:, 2 * HEADS_PER_GROUP + r:2 * HEADS_PER_GROUP + r + 1]
        o = g_c * o_cmp[sl] + g_s * o_slc[sl] + g_w * o_win[sl]
        outs.append(o[:, :D_V])
    o_ref[0] = jnp.concatenate(outs, axis=-1).astype(o_ref.dtype)


def _attention(qkv, ks, kwp, vwp, kc, vc, rest, bias_c, bias_t, bias_w, overlap):
    bsz, _, seq, _ = qkv.shape
    nq = seq // TQ
    g_cols0 = REST_GATE0 // LANE
    head = lambda base: pl.BlockSpec((1, 1, seq, LANE), lambda b, g, i: (b, base + g, 0, 0))
    in_specs = [
        pl.BlockSpec((1, HEADS_PER_GROUP, TQ, LANE), lambda b, g, i: (b, g, i, 0)),
        pl.BlockSpec((1, 1, LANE, LANE), lambda b, g, i: (b, g, 0, 0)),
        pl.BlockSpec((1, 1, LANE, LANE), lambda b, g, i: (b, g, 0, 0)),
        pl.BlockSpec((1, 1, seq, LANE), lambda b, g, i: (b, g, 0, 0)),
        head(QKV_VS0),
        pl.BlockSpec((1, 1, seq + WINDOW, LANE), lambda b, g, i: (b, g, 0, 0)),
        pl.BlockSpec((1, 1, seq + WINDOW, LANE), lambda b, g, i: (b, g, 0, 0)),
        pl.BlockSpec((TQ, LANE), lambda b, g, i: (b * nq + i, g_cols0 + g)),
        pl.BlockSpec((HEADS_PER_GROUP, TQ, LANE), lambda b, g, i: (g, i, 0)),
        pl.BlockSpec((N_BIAS_TILES, HEADS_PER_GROUP, TQ, TQ), lambda b, g, i: (0, g, 0, 0)),
        pl.BlockSpec((HEADS_PER_GROUP, WIN_TQ, WIN_SPAN), lambda b, g, i: (g, 0, 0)),
        pl.BlockSpec((LANE, LANE), lambda b, g, i: (0, 0)),
    ]
    return pl.pallas_call(
        _attn_kernel,
        grid=(bsz, N_GROUPS, nq),
        in_specs=in_specs,
        out_specs=pl.BlockSpec((1, TQ, HEADS_PER_GROUP * D_V), lambda b, g, i: (b, i, g)),
        out_shape=jax.ShapeDtypeStruct((bsz, seq, N_HEADS * D_V), BF16),
        scratch_shapes=[pltpu.VMEM((HEADS_PER_GROUP * TQ, SLC_CHUNK), F32),
                        pltpu.VMEM((HEADS_PER_GROUP * TQ, SLC_CHUNK), BF16),
                        pltpu.VMEM((HEADS_PER_GROUP * TQ, LANE), F32),
                        pltpu.VMEM((HEADS_PER_GROUP * TQ, LANE), F32)],
        compiler_params=_cparams("parallel", "parallel", "arbitrary"),
        name="attention",
    )(qkv, kc, vc, ks, qkv, kwp, vwp, rest, bias_c, bias_t, bias_w, overlap)


def _tail_kernel(attn_ref, ga_ref, gg_ref, ha_ref, hg_ref, ma_ref, mb_ref, x_ref, mod_ref,
                 wa_ref, wc_ref, wo_ref, cw_ref, cvec_ref, gvec_ref,
                 x1_ref, h2t_ref, u_ref, ush_ref, *, tiles_per_seq):
    first = (pl.program_id(0) % tiles_per_seq) == 0
    y_a = jnp.dot(attn_ref[...], wa_ref[...], preferred_element_type=F32)

    halo = ha_ref[...].astype(F32) * _sigmoid(hg_ref[...].astype(F32))
    u_ref[0:HALO, :] = jnp.where(first, 0.0, halo)
    u_ref[HALO:, :] = ga_ref[...].astype(F32) * _sigmoid(gg_ref[...].astype(F32))

    conv_b, ln_g, ln_b = cvec_ref[0:1, :], cvec_ref[1:2, :], cvec_ref[2:3, :]
    n_sh = HALO + TAIL_TM - SUBLANE
    for r in range(1, SUBLANE):
        ush_ref[r - 1, 0:n_sh, :] = u_ref[r:r + n_sh, :]
    blk = 32
    pieces = []
    for rb in range(TAIL_TM // blk):
        acc = jnp.zeros((blk, D_MODEL), F32)
        for j in range(CONV_WIDTH):
            start = HALO - (CONV_WIDTH - 1) + rb * blk + j
            r, a = start % SUBLANE, start - start % SUBLANE
            win = u_ref[a:a + blk, :] if r == 0 else ush_ref[r - 1, a:a + blk, :]
            acc = acc + cw_ref[j:j + 1, :] * win
        pieces.append(acc)
    y = jnp.concatenate(pieces, axis=0) + conv_b
    yc = y - jnp.mean(y, axis=-1, keepdims=True)
    yn = yc * lax.rsqrt(jnp.mean(yc * yc, axis=-1, keepdims=True) + NORM_EPS) * ln_g + ln_b
    act = yn * _sigmoid(yn)
    y_b = jnp.dot(act.astype(BF16), wc_ref[...], preferred_element_type=F32)

    merged = (_sigmoid(ma_ref[...].astype(F32)) * y_a + _sigmoid(mb_ref[...].astype(F32)) * y_b)
    out = jnp.dot(merged.astype(BF16), wo_ref[...], preferred_element_type=F32)

    gt1, sh2, sc2 = mod_ref[0, 0:1, :], mod_ref[0, 1:2, :], mod_ref[0, 2:3, :]
    x1 = x_ref[...] + gt1 * _rms(out, gvec_ref[0:1, :])
    x1_ref[...] = x1
    h2 = _rms(x1, gvec_ref[1:2, :]) * (1.0 + sc2) + sh2
    h2t_ref[...] = h2.T.astype(BF16)


def _mixer_tail(attn2d, rest, x2d, mod3, wa, wc, wo, cw, cvec, gvec, seq):
    t, d = x2d.shape
    tiles_per_seq = seq // TAIL_TM
    hpt = TAIL_TM // HALO
    row = lambda cb: pl.BlockSpec((TAIL_TM, d), lambda i: (i, cb))
    halo = lambda cb: pl.BlockSpec((HALO, d), lambda i: (jnp.maximum(i * hpt - 1, 0), cb))
    full = lambda a: pl.BlockSpec(a.shape, lambda i: (0,) * a.ndim)
    return pl.pallas_call(
        functools.partial(_tail_kernel, tiles_per_seq=tiles_per_seq),
        grid=(t // TAIL_TM,),
        in_specs=[row(0), row(REST_GLU_A), row(REST_GLU_G), halo(REST_GLU_A), halo(REST_GLU_G),
                  row(REST_MRG_A), row(REST_MRG_B), row(0),
                  pl.BlockSpec((1, 3, d), lambda i: (i // tiles_per_seq, 0, 0)),
                  full(wa), full(wc), full(wo), full(cw), full(cvec), full(gvec)],
        out_specs=[row(0), pl.BlockSpec((d, TAIL_TM), lambda i: (0, i))],
        out_shape=[jax.ShapeDtypeStruct((t, d), F32), jax.ShapeDtypeStruct((d, t), BF16)],
        scratch_shapes=[pltpu.VMEM((HALO + TAIL_TM, d), F32),
                        pltpu.VMEM((SUBLANE - 1, HALO + TAIL_TM, d), F32)],
        compiler_params=_cparams("parallel"),
        name="mixer_tail",
    )(attn2d, rest, rest, rest, rest, rest, rest, x2d, mod3, wa, wc, wo, cw, cvec, gvec)


def _cand_pairs():
    return [(i, j) for i in range(PEER_TOPK) for j in range(PEER_TOPK)
            if (i + 1) * (j + 1) <= PEER_TOPK]


def _top16_rows(s):
    n = s.shape[0]
    kio = lax.broadcasted_iota(jnp.int32, s.shape, 0).astype(F32)
    vals, idxs = [], []
    for _ in range(PEER_TOPK):
        m = jnp.max(s, axis=0, keepdims=True)
        idx = jnp.min(jnp.where(s == m, kio, float(n)), axis=0, keepdims=True)
        vals.append(m)
        idxs.append(idx)
        s = jnp.where(kio == idx, -jnp.inf, s)
    return vals, idxs


def _sort16_network():
    def merge(lo, hi, r):
        step = r * 2
        if step < hi - lo:
            yield from merge(lo, hi, step)
            yield from merge(lo + r, hi, step)
            yield from [(i, i + r) for i in range(lo + r, hi - r, step)]
        else:
            yield (lo, lo + r)

    def sort(lo, hi):
        if hi - lo >= 1:
            mid = lo + (hi - lo) // 2
            yield from sort(lo, mid)
            yield from sort(mid + 1, hi)
            yield from merge(lo, hi, 1)

    return list(sort(0, PEER_TOPK - 1))


def _top16_values(s):
    n_grp = s.shape[0] // SUBLANE
    assert n_grp == PEER_TOPK
    lst = [s[SUBLANE * v:SUBLANE * (v + 1), :] for v in range(n_grp)]
    for a, b in _sort16_network():
        lst[a], lst[b] = jnp.maximum(lst[a], lst[b]), jnp.minimum(lst[a], lst[b])
    for shift in (4, 2, 1):
        other = [pltpu.roll(x, shift, 0) for x in lst]
        lst = [jnp.maximum(lst[j], other[n_grp - 1 - j]) for j in range(n_grp)]
        dist = n_grp // 2
        while dist >= 1:
            for j in range(n_grp):
                if (j // dist) % 2 == 0:
                    lst[j], lst[j + dist] = (jnp.maximum(lst[j], lst[j + dist]),
                                             jnp.minimum(lst[j], lst[j + dist]))
            dist //= 2
    return [x[0:1, :] for x in lst]


def _route_kernel(h2t_ref, wqt_ref, sk_ref, f_ref, cnt_ref, g_ref, rb_ref):
    q2t = jnp.dot(wqt_ref[...], h2t_ref[...], preferred_element_type=F32)
    tl = q2t.shape[1]
    kio = lax.broadcasted_iota(jnp.int32, (PEER_NKEYS, tl), 0).astype(F32)
    pairs = _cand_pairs()
    n_pad = -len(pairs) % 8
    flat_ids = np.array([i * PEER_TOPK + j for i, j in pairs] + [PEER_TOPK ** 2] * n_pad, np.float32)
    group_start = [min(k for k, (i, _) in enumerate(pairs) if i == ii) for ii in range(PEER_TOPK)]
    group_len = [sum(1 for (i, _) in pairs if i == ii) for ii in range(PEER_TOPK)]
    n_rows = len(pairs) + n_pad
    flat_col = lax.broadcasted_iota(jnp.int32, (n_rows, tl), 0)
    flat = jnp.zeros((n_rows, tl), F32)
    for k in range(n_rows):
        flat = jnp.where(flat_col == k, float(flat_ids[k]), flat)

    def head_scores(h):
        out = []
        for c in range(2):
            hc = 2 * h + c
            qt = q2t[hc * LANE:(hc + 1) * LANE, :].astype(BF16)
            out.append(jnp.dot(sk_ref[hc], qt, preferred_element_type=F32))
        return out

    def write_tables(h, scores, v1, v2, is_rank1, is_rank2):
        cand0 = jnp.concatenate([v1[i] + v2[j] for i, j in pairs]
                                + [jnp.full((n_pad, tl), -jnp.inf, F32)], axis=0)
        cand = cand0
        for _ in range(PEER_TOPK):
            m = jnp.max(cand, axis=0, keepdims=True)
            fid = jnp.min(jnp.where(cand == m, flat, float(PEER_TOPK ** 2 + 1)), axis=0, keepdims=True)
            cand = jnp.where(flat == fid, -jnp.inf, cand)
        picked = jnp.where((cand == -jnp.inf) & (flat < float(PEER_TOPK ** 2)), 1.0, 0.0)
        top = v1[0] + v2[0]
        z = jnp.sum(picked * jnp.exp(cand0 - top), axis=0, keepdims=True)
        counts = [jnp.sum(picked[group_start[i]:group_start[i] + group_len[i]], axis=0, keepdims=True)
                  for i in range(PEER_TOPK)]
        cnt = jnp.zeros((PEER_NKEYS, tl), F32)
        rank_b = jnp.full((PEER_NKEYS, tl), float(PEER_NKEYS), F32)
        for i in range(PEER_TOPK):
            cnt = jnp.where(is_rank1(i), counts[i], cnt)
            rank_b = jnp.where(is_rank2(i), float(i), rank_b)
        outs = ((f_ref, jnp.exp(scores[0] - v1[0]) * (0.5 / z)),
                (g_ref, jnp.exp(scores[1] - v2[0])), (cnt_ref, cnt), (rb_ref, rank_b))
        for ref, val in outs:
            for ch in range(tl // LANE):
                ref[h, ch] = val[:, ch * LANE:(ch + 1) * LANE].astype(ref.dtype)

    tied = jnp.zeros((1, tl), F32)
    for h in range(PEER_HEADS):
        scores = head_scores(h)
        tops = [_top16_values(sc) for sc in scores]
        for sc, v in zip(scores, tops):
            n_ge = jnp.sum(jnp.where(sc >= v[PEER_TOPK - 1], 1.0, 0.0), axis=0, keepdims=True)
            tied = jnp.maximum(tied, jnp.where(n_ge == float(PEER_TOPK), 0.0, 1.0))
            for i in range(PEER_TOPK - 1):
                tied = jnp.maximum(tied, jnp.where(v[i] > v[i + 1], 0.0, 1.0))
        write_tables(h, scores, tops[0], tops[1],
                     lambda i, sc=scores[0], v=tops[0]: sc == v[i],
                     lambda i, sc=scores[1], v=tops[1]: sc == v[i])

    @pl.when(jnp.max(tied) > 0.0)
    def _():
        for h in range(PEER_HEADS):
            scores = head_scores(h)
            (v1, i1), (v2, i2) = [_top16_rows(sc) for sc in scores]
            write_tables(h, scores, v1, v2, lambda i, i1=i1: kio == i1[i], lambda i, i2=i2: kio == i2[i])


def _route(h2t, wqt, sk):
    d, t = h2t.shape
    spec = pl.BlockSpec((PEER_HEADS, ROUTE_TL // LANE, PEER_NKEYS, LANE), lambda i: (0, i, 0, 0))
    out = jax.ShapeDtypeStruct((PEER_HEADS, t // LANE, PEER_NKEYS, LANE), F32)
    out16 = jax.ShapeDtypeStruct((PEER_HEADS, t // LANE, PEER_NKEYS, LANE), BF16)
    return pl.pallas_call(
        _route_kernel,
        grid=(t // ROUTE_TL,),
        in_specs=[pl.BlockSpec((d, ROUTE_TL), lambda i: (0, i)),
                  pl.BlockSpec(wqt.shape, lambda i: (0, 0)),
                  pl.BlockSpec(sk.shape, lambda i: (0, 0, 0))],
        out_specs=[spec, spec, spec, spec],
        out_shape=[out, out, out16, out16],
        compiler_params=_cparams("parallel"),
        name="peer_route",
    )(h2t, wqt, sk)


def _expert_kernel(h2t_ref, u_ref, vt_ref, f_ref, cnt_ref, g_ref, rb_ref, x1_ref, mod_ref, gp_ref,
                   o_ref, acc_ref, act_ref, coef_ref):
    e = pl.program_id(1)
    n_chunks = EXP_TL // LANE
    a_per_step = EXP_TE // PEER_NKEYS
    a_group = 2

    @pl.when(e == 0)
    def _():
        acc_ref[...] = jnp.zeros_like(acc_ref)

    act = jnp.dot(u_ref[...], h2t_ref[...], preferred_element_type=F32)
    for c in range(n_chunks):
        act_ref[c] = act[:, c * LANE:(c + 1) * LANE]

    def lane_chunk(c, carry):
        for ag in range(a_per_step // a_group):
            coefs = [jnp.zeros((PEER_NKEYS, LANE), BF16) for _ in range(a_group)]

            def row_tile(ref, h, a):
                row = jnp.broadcast_to(ref[h, c, pl.ds(a, 1), :], (BF16_ROWS, LANE)).astype(BF16)
                return jnp.broadcast_to(row[None], (PEER_NKEYS // BF16_ROWS, BF16_ROWS, LANE)
                                        ).reshape(PEER_NKEYS, LANE)

            for h in range(PEER_HEADS):
                rank_b = rb_ref[h, c]
                g_b = g_ref[h, c]
                for k in range(a_group):
                    a = e * a_per_step + ag * a_group + k
                    picked = jnp.maximum(jnp.minimum(row_tile(cnt_ref, h, a) - rank_b, g_b), 0.0)
                    coefs[k] = coefs[k] + row_tile(f_ref, h, a) * picked
            for k in range(a_group):
                rows = pl.ds((ag * a_group + k) * PEER_NKEYS, PEER_NKEYS)
                coef_ref[c, rows, :] = _gelu_x2(act_ref[c, rows, :].astype(BF16)) * coefs[k]
        return carry

    lax.fori_loop(0, n_chunks, lane_chunk, 0)
    coef = jnp.concatenate([coef_ref[c] for c in range(n_chunks)], axis=1)
    acc_ref[...] += jnp.dot(vt_ref[0], coef, preferred_element_type=F32)

    @pl.when(e == pl.num_programs(1) - 1)
    def _():
        y = acc_ref[...].T
        o_ref[...] = x1_ref[...] + mod_ref[0] * _rms(y, gp_ref[...])


def _experts(h2t, u, vt, f, cnt, g, rb, x1, gt2, gpost, seq):
    d, t = h2t.shape
    n_exp = u.shape[0]
    tiles_per_seq = seq // EXP_TL
    n_chunks = EXP_TL // LANE
    route = pl.BlockSpec((PEER_HEADS, n_chunks, PEER_NKEYS, LANE), lambda i, e: (0, i, 0, 0))
    return pl.pallas_call(
        _expert_kernel,
        grid=(t // EXP_TL, n_exp // EXP_TE),
        in_specs=[pl.BlockSpec((d, EXP_TL), lambda i, e: (0, i)),
                  pl.BlockSpec((EXP_TE, d), lambda i, e: (e, 0)),
                  pl.BlockSpec((1, d, EXP_TE), lambda i, e: (e, 0, 0)),
                  route, route, route, route,
                  pl.BlockSpec((EXP_TL, d), lambda i, e: (i, 0)),
                  pl.BlockSpec((1, 1, d), lambda i, e: (i // tiles_per_seq, 0, 0)),
                  pl.BlockSpec((1, d), lambda i, e: (0, 0))],
        out_specs=pl.BlockSpec((EXP_TL, d), lambda i, e: (i, 0)),
        out_shape=jax.ShapeDtypeStruct((t, d), F32),
        scratch_shapes=[pltpu.VMEM((d, EXP_TL), F32), pltpu.VMEM((n_chunks, EXP_TE, LANE), F32),
                        pltpu.VMEM((n_chunks, EXP_TE, LANE), BF16)],
        compiler_params=_cparams("parallel", "arbitrary"),
        name="peer_experts",
    )(h2t, u, vt, f, cnt, g, rb, x1, gt2, gpost)


def _pad_heads(w, n, width):
    d = w.shape[0]
    return jnp.pad(w.reshape(d, n, width), ((0, 0), (0, 0), (0, LANE - width))).reshape(d, n * LANE)


def _split_w_in(w_in):
    q_cols = N_HEADS * D_QK
    k_cols = N_GROUPS * D_QK
    v_cols = N_GROUPS * D_V
    sizes = (q_cols, k_cols, k_cols, k_cols, v_cols, v_cols, v_cols, 3 * N_HEADS, 2 * D_MODEL, 2 * D_MODEL)
    offs = np.cumsum((0,) + sizes)
    parts = [w_in[:, offs[k]:offs[k + 1]] for k in range(len(sizes))]
    wq, wkc, wks, wkw, wvc, wvs, wvw, wgate, wglu, wmerge = parts
    w_qkv = jnp.concatenate(
        [_pad_heads(wq * (ATTN_SCALE * LOG2E), N_HEADS, D_QK)]
        + [_pad_heads(w, N_GROUPS, D_QK) for w in (wkc, wks, wkw)]
        + [_pad_heads(w, N_GROUPS, D_V) for w in (wvc, wvs, wvw)], axis=1)
    d = w_in.shape[0]
    wg = wgate.reshape(d, 3, N_GROUPS, HEADS_PER_GROUP).transpose(0, 2, 1, 3)
    wg = wg.reshape(d, N_GROUPS, 3 * HEADS_PER_GROUP)
    wg = jnp.pad(wg, ((0, 0), (0, 0), (0, LANE - 3 * HEADS_PER_GROUP))).reshape(d, N_GROUPS * LANE)
    w_rest = jnp.concatenate([wglu, wmerge, wg], axis=1)
    return w_qkv.astype(BF16), w_rest.astype(BF16)


def _cmp_weights(w1, w2, pos, dh):
    hidden = w1.shape[1]
    w1p = jnp.pad(w1.reshape(CMP_BLOCK, dh, hidden), ((0, 0), (0, LANE - dh), (0, 0)))
    w1p = w1p.reshape(2, CMP_STRIDE * LANE, hidden).astype(BF16)
    w2p = jnp.pad(w2, ((0, 0), (0, LANE - dh))).astype(BF16)
    posp = jnp.pad(pos, ((0, 0), (0, LANE - dh))).reshape(2, CMP_STRIDE * LANE)
    return w1p, w2p, posp


def kernel(x, c, w_ada, b_ada, g_pre_mix, g_post_mix, g_pre_ffn, g_post_ffn, rel_table, w_in,
           cmp_w1k, cmp_w2k, cmp_pos_k, cmp_w1v, cmp_w2v, cmp_pos_v, w_attn_out,
           conv_w, conv_b, conv_ln_g, conv_ln_b, w_conv_out, w_out,
           peer_wq, peer_subkeys, peer_u, peer_v):
    bsz, seq, d = x.shape
    t = bsz * seq
    assert d == D_MODEL and w_ada.shape[0] == 1, "single-layer block with D_MODEL channels"
    assert seq % PROJ_TM == 0 and seq // CMP_STRIDE == LANE and seq % EXP_TL == 0
    x2d = x.reshape(t, d)

    mod = _ada(c, w_ada[0], b_ada[0]).reshape(bsz, 6, d)
    mod_in = mod[:, 0:2]
    mod_tail = mod[:, 2:5]
    mod_out = mod[:, 5:6]

    bkt_c, bkt_t, bkt_w = _static_buckets(seq)
    bias_c = _bias_table(rel_table, bkt_c)
    bias_t = _bias_table(rel_table, bkt_t).reshape(N_HEADS, N_BIAS_TILES, TQ, TQ).transpose(1, 0, 2, 3)
    bias_w = _bias_table(rel_table, bkt_w)

    w_qkv, w_rest = _split_w_in(w_in[0])
    add_qkv = np.zeros((QKV_HEADS, LANE), np.float32)
    add_qkv[QKV_VS0:QKV_VS0 + N_GROUPS, D_V:] = 1.0
    add_qkv[QKV_VW0:QKV_VW0 + N_GROUPS, D_V:] = 1.0
    qkv = _project(x2d, mod_in, g_pre_mix, w_qkv, jnp.asarray(add_qkv.reshape(1, -1)), seq,
                   heads_out=True)
    rest = _project(x2d, mod_in, g_pre_mix, w_rest, jnp.zeros((1, REST_COLS), F32), seq,
                    heads_out=False)

    n_chunk = seq // CMP_STRIDE
    kch = qkv[:, QKV_KC0:QKV_KC0 + N_GROUPS].reshape(bsz, N_GROUPS, n_chunk, CMP_STRIDE * LANE)
    vch = qkv[:, QKV_VC0:QKV_VC0 + N_GROUPS].reshape(bsz, N_GROUPS, n_chunk, CMP_STRIDE * LANE)
    w1k, w2k, pk = _cmp_weights(cmp_w1k[0], cmp_w2k[0], cmp_pos_k[0], D_QK)
    w1v, w2v, pv = _cmp_weights(cmp_w1v[0], cmp_w2v[0], cmp_pos_v[0], D_V)
    kc, vc = _compress(kch, vch, pk, pv, w1k, w2k, w1v, w2v)

    pad = ((0, 0), (0, 0), (WINDOW, 0), (0, 0))
    kwp = jnp.pad(qkv[:, QKV_KW0:QKV_KW0 + N_GROUPS], pad)
    vwp = jnp.pad(qkv[:, QKV_VW0:QKV_VW0 + N_GROUPS], pad)

    n_idx = np.arange(LANE)[:, None] * CMP_STRIDE
    j_idx = np.arange(LANE)[None, :] * SLC_BLOCK
    overlap = ((n_idx <= j_idx + SLC_BLOCK - 1) & (n_idx + CMP_BLOCK - 1 >= j_idx)
               & (np.arange(LANE)[None, :] < seq // SLC_BLOCK)
               & (np.arange(LANE)[:, None] < n_chunk - 1)).astype(np.float32)
    blk_onehot = (np.arange(LANE)[None, :] == D_QK + np.arange(seq)[:, None] // SLC_BLOCK)
    ks = qkv[:, QKV_KS0:QKV_KS0 + N_GROUPS] + jnp.asarray(blk_onehot, dtype=BF16)
    attn = _attention(qkv, ks, kwp, vwp, kc, vc, rest, bias_c, bias_t, bias_w, jnp.asarray(overlap))

    cvec = jnp.stack([conv_b[0], conv_ln_g[0], conv_ln_b[0]])
    gvec = jnp.stack([g_post_mix[0], g_pre_ffn[0]])
    x1, h2t = _mixer_tail(attn.reshape(t, N_HEADS * D_V), rest, x2d, mod_tail,
                          w_attn_out[0].astype(BF16), w_conv_out[0].astype(BF16),
                          w_out[0].astype(BF16), conv_w[0, :, 0, :], cvec, gvec, seq)

    wqt = peer_wq[0].T.astype(BF16)
    sk = peer_subkeys[0].reshape(2 * PEER_HEADS, PEER_NKEYS, PEER_DQ // 2).astype(BF16)
    f, cnt, g, rb = _route(h2t, wqt, sk)
    n_exp = peer_v.shape[1]
    vt = peer_v[0].reshape(n_exp // EXP_TE, EXP_TE, d).transpose(0, 2, 1).astype(BF16)
    out = _experts(h2t, peer_u[0].astype(BF16), vt, f, cnt, g, rb, x1, mod_out, g_post_ffn, seq)
    return out.reshape(bsz, seq, d)
```

```python
import functools
import math

import numpy as np
import jax
import jax.numpy as jnp
from jax import lax
from jax.experimental import pallas as pl
from jax.experimental.pallas import tpu as pltpu

F32 = jnp.float32
BF16 = jnp.bfloat16

D_MODEL = 1024
N_HEADS = 16
N_GROUPS = 4
HEADS_PER_GROUP = N_HEADS // N_GROUPS
D_QK = 96
D_V = 64
CMP_BLOCK = 32
CMP_STRIDE = 16
CMP_HIDDEN = 256
SLC_BLOCK = 64
SLC_TOPN = 16
WINDOW = 512
ATTN_SCALE = D_QK ** -0.5
FORCE_BONUS = 1e6
N_BUCKETS = 32
MAX_DISTANCE = 128
CONV_WIDTH = 31
PEER_HEADS = 8
PEER_NKEYS = 128
PEER_DQ = 256
PEER_TOPK = 16
NORM_EPS = 1e-6
NEG_INF = -1e30
LOG2E = math.log2(math.e)

LANE = 128
SUBLANE = 8
BF16_ROWS = 16
VMEM_LIMIT = 56 * 1024 * 1024

TQ = 256
SLC_CHUNK = 512
WIN_TQ = 128
WIN_SPAN = WIN_TQ + WINDOW
N_BIAS_TILES = 4
PROJ_TM = 2048
PROJ_TN_HEADS = 1024
PROJ_TN_REST = 768
TAIL_TM = 512
HALO = 32
ROUTE_TL = 256
EXP_TL = 512
EXP_TE = 2048

QKV_Q0, QKV_KC0, QKV_KS0, QKV_KW0, QKV_VC0, QKV_VS0, QKV_VW0 = 0, 16, 20, 24, 28, 32, 36
QKV_HEADS = 40
REST_GLU_A, REST_GLU_G, REST_MRG_A, REST_MRG_B = 0, 1, 2, 3
REST_GATE0 = 4 * D_MODEL
REST_COLS = 4 * D_MODEL + N_GROUPS * LANE

NT_DIMS = (((1,), (1,)), ((), ()))


def _cparams(*sem):
    return pltpu.CompilerParams(dimension_semantics=sem, vmem_limit_bytes=VMEM_LIMIT)


def _gelu(x):
    return 0.5 * x * (1.0 + jnp.tanh(math.sqrt(2.0 / math.pi) * (x + 0.044715 * (x * x * x))))


def _gelu_x2(x):
    k1 = math.sqrt(2.0 / math.pi)
    return x * (1.0 + jnp.tanh(x * (k1 + (k1 * 0.044715) * (x * x))))


def _sigmoid(x):
    return 1.0 / (1.0 + jnp.exp(-x))


def _rms(x, g):
    return x * lax.rsqrt(jnp.mean(x * x, axis=-1, keepdims=True) + NORM_EPS) * g


def _ada_kernel(c_ref, w_ref, b_ref, o_ref):
    c = c_ref[...]
    c_act = c * _sigmoid(c)
    o_ref[...] = jnp.dot(c_act, w_ref[...], preferred_element_type=F32,
                         precision=lax.Precision.HIGHEST) + b_ref[...]


def _ada(c, w, b):
    bsz, d = c.shape
    n = w.shape[1]
    tn = 1024
    return pl.pallas_call(
        _ada_kernel,
        grid=(n // tn,),
        in_specs=[pl.BlockSpec((bsz, d), lambda j: (0, 0)),
                  pl.BlockSpec((d, tn), lambda j: (0, j)),
                  pl.BlockSpec((1, tn), lambda j: (0, j))],
        out_specs=pl.BlockSpec((bsz, tn), lambda j: (0, j)),
        out_shape=jax.ShapeDtypeStruct((bsz, n), F32),
        compiler_params=_cparams("parallel"),
        name="ada",
    )(c, w, b.reshape(1, n))


def _t5_bucket_np(dist):
    n = np.maximum(dist, 0)
    max_exact = N_BUCKETS // 2
    nf = np.maximum(n, 1).astype(np.float64)
    large = max_exact + (np.log(nf / max_exact) / math.log(MAX_DISTANCE / max_exact)
                         * (N_BUCKETS - max_exact)).astype(np.int64)
    return np.where(n < max_exact, n, np.minimum(large, N_BUCKETS - 1)).astype(np.int32)


def _bias_kernel(tab_ref, bkt_ref, o_ref):
    h = pl.program_id(0)
    bkt = bkt_ref[...]
    acc = jnp.where(bkt < 0, NEG_INF, 0.0).astype(F32)
    for i in range(N_BUCKETS):
        acc = jnp.where(bkt == i, tab_ref[i, h] * LOG2E, acc)
    o_ref[0] = acc


def _bias_table(rel_table, bucket):
    rows, cols = bucket.shape
    return pl.pallas_call(
        _bias_kernel,
        grid=(N_HEADS,),
        in_specs=[pl.BlockSpec(memory_space=pltpu.SMEM),
                  pl.BlockSpec((rows, cols), lambda h: (0, 0))],
        out_specs=pl.BlockSpec((1, rows, cols), lambda h: (h, 0, 0)),
        out_shape=jax.ShapeDtypeStruct((N_HEADS, rows, cols), F32),
        compiler_params=_cparams("parallel"),
        name="bias_table",
    )(rel_table, jnp.asarray(bucket))


def _static_buckets(seq):
    t = np.arange(seq)[:, None]
    n = np.arange(LANE)[None, :]
    dist_c = t - (n * CMP_STRIDE + CMP_BLOCK - 1)
    bkt_c = np.where(dist_c >= 0, _t5_bucket_np(dist_c), -1).astype(np.int32)
    qi = np.arange(TQ)[:, None]
    kj = np.arange(TQ)[None, :]
    tiles = [np.where(d * TQ + qi - kj >= 0, _t5_bucket_np(d * TQ + qi - kj), -1) for d in range(3)]
    tiles.append(np.full((TQ, TQ), -1))
    bkt_t = np.concatenate(tiles, axis=0).astype(np.int32)
    kw = np.arange(WIN_SPAN)[None, :]
    dist_w = WINDOW + np.arange(WIN_TQ)[:, None] - kw
    band = (dist_w >= 0) & (dist_w < WINDOW)
    bkt_w = np.where(band, _t5_bucket_np(dist_w), -1).astype(np.int32)
    return bkt_c, bkt_t, bkt_w


def _proj_kernel(x_ref, mod_ref, g_ref, wh_ref, add_ref, wr_ref, oh_ref, or_ref, h_ref, *, n_head_tiles):
    j = pl.program_id(1)

    @pl.when(j == 0)
    def _():
        x = x_ref[...]
        sh = mod_ref[0, 0:1, :]
        sc = mod_ref[0, 1:2, :]
        h_ref[...] = (_rms(x, g_ref[...]) * (1.0 + sc) + sh).astype(BF16)

    @pl.when(j < n_head_tiles)
    def _():
        res = (jnp.dot(h_ref[...], wh_ref[...], preferred_element_type=F32) + add_ref[...]).astype(BF16)
        for k in range(PROJ_TN_HEADS // LANE):
            oh_ref[0, k] = res[:, k * LANE:(k + 1) * LANE]

    @pl.when(j >= n_head_tiles)
    def _():
        or_ref[...] = jnp.dot(h_ref[...], wr_ref[...], preferred_element_type=F32).astype(BF16)


def _project(x2d, mod3, g, w_heads, add_heads, w_rest, seq):
    t, d = x2d.shape
    n_h, n_r = w_heads.shape[1], w_rest.shape[1]
    bsz = t // seq
    tiles_per_seq = seq // PROJ_TM
    assert n_h % PROJ_TN_HEADS == 0 and n_r % PROJ_TN_REST == 0
    nh, nr = n_h // PROJ_TN_HEADS, n_r // PROJ_TN_REST
    hpt = PROJ_TN_HEADS // LANE
    head_col = lambda j: jnp.minimum(j, nh - 1)
    rest_col = lambda j: jnp.maximum(j - nh, 0)
    return pl.pallas_call(
        functools.partial(_proj_kernel, n_head_tiles=nh),
        grid=(t // PROJ_TM, nh + nr),
        in_specs=[pl.BlockSpec((PROJ_TM, d), lambda i, j: (i, 0)),
                  pl.BlockSpec((1, 2, d), lambda i, j: (i // tiles_per_seq, 0, 0)),
                  pl.BlockSpec((1, d), lambda i, j: (0, 0)),
                  pl.BlockSpec((d, PROJ_TN_HEADS), lambda i, j: (0, head_col(j))),
                  pl.BlockSpec((1, PROJ_TN_HEADS), lambda i, j: (0, head_col(j))),
                  pl.BlockSpec((d, PROJ_TN_REST), lambda i, j: (0, rest_col(j)))],
        out_specs=[pl.BlockSpec((1, hpt, PROJ_TM, LANE),
                                lambda i, j: (i // tiles_per_seq, head_col(j), i % tiles_per_seq, 0)),
                   pl.BlockSpec((PROJ_TM, PROJ_TN_REST), lambda i, j: (i, rest_col(j)))],
        out_shape=[jax.ShapeDtypeStruct((bsz, n_h // LANE, seq, LANE), BF16),
                   jax.ShapeDtypeStruct((t, n_r), BF16)],
        scratch_shapes=[pltpu.VMEM((PROJ_TM, d), BF16)],
        compiler_params=_cparams("parallel", "arbitrary"),
        name="proj",
    )(x2d, mod3, g, w_heads, add_heads, w_rest)


def _compress_kernel(kch_ref, vch_ref, pk_ref, pv_ref, w1k_ref, w2k_ref, w1v_ref, w2v_ref,
                     kc_ref, vc_ref):
    def one(ch_ref, pos_ref, w1_ref, w2_ref, o_ref):
        a = ch_ref[0, 0].astype(F32)
        a1 = (a + pos_ref[0:1, :]).astype(BF16)
        a2 = (a + pos_ref[1:2, :]).astype(BF16)
        p1 = jnp.dot(a1, w1_ref[0], preferred_element_type=F32)
        p2 = jnp.dot(a2, w1_ref[1], preferred_element_type=F32)
        n_chunk = p2.shape[0]
        hid = _gelu(p1 + pltpu.roll(p2, n_chunk - 1, 0))
        o_ref[0, 0] = jnp.dot(hid.astype(BF16), w2_ref[...],
                              preferred_element_type=F32).astype(BF16)

    one(kch_ref, pk_ref, w1k_ref, w2k_ref, kc_ref)
    one(vch_ref, pv_ref, w1v_ref, w2v_ref, vc_ref)


def _compress(kch, vch, pk, pv, w1k, w2k, w1v, w2v):
    bsz, g, n_chunk, width = kch.shape
    assert n_chunk == LANE, "compressed keys are laid out on one 128-row tile"
    spec_in = pl.BlockSpec((1, 1, n_chunk, width), lambda b, gg: (b, gg, 0, 0))
    spec_out = pl.BlockSpec((1, 1, n_chunk, LANE), lambda b, gg: (b, gg, 0, 0))
    full = lambda a: pl.BlockSpec(a.shape, lambda b, gg: (0,) * a.ndim)
    out = jax.ShapeDtypeStruct((bsz, g, n_chunk, LANE), BF16)
    return pl.pallas_call(
        _compress_kernel,
        grid=(bsz, g),
        in_specs=[spec_in, spec_in, full(pk), full(pv), full(w1k), full(w2k), full(w1v), full(w2v)],
        out_specs=[spec_out, spec_out],
        out_shape=[out, out],
        compiler_params=_cparams("parallel", "parallel"),
        name="compress",
    )(kch, vch, pk, pv, w1k, w2k, w1v, w2v)


def _attn_kernel(q_ref, kc_ref, vc_ref, ks_ref, vs_ref, kw_ref, vw_ref, gate_ref,
                 bias_c_ref, bias_t_ref, bias_w_ref, overlap_ref,
                 o_ref, s_ref, p_ref, a_ref, acc_ref):
    qi = pl.program_id(2)
    rows = HEADS_PER_GROUP * TQ
    seq = ks_ref.shape[2]
    q4 = q_ref[0].reshape(rows, LANE)

    lc = lax.dot_general(q4, kc_ref[0, 0], NT_DIMS, preferred_element_type=F32)
    lc = lc + bias_c_ref[...].reshape(rows, LANE)
    valid = lc > 0.5 * NEG_INF
    mc = jnp.max(lc, axis=-1, keepdims=True)
    ec = jnp.where(valid, jnp.exp2(lc - mc), 0.0)
    sc = jnp.sum(ec, axis=-1, keepdims=True)
    pc = ec / jnp.where(sc > 0.0, sc, 1.0)
    o_cmp = jnp.dot(pc.astype(BF16), vc_ref[0, 0], preferred_element_type=F32)

    psum = jnp.sum(pc.reshape(HEADS_PER_GROUP, TQ, LANE), axis=0)
    imp = jnp.dot(psum, overlap_ref[...], preferred_element_type=F32,
                  precision=lax.Precision.HIGHEST)
    t_pos = qi * TQ + lax.broadcasted_iota(jnp.int32, (TQ, LANE), 0)
    j_blk = lax.broadcasted_iota(jnp.int32, (TQ, LANE), 1)
    cur = t_pos // SLC_BLOCK
    forced = (j_blk == 0) | (j_blk == cur) | (j_blk == cur - 1)
    score = jnp.where(j_blk <= cur, imp + jnp.where(forced, FORCE_BONUS, 0.0), NEG_INF)
    n_slc = seq // SLC_BLOCK
    sc_t = score.T[:n_slc]
    j_row = lax.broadcasted_iota(jnp.int32, (n_slc, TQ), 0)
    rank = jnp.zeros((n_slc, TQ), F32)
    for jp in range(n_slc):
        row = sc_t[jp:jp + 1, :]
        ge = jnp.where(row >= sc_t, 1.0, 0.0)
        gt = jnp.where(row > sc_t, 1.0, 0.0)
        rank = rank + jnp.where(j_row > jp, ge, gt)
    sel_t = jnp.where(rank < float(min(SLC_TOPN, n_slc)), 1.0, 0.0)
    assert D_QK + n_slc == LANE
    sel_add = jnp.concatenate([jnp.zeros((D_QK, TQ), F32), (sel_t - 1.0) * (-NEG_INF)], axis=0)
    q_sel = (q_ref[0] + sel_add.T.astype(BF16)[None]).reshape(rows, LANE)

    tiles_per_chunk = SLC_CHUNK // TQ

    def slc_step(c, carry):
        k0 = pl.multiple_of(c * SLC_CHUNK, SLC_CHUNK)
        kt = ks_ref[0, 0, pl.ds(k0, SLC_CHUNK), :]
        vt = vs_ref[0, 0, pl.ds(k0, SLC_CHUNK), :]
        s_ref[...] = lax.dot_general(q_sel, kt, NT_DIMS, preferred_element_type=F32)
        behind = [qi - (c * tiles_per_chunk + k) for k in range(tiles_per_chunk)]
        tile_ids = [jnp.where(d < 0, N_BIAS_TILES - 1, jnp.minimum(d, N_BIAS_TILES - 2)) for d in behind]
        out = []
        for r in range(HEADS_PER_GROUP):
            m = carry[r]
            rsl = slice(r * TQ, (r + 1) * TQ)
            bias = jnp.concatenate([bias_t_ref[tile_ids[k], r] for k in range(tiles_per_chunk)],
                                   axis=-1)
            s = s_ref[rsl, :] + bias
            m_new = jnp.maximum(m, jnp.max(s, axis=-1, keepdims=True))
            p_ref[rsl, :] = jnp.exp2(s - m_new).astype(BF16)
            a_ref[rsl, :] = jnp.broadcast_to(jnp.exp2(m - m_new), (TQ, LANE))
            out.append(m_new)
        acc_ref[...] = a_ref[...] * acc_ref[...] + jnp.dot(p_ref[...], vt, preferred_element_type=F32)
        return tuple(out)

    n_chunks = (qi + tiles_per_chunk) // tiles_per_chunk
    acc_ref[...] = jnp.zeros_like(acc_ref)
    init = tuple(jnp.full((TQ, 1), -jnp.inf, F32) for _ in range(HEADS_PER_GROUP))
    lax.fori_loop(0, n_chunks, slc_step, init)
    acc_s = acc_ref[...]
    o_slc = acc_s / pltpu.roll(acc_s, LANE - D_V, 1)

    kj = lax.broadcasted_iota(jnp.int32, (1, WIN_SPAN), 1)
    slabs = []
    for part in range(TQ // WIN_TQ):
        t0 = qi * TQ + part * WIN_TQ
        w0 = pl.multiple_of(t0, WIN_TQ)
        kwin = kw_ref[0, 0, pl.ds(w0, WIN_SPAN), :]
        vwin = vw_ref[0, 0, pl.ds(w0, WIN_SPAN), :]
        q_part = q_ref[0, :, part * WIN_TQ:(part + 1) * WIN_TQ, :].reshape(HEADS_PER_GROUP * WIN_TQ, LANE)
        sw = lax.dot_general(q_part, kwin, NT_DIMS, preferred_element_type=F32)
        sw = sw + bias_w_ref[...].reshape(HEADS_PER_GROUP * WIN_TQ, WIN_SPAN)
        sw = sw + jnp.where(kj >= WINDOW - t0, 0.0, NEG_INF)
        mw = jnp.max(sw, axis=-1, keepdims=True)
        pw = jnp.exp2(sw - mw)
        ow = jnp.dot(pw.astype(BF16), vwin, preferred_element_type=F32)
        ow = ow / pltpu.roll(ow, LANE - D_V, 1)
        slabs.append(ow.reshape(HEADS_PER_GROUP, WIN_TQ, LANE))
    o_win = jnp.concatenate(slabs, axis=1).reshape(rows, LANE)

    gates = _sigmoid(gate_ref[...].astype(F32))
    outs = []
    for r in range(HEADS_PER_GROUP):
        sl = slice(r * TQ, (r + 1) * TQ)
        g_c = gates[:, r:r + 1]
        g_s = gates[:, HEADS_PER_GROUP + r:HEADS_PER_GROUP + r + 1]
        g_w = gates[:, 2 * HEADS_PER_GROUP + r:2 * HEADS_PER_GROUP + r + 1]
        o = g_c * o_cmp[sl] + g_s * o_slc[sl] + g_w * o_win[sl]
        outs.append(o[:, :D_V])
    o_ref[0] = jnp.concatenate(outs, axis=-1).astype(o_ref.dtype)


def _attention(qkv, ks, kwp, vwp, kc, vc, rest, bias_c, bias_t, bias_w, overlap):
    bsz, _, seq, _ = qkv.shape
    nq = seq // TQ
    g_cols0 = REST_GATE0 // LANE
    head = lambda base: pl.BlockSpec((1, 1, seq, LANE), lambda b, g, i: (b, base + g, 0, 0))
    in_specs = [
        pl.BlockSpec((1, HEADS_PER_GROUP, TQ, LANE), lambda b, g, i: (b, g, i, 0)),
        pl.BlockSpec((1, 1, LANE, LANE), lambda b, g, i: (b, g, 0, 0)),
        pl.BlockSpec((1, 1, LANE, LANE), lambda b, g, i: (b, g, 0, 0)),
        pl.BlockSpec((1, 1, seq, LANE), lambda b, g, i: (b, g, 0, 0)),
        head(QKV_VS0),
        pl.BlockSpec((1, 1, seq + WINDOW, LANE), lambda b, g, i: (b, g, 0, 0)),
        pl.BlockSpec((1, 1, seq + WINDOW, LANE), lambda b, g, i: (b, g, 0, 0)),
        pl.BlockSpec((TQ, LANE), lambda b, g, i: (b * nq + i, g_cols0 + g)),
        pl.BlockSpec((HEADS_PER_GROUP, TQ, LANE), lambda b, g, i: (g, i, 0)),
        pl.BlockSpec((N_BIAS_TILES, HEADS_PER_GROUP, TQ, TQ), lambda b, g, i: (0, g, 0, 0)),
        pl.BlockSpec((HEADS_PER_GROUP, WIN_TQ, WIN_SPAN), lambda b, g, i: (g, 0, 0)),
        pl.BlockSpec((LANE, LANE), lambda b, g, i: (0, 0)),
    ]
    return pl.pallas_call(
        _attn_kernel,
        grid=(bsz, N_GROUPS, nq),
        in_specs=in_specs,
        out_specs=pl.BlockSpec((1, TQ, HEADS_PER_GROUP * D_V), lambda b, g, i: (b, i, g)),
        out_shape=jax.ShapeDtypeStruct((bsz, seq, N_HEADS * D_V), BF16),
        scratch_shapes=[pltpu.VMEM((HEADS_PER_GROUP * TQ, SLC_CHUNK), F32),
                        pltpu.VMEM((HEADS_PER_GROUP * TQ, SLC_CHUNK), BF16),
                        pltpu.VMEM((HEADS_PER_GROUP * TQ, LANE), F32),
                        pltpu.VMEM((HEADS_PER_GROUP * TQ, LANE), F32)],
        compiler_params=_cparams("parallel", "parallel", "arbitrary"),
        name="attention",
    )(qkv, kc, vc, ks, qkv, kwp, vwp, rest, bias_c, bias_t, bias_w, overlap)


def _tail_kernel(attn_ref, ga_ref, gg_ref, ha_ref, hg_ref, ma_ref, mb_ref, x_ref, mod_ref,
                 wa_ref, wc_ref, wo_ref, cw_ref, cvec_ref, gvec_ref,
                 x1_ref, h2t_ref, u_ref, ush_ref, *, tiles_per_seq):
    first = (pl.program_id(0) % tiles_per_seq) == 0
    y_a = jnp.dot(attn_ref[...], wa_ref[...], preferred_element_type=F32)

    halo = ha_ref[...].astype(F32) * _sigmoid(hg_ref[...].astype(F32))
    u_ref[0:HALO, :] = jnp.where(first, 0.0, halo)
    u_ref[HALO:, :] = ga_ref[...].astype(F32) * _sigmoid(gg_ref[...].astype(F32))

    conv_b, ln_g, ln_b = cvec_ref[0:1, :], cvec_ref[1:2, :], cvec_ref[2:3, :]
    n_sh = HALO + TAIL_TM - SUBLANE
    for r in range(1, SUBLANE):
        ush_ref[r - 1, 0:n_sh, :] = u_ref[r:r + n_sh, :]
    blk = 32
    pieces = []
    for rb in range(TAIL_TM // blk):
        acc = jnp.zeros((blk, D_MODEL), F32)
        for j in range(CONV_WIDTH):
            start = HALO - (CONV_WIDTH - 1) + rb * blk + j
            r, a = start % SUBLANE, start - start % SUBLANE
            win = u_ref[a:a + blk, :] if r == 0 else ush_ref[r - 1, a:a + blk, :]
            acc = acc + cw_ref[j:j + 1, :] * win
        pieces.append(acc)
    y = jnp.concatenate(pieces, axis=0) + conv_b
    yc = y - jnp.mean(y, axis=-1, keepdims=True)
    yn = yc * lax.rsqrt(jnp.mean(yc * yc, axis=-1, keepdims=True) + NORM_EPS) * ln_g + ln_b
    act = yn * _sigmoid(yn)
    y_b = jnp.dot(act.astype(BF16), wc_ref[...], preferred_element_type=F32)

    merged = (_sigmoid(ma_ref[...].astype(F32)) * y_a + _sigmoid(mb_ref[...].astype(F32)) * y_b)
    out = jnp.dot(merged.astype(BF16), wo_ref[...], preferred_element_type=F32)

    gt1, sh2, sc2 = mod_ref[0, 0:1, :], mod_ref[0, 1:2, :], mod_ref[0, 2:3, :]
    x1 = x_ref[...] + gt1 * _rms(out, gvec_ref[0:1, :])
    x1_ref[...] = x1
    h2 = _rms(x1, gvec_ref[1:2, :]) * (1.0 + sc2) + sh2
    h2t_ref[...] = h2.T.astype(BF16)


def _mixer_tail(attn2d, rest, x2d, mod3, wa, wc, wo, cw, cvec, gvec, seq):
    t, d = x2d.shape
    tiles_per_seq = seq // TAIL_TM
    hpt = TAIL_TM // HALO
    row = lambda cb: pl.BlockSpec((TAIL_TM, d), lambda i: (i, cb))
    halo = lambda cb: pl.BlockSpec((HALO, d), lambda i: (jnp.maximum(i * hpt - 1, 0), cb))
    full = lambda a: pl.BlockSpec(a.shape, lambda i: (0,) * a.ndim)
    return pl.pallas_call(
        functools.partial(_tail_kernel, tiles_per_seq=tiles_per_seq),
        grid=(t // TAIL_TM,),
        in_specs=[row(0), row(REST_GLU_A), row(REST_GLU_G), halo(REST_GLU_A), halo(REST_GLU_G),
                  row(REST_MRG_A), row(REST_MRG_B), row(0),
                  pl.BlockSpec((1, 3, d), lambda i: (i // tiles_per_seq, 0, 0)),
                  full(wa), full(wc), full(wo), full(cw), full(cvec), full(gvec)],
        out_specs=[row(0), pl.BlockSpec((d, TAIL_TM), lambda i: (0, i))],
        out_shape=[jax.ShapeDtypeStruct((t, d), F32), jax.ShapeDtypeStruct((d, t), BF16)],
        scratch_shapes=[pltpu.VMEM((HALO + TAIL_TM, d), F32),
                        pltpu.VMEM((SUBLANE - 1, HALO + TAIL_TM, d), F32)],
        compiler_params=_cparams("parallel"),
        name="mixer_tail",
    )(attn2d, rest, rest, rest, rest, rest, rest, x2d, mod3, wa, wc, wo, cw, cvec, gvec)


def _cand_pairs():
    return [(i, j) for i in range(PEER_TOPK) for j in range(PEER_TOPK)
            if (i + 1) * (j + 1) <= PEER_TOPK]


def _top16_rows(s):
    n = s.shape[0]
    kio = lax.broadcasted_iota(jnp.int32, s.shape, 0).astype(F32)
    vals, idxs = [], []
    for _ in range(PEER_TOPK):
        m = jnp.max(s, axis=0, keepdims=True)
        idx = jnp.min(jnp.where(s == m, kio, float(n)), axis=0, keepdims=True)
        vals.append(m)
        idxs.append(idx)
        s = jnp.where(kio == idx, -jnp.inf, s)
    return vals, idxs


def _sort16_network():
    def merge(lo, hi, r):
        step = r * 2
        if step < hi - lo:
            yield from merge(lo, hi, step)
            yield from merge(lo + r, hi, step)
            yield from [(i, i + r) for i in range(lo + r, hi - r, step)]
        else:
            yield (lo, lo + r)

    def sort(lo, hi):
        if hi - lo >= 1:
            mid = lo + (hi - lo) // 2
            yield from sort(lo, mid)
            yield from sort(mid + 1, hi)
            yield from merge(lo, hi, 1)

    return list(sort(0, PEER_TOPK - 1))


def _top16_values(s):
    n_grp = s.shape[0] // SUBLANE
    assert n_grp == PEER_TOPK
    lst = [s[SUBLANE * v:SUBLANE * (v + 1), :] for v in range(n_grp)]
    for a, b in _sort16_network():
        lst[a], lst[b] = jnp.maximum(lst[a], lst[b]), jnp.minimum(lst[a], lst[b])
    for shift in (4, 2, 1):
        other = [pltpu.roll(x, shift, 0) for x in lst]
        lst = [jnp.maximum(lst[j], other[n_grp - 1 - j]) for j in range(n_grp)]
        dist = n_grp // 2
        while dist >= 1:
            for j in range(n_grp):
                if (j // dist) % 2 == 0:
                    lst[j], lst[j + dist] = (jnp.maximum(lst[j], lst[j + dist]),
                                             jnp.minimum(lst[j], lst[j + dist]))
            dist //= 2
    return [x[0:1, :] for x in lst]


def _route_kernel(h2t_ref, wqt_ref, sk_ref, f_ref, cnt_ref, g_ref, rb_ref):
    q2t = jnp.dot(wqt_ref[...], h2t_ref[...], preferred_element_type=F32)
    tl = q2t.shape[1]
    kio = lax.broadcasted_iota(jnp.int32, (PEER_NKEYS, tl), 0).astype(F32)
    pairs = _cand_pairs()
    n_pad = -len(pairs) % 8
    flat_ids = np.array([i * PEER_TOPK + j for i, j in pairs] + [PEER_TOPK ** 2] * n_pad, np.float32)
    group_start = [min(k for k, (i, _) in enumerate(pairs) if i == ii) for ii in range(PEER_TOPK)]
    group_len = [sum(1 for (i, _) in pairs if i == ii) for ii in range(PEER_TOPK)]
    n_rows = len(pairs) + n_pad
    flat_col = lax.broadcasted_iota(jnp.int32, (n_rows, tl), 0)
    flat = jnp.zeros((n_rows, tl), F32)
    for k in range(n_rows):
        flat = jnp.where(flat_col == k, float(flat_ids[k]), flat)

    def head_scores(h):
        out = []
        for c in range(2):
            hc = 2 * h + c
            qt = q2t[hc * LANE:(hc + 1) * LANE, :].astype(BF16)
            out.append(jnp.dot(sk_ref[hc], qt, preferred_element_type=F32))
        return out

    def write_tables(h, scores, v1, v2, is_rank1, is_rank2):
        cand0 = jnp.concatenate([v1[i] + v2[j] for i, j in pairs]
                                + [jnp.full((n_pad, tl), -jnp.inf, F32)], axis=0)
        cand = cand0
        for _ in range(PEER_TOPK):
            m = jnp.max(cand, axis=0, keepdims=True)
            fid = jnp.min(jnp.where(cand == m, flat, float(PEER_TOPK ** 2 + 1)), axis=0, keepdims=True)
            cand = jnp.where(flat == fid, -jnp.inf, cand)
        picked = jnp.where((cand == -jnp.inf) & (flat < float(PEER_TOPK ** 2)), 1.0, 0.0)
        top = v1[0] + v2[0]
        z = jnp.sum(picked * jnp.exp(cand0 - top), axis=0, keepdims=True)
        counts = [jnp.sum(picked[group_start[i]:group_start[i] + group_len[i]], axis=0, keepdims=True)
                  for i in range(PEER_TOPK)]
        cnt = jnp.zeros((PEER_NKEYS, tl), F32)
        rank_b = jnp.full((PEER_NKEYS, tl), float(PEER_NKEYS), F32)
        for i in range(PEER_TOPK):
            cnt = jnp.where(is_rank1(i), counts[i], cnt)
            rank_b = jnp.where(is_rank2(i), float(i), rank_b)
        outs = ((f_ref, jnp.exp(scores[0] - v1[0]) * (0.5 / z)),
                (g_ref, jnp.exp(scores[1] - v2[0])), (cnt_ref, cnt), (rb_ref, rank_b))
        for ref, val in outs:
            for ch in range(tl // LANE):
                ref[h, ch] = val[:, ch * LANE:(ch + 1) * LANE].astype(ref.dtype)

    tied = jnp.zeros((1, tl), F32)
    for h in range(PEER_HEADS):
        scores = head_scores(h)
        tops = [_top16_values(sc) for sc in scores]
        for sc, v in zip(scores, tops):
            n_ge = jnp.sum(jnp.where(sc >= v[PEER_TOPK - 1], 1.0, 0.0), axis=0, keepdims=True)
            tied = jnp.maximum(tied, jnp.where(n_ge == float(PEER_TOPK), 0.0, 1.0))
            for i in range(PEER_TOPK - 1):
                tied = jnp.maximum(tied, jnp.where(v[i] > v[i + 1], 0.0, 1.0))
        write_tables(h, scores, tops[0], tops[1],
                     lambda i, sc=scores[0], v=tops[0]: sc == v[i],
                     lambda i, sc=scores[1], v=tops[1]: sc == v[i])

    @pl.when(jnp.max(tied) > 0.0)
    def _():
        for h in range(PEER_HEADS):
            scores = head_scores(h)
            (v1, i1), (v2, i2) = [_top16_rows(sc) for sc in scores]
            write_tables(h, scores, v1, v2, lambda i, i1=i1: kio == i1[i], lambda i, i2=i2: kio == i2[i])


def _route(h2t, wqt, sk):
    d, t = h2t.shape
    spec = pl.BlockSpec((PEER_HEADS, ROUTE_TL // LANE, PEER_NKEYS, LANE), lambda i: (0, i, 0, 0))
    out = jax.ShapeDtypeStruct((PEER_HEADS, t // LANE, PEER_NKEYS, LANE), F32)
    out16 = jax.ShapeDtypeStruct((PEER_HEADS, t // LANE, PEER_NKEYS, LANE), BF16)
    return pl.pallas_call(
        _route_kernel,
        grid=(t // ROUTE_TL,),
        in_specs=[pl.BlockSpec((d, ROUTE_TL), lambda i: (0, i)),
                  pl.BlockSpec(wqt.shape, lambda i: (0, 0)),
                  pl.BlockSpec(sk.shape, lambda i: (0, 0, 0))],
        out_specs=[spec, spec, spec, spec],
        out_shape=[out, out, out16, out16],
        compiler_params=_cparams("parallel"),
        name="peer_route",
    )(h2t, wqt, sk)


def _expert_kernel(h2t_ref, u_ref, vt_ref, f_ref, cnt_ref, g_ref, rb_ref, x1_ref, mod_ref, gp_ref,
                   o_ref, acc_ref, act_ref, coef_ref):
    e = pl.program_id(1)
    n_chunks = EXP_TL // LANE
    a_per_step = EXP_TE // PEER_NKEYS
    a_group = 2

    @pl.when(e == 0)
    def _():
        acc_ref[...] = jnp.zeros_like(acc_ref)

    act = jnp.dot(u_ref[...], h2t_ref[...], preferred_element_type=F32)
    for c in range(n_chunks):
        act_ref[c] = act[:, c * LANE:(c + 1) * LANE]

    def lane_chunk(c, carry):
        for ag in range(a_per_step // a_group):
            coefs = [jnp.zeros((PEER_NKEYS, LANE), BF16) for _ in range(a_group)]

            def row_tile(ref, h, a):
                row = jnp.broadcast_to(ref[h, c, pl.ds(a, 1), :], (BF16_ROWS, LANE)).astype(BF16)
                return jnp.broadcast_to(row[None], (PEER_NKEYS // BF16_ROWS, BF16_ROWS, LANE)
                                        ).reshape(PEER_NKEYS, LANE)

            for h in range(PEER_HEADS):
                rank_b = rb_ref[h, c]
                g_b = g_ref[h, c]
                for k in range(a_group):
                    a = e * a_per_step + ag * a_group + k
                    picked = jnp.maximum(jnp.minimum(row_tile(cnt_ref, h, a) - rank_b, g_b), 0.0)
                    coefs[k] = coefs[k] + row_tile(f_ref, h, a) * picked
            for k in range(a_group):
                rows = pl.ds((ag * a_group + k) * PEER_NKEYS, PEER_NKEYS)
                coef_ref[c, rows, :] = _gelu_x2(act_ref[c, rows, :].astype(BF16)) * coefs[k]
        return carry

    lax.fori_loop(0, n_chunks, lane_chunk, 0)
    coef = jnp.concatenate([coef_ref[c] for c in range(n_chunks)], axis=1)
    acc_ref[...] += jnp.dot(vt_ref[0], coef, preferred_element_type=F32)

    @pl.when(e == pl.num_programs(1) - 1)
    def _():
        y = acc_ref[...].T
        o_ref[...] = x1_ref[...] + mod_ref[0] * _rms(y, gp_ref[...])


def _experts(h2t, u, vt, f, cnt, g, rb, x1, gt2, gpost, seq):
    d, t = h2t.shape
    n_exp = u.shape[0]
    tiles_per_seq = seq // EXP_TL
    n_chunks = EXP_TL // LANE
    route = pl.BlockSpec((PEER_HEADS, n_chunks, PEER_NKEYS, LANE), lambda i, e: (0, i, 0, 0))
    return pl.pallas_call(
        _expert_kernel,
        grid=(t // EXP_TL, n_exp // EXP_TE),
        in_specs=[pl.BlockSpec((d, EXP_TL), lambda i, e: (0, i)),
                  pl.BlockSpec((EXP_TE, d), lambda i, e: (e, 0)),
                  pl.BlockSpec((1, d, EXP_TE), lambda i, e: (e, 0, 0)),
                  route, route, route, route,
                  pl.BlockSpec((EXP_TL, d), lambda i, e: (i, 0)),
                  pl.BlockSpec((1, 1, d), lambda i, e: (i // tiles_per_seq, 0, 0)),
                  pl.BlockSpec((1, d), lambda i, e: (0, 0))],
        out_specs=pl.BlockSpec((EXP_TL, d), lambda i, e: (i, 0)),
        out_shape=jax.ShapeDtypeStruct((t, d), F32),
        scratch_shapes=[pltpu.VMEM((d, EXP_TL), F32), pltpu.VMEM((n_chunks, EXP_TE, LANE), F32),
                        pltpu.VMEM((n_chunks, EXP_TE, LANE), BF16)],
        compiler_params=_cparams("parallel", "arbitrary"),
        name="peer_experts",
    )(h2t, u, vt, f, cnt, g, rb, x1, gt2, gpost)


def _pad_heads(w, n, width):
    d = w.shape[0]
    return jnp.pad(w.reshape(d, n, width), ((0, 0), (0, 0), (0, LANE - width))).reshape(d, n * LANE)


def _split_w_in(w_in):
    q_cols = N_HEADS * D_QK
    k_cols = N_GROUPS * D_QK
    v_cols = N_GROUPS * D_V
    sizes = (q_cols, k_cols, k_cols, k_cols, v_cols, v_cols, v_cols, 3 * N_HEADS, 2 * D_MODEL, 2 * D_MODEL)
    offs = np.cumsum((0,) + sizes)
    parts = [w_in[:, offs[k]:offs[k + 1]] for k in range(len(sizes))]
    wq, wkc, wks, wkw, wvc, wvs, wvw, wgate, wglu, wmerge = parts
    w_qkv = jnp.concatenate(
        [_pad_heads(wq * (ATTN_SCALE * LOG2E), N_HEADS, D_QK)]
        + [_pad_heads(w, N_GROUPS, D_QK) for w in (wkc, wks, wkw)]
        + [_pad_heads(w, N_GROUPS, D_V) for w in (wvc, wvs, wvw)], axis=1)
    d = w_in.shape[0]
    wg = wgate.reshape(d, 3, N_GROUPS, HEADS_PER_GROUP).transpose(0, 2, 1, 3)
    wg = wg.reshape(d, N_GROUPS, 3 * HEADS_PER_GROUP)
    wg = jnp.pad(wg, ((0, 0), (0, 0), (0, LANE - 3 * HEADS_PER_GROUP))).reshape(d, N_GROUPS * LANE)
    w_rest = jnp.concatenate([wglu, wmerge, wg], axis=1)
    return w_qkv.astype(BF16), w_rest.astype(BF16)


def _cmp_weights(w1, w2, pos, dh):
    hidden = w1.shape[1]
    w1p = jnp.pad(w1.reshape(CMP_BLOCK, dh, hidden), ((0, 0), (0, LANE - dh), (0, 0)))
    w1p = w1p.reshape(2, CMP_STRIDE * LANE, hidden).astype(BF16)
    w2p = jnp.pad(w2, ((0, 0), (0, LANE - dh))).astype(BF16)
    posp = jnp.pad(pos, ((0, 0), (0, LANE - dh))).reshape(2, CMP_STRIDE * LANE)
    return w1p, w2p, posp


def kernel(x, c, w_ada, b_ada, g_pre_mix, g_post_mix, g_pre_ffn, g_post_ffn, rel_table, w_in,
           cmp_w1k, cmp_w2k, cmp_pos_k, cmp_w1v, cmp_w2v, cmp_pos_v, w_attn_out,
           conv_w, conv_b, conv_ln_g, conv_ln_b, w_conv_out, w_out,
           peer_wq, peer_subkeys, peer_u, peer_v):
    bsz, seq, d = x.shape
    t = bsz * seq
    assert d == D_MODEL and w_ada.shape[0] == 1, "single-layer block with D_MODEL channels"
    assert seq % PROJ_TM == 0 and seq // CMP_STRIDE == LANE and seq % EXP_TL == 0
    x2d = x.reshape(t, d)

    mod = _ada(c, w_ada[0], b_ada[0]).reshape(bsz, 6, d)
    mod_in = mod[:, 0:2]
    mod_tail = mod[:, 2:5]
    mod_out = mod[:, 5:6]

    bkt_c, bkt_t, bkt_w = _static_buckets(seq)
    bias_c = _bias_table(rel_table, bkt_c)
    bias_t = _bias_table(rel_table, bkt_t).reshape(N_HEADS, N_BIAS_TILES, TQ, TQ).transpose(1, 0, 2, 3)
    bias_w = _bias_table(rel_table, bkt_w)

    w_qkv, w_rest = _split_w_in(w_in[0])
    add_qkv = np.zeros((QKV_HEADS, LANE), np.float32)
    add_qkv[QKV_VS0:QKV_VS0 + N_GROUPS, D_V:] = 1.0
    add_qkv[QKV_VW0:QKV_VW0 + N_GROUPS, D_V:] = 1.0
    qkv, rest = _project(x2d, mod_in, g_pre_mix, w_qkv, jnp.asarray(add_qkv.reshape(1, -1)), w_rest,
                         seq)

    n_chunk = seq // CMP_STRIDE
    kch = qkv[:, QKV_KC0:QKV_KC0 + N_GROUPS].reshape(bsz, N_GROUPS, n_chunk, CMP_STRIDE * LANE)
    vch = qkv[:, QKV_VC0:QKV_VC0 + N_GROUPS].reshape(bsz, N_GROUPS, n_chunk, CMP_STRIDE * LANE)
    w1k, w2k, pk = _cmp_weights(cmp_w1k[0], cmp_w2k[0], cmp_pos_k[0], D_QK)
    w1v, w2v, pv = _cmp_weights(cmp_w1v[0], cmp_w2v[0], cmp_pos_v[0], D_V)
    kc, vc = _compress(kch, vch, pk, pv, w1k, w2k, w1v, w2v)

    pad = ((0, 0), (0, 0), (WINDOW, 0), (0, 0))
    kwp = jnp.pad(qkv[:, QKV_KW0:QKV_KW0 + N_GROUPS], pad)
    vwp = jnp.pad(qkv[:, QKV_VW0:QKV_VW0 + N_GROUPS], pad)

    n_idx = np.arange(LANE)[:, None] * CMP_STRIDE
    j_idx = np.arange(LANE)[None, :] * SLC_BLOCK
    overlap = ((n_idx <= j_idx + SLC_BLOCK - 1) & (n_idx + CMP_BLOCK - 1 >= j_idx)
               & (np.arange(LANE)[None, :] < seq // SLC_BLOCK)
               & (np.arange(LANE)[:, None] < n_chunk - 1)).astype(np.float32)
    blk_onehot = (np.arange(LANE)[None, :] == D_QK + np.arange(seq)[:, None] // SLC_BLOCK)
    ks = qkv[:, QKV_KS0:QKV_KS0 + N_GROUPS] + jnp.asarray(blk_onehot, dtype=BF16)
    attn = _attention(qkv, ks, kwp, vwp, kc, vc, rest, bias_c, bias_t, bias_w, jnp.asarray(overlap))

    cvec = jnp.stack([conv_b[0], conv_ln_g[0], conv_ln_b[0]])
    gvec = jnp.stack([g_post_mix[0], g_pre_ffn[0]])
    x1, h2t = _mixer_tail(attn.reshape(t, N_HEADS * D_V), rest, x2d, mod_tail,
                          w_attn_out[0].astype(BF16), w_conv_out[0].astype(BF16),
                          w_out[0].astype(BF16), conv_w[0, :, 0, :], cvec, gvec, seq)

    wqt = peer_wq[0].T.astype(BF16)
    sk = peer_subkeys[0].reshape(2 * PEER_HEADS, PEER_NKEYS, PEER_DQ // 2).astype(BF16)
    f, cnt, g, rb = _route(h2t, wqt, sk)
    n_exp = peer_v.shape[1]
    vt = peer_v[0].reshape(n_exp // EXP_TE, EXP_TE, d).transpose(0, 2, 1).astype(BF16)
    out = _experts(h2t, peer_u[0].astype(BF16), vt, f, cnt, g, rb, x1, mod_out, g_post_ffn, seq)
    return out.reshape(bsz, seq, d)
```
